```python
import math
import jax, jax.numpy as jnp
from jax import lax
import numpy as np

D_MODEL = 2048
BATCH = 1
SEQ = 8192
DEPTH = 2
DEC_BATCH = 4
DEC_SEQ = 4096
PAST_LEN = 128

N_EVEN = (DEPTH + 1) // 2
N_ODD = DEPTH // 2
W_A = D_MODEL // 2
CONV_W = 31
CONV_PAD = CONV_W // 2
W_B = D_MODEL // 2
H_B = 4
DH_B = W_B // (2 * H_B)
Q_BLOCK = 128
N_BUCKETS = 32
MAX_DIST = 128
W_C = D_MODEL // 2
G_C = 8
DG_C = W_C // G_C
W_D = D_MODEL // 2
G_D = 4
DG_D = W_D // G_D
CHUNK = 128
EPS = 1e-6

IN_EVEN = 2 * W_A + W_A + W_B + W_B + W_B + W_B
SPLIT_EVEN = (2 * W_A, 3 * W_A, 3 * W_A + W_B, 3 * W_A + 2 * W_B, 3 * W_A + 3 * W_B)
IN_ODD = W_C + W_C + W_D + W_D + W_D
SPLIT_ODD = (W_C, 2 * W_C, 2 * W_C + W_D, 2 * W_C + 2 * W_D)

kernel_name = "hybrid_conv_diffattn_fnet_gmlp_encoder"


def rmsnorm(x, g):
    xf = x.astype(jnp.float32)
    y = xf * lax.rsqrt(jnp.mean(xf * xf, axis=-1, keepdims=True) + EPS)
    return (y * g.astype(jnp.float32)).astype(x.dtype)


def layernorm(x, g, b):
    xf = x.astype(jnp.float32)
    mu = jnp.mean(xf, axis=-1, keepdims=True)
    xc = xf - mu
    y = xc * lax.rsqrt(jnp.mean(xc * xc, axis=-1, keepdims=True) + EPS)
    return (y * g.astype(jnp.float32) + b.astype(jnp.float32)).astype(x.dtype)


def t5_bucket(rel):
    nb = N_BUCKETS // 2
    ret = jnp.where(rel > 0, nb, 0)
    n = jnp.abs(rel)
    max_exact = nb // 2
    nf = jnp.maximum(n, 1).astype(jnp.float32)
    large = max_exact + (jnp.log(nf / max_exact) / math.log(MAX_DIST / max_exact)
                         * (nb - max_exact)).astype(jnp.int32)
    large = jnp.minimum(large, nb - 1)
    return ret + jnp.where(n < max_exact, n, large)


def conformer_conv(a_in, conv_w, conv_b, ln_g, ln_b):
    a = a_in[..., :W_A] * jax.nn.sigmoid(a_in[..., W_A:])
    y = lax.conv_general_dilated(
        a, conv_w[:, None, :], window_strides=(1,), padding=[(CONV_PAD, CONV_PAD)],
        dimension_numbers=("NWC", "WIO", "NWC"), feature_group_count=W_A) + conv_b
    return jax.nn.silu(layernorm(y, ln_g, ln_b))


def diff_attention(q, k, v, q_norm_g, k_norm_g, lq1, lk1, lq2, lk2, subln_g, rel_bias, lam_init):
    B, S, _ = q.shape
    q = rmsnorm(q.reshape(B, S, H_B, 2, DH_B), q_norm_g)
    k = rmsnorm(k.reshape(B, S, H_B, 2, DH_B), k_norm_g)
    q1, q2 = q[..., 0, :], q[..., 1, :]
    k1, k2 = k[..., 0, :], k[..., 1, :]
    v = v.reshape(B, S, H_B, 2 * DH_B)
    lam = (jnp.exp(jnp.sum(lq1.astype(jnp.float32) * lk1.astype(jnp.float32)))
           - jnp.exp(jnp.sum(lq2.astype(jnp.float32) * lk2.astype(jnp.float32))) + lam_init)
    scale = 1.0 / math.sqrt(DH_B)
    nblk = S // Q_BLOCK
    qb1 = q1.reshape(B, nblk, Q_BLOCK, H_B, DH_B).transpose(1, 0, 2, 3, 4)
    qb2 = q2.reshape(B, nblk, Q_BLOCK, H_B, DH_B).transpose(1, 0, 2, 3, 4)
    kpos = jnp.arange(S, dtype=jnp.int32)

    def block(args):
        q1b, q2b, i = args
        qpos = i * Q_BLOCK + jnp.arange(Q_BLOCK, dtype=jnp.int32)
        bias = rel_bias[t5_bucket(kpos[None, :] - qpos[:, None])]
        bias = bias.transpose(2, 0, 1).astype(jnp.float32)[None]
        s1 = jnp.einsum("bqhd,bkhd->bhqk", q1b, k1).astype(jnp.float32) * scale + bias
        s2 = jnp.einsum("bqhd,bkhd->bhqk", q2b, k2).astype(jnp.float32) * scale + bias
        attn = jax.nn.softmax(s1, axis=-1) - lam * jax.nn.softmax(s2, axis=-1)
        return jnp.einsum("bhqk,bkhe->bqhe", attn.astype(v.dtype), v)

    o = lax.map(block, (qb1, qb2, jnp.arange(nblk, dtype=jnp.int32)))
    o = o.transpose(1, 0, 2, 3, 4).reshape(B, S, H_B, 2 * DH_B)
    o = rmsnorm(o, subln_g) * (1.0 - lam_init)
    return o.reshape(B, S, W_B)


def fnet_mix(c):
    B, S, _ = c.shape
    cf = c.astype(jnp.float32).reshape(B, S, G_C, DG_C)
    y = jnp.real(jnp.fft.fftn(cf, axes=(1, 3), norm="ortho"))
    return y.reshape(B, S, W_C).astype(c.dtype)


def gmlp_spatial_gate(u, v, ln_g, ln_b, w_s, b_s):
    B, S, _ = u.shape
    v = layernorm(v, ln_g, ln_b).reshape(B, S // CHUNK, CHUNK, G_D, DG_D)
    sv = jnp.einsum("gpq,bnqgc->bnpgc", w_s, v) + b_s.T[None, None, :, :, None]
    return u * sv.reshape(B, S, W_D)


def trunk(x, norm_g, w_in_even, conv_w, conv_b, conv_ln_g, conv_ln_b, q_norm_g, k_norm_g,
          lam_q1, lam_k1, lam_q2, lam_k2, subln_g, rel_bias, w_out_even,
          w_in_odd, sgu_ln_g, sgu_ln_b, sgu_w, sgu_b, w_out_odd):
    for l in range(DEPTH):
        h = rmsnorm(x, norm_g[l])
        e = l // 2
        if l % 2 == 0:
            p = h @ w_in_even[e]
            a_in, a_gate, q, k, v, b_gate = jnp.split(p, SPLIT_EVEN, axis=-1)
            a_out = conformer_conv(a_in, conv_w[e], conv_b[e], conv_ln_g[e], conv_ln_b[e])
            lam_init = 0.8 - 0.6 * math.exp(-0.3 * l)
            b_out = diff_attention(q, k, v, q_norm_g[e], k_norm_g[e], lam_q1[e], lam_k1[e],
                                   lam_q2[e], lam_k2[e], subln_g[e], rel_bias, lam_init)
            mix = jnp.concatenate([a_out * jax.nn.silu(a_gate), b_out * jax.nn.silu(b_gate)], axis=-1)
            x = x + mix @ w_out_even[e]
        else:
            p = h @ w_in_odd[e]
            c_in, c_gate, u, v, d_gate = jnp.split(p, SPLIT_ODD, axis=-1)
            c_out = fnet_mix(c_in)
            d_out = gmlp_spatial_gate(u, v, sgu_ln_g[e], sgu_ln_b[e], sgu_w[e], sgu_b[e])
            mix = jnp.concatenate([c_out * jax.nn.silu(c_gate), d_out * jax.nn.silu(d_gate)], axis=-1)
            x = x + mix @ w_out_odd[e]
    return x


def setup_inputs(seed: int = 0) -> dict:
    key = jax.random.key(seed)
    ks = jax.random.split(key, 24)
    f = jnp.float32
    nrm = lambda k, shape, s: jax.random.normal(k, shape, f) * s
    return {
        "x_prompt": jax.random.normal(ks[0], (BATCH, SEQ, D_MODEL), f),
        "x_sample": jax.random.normal(ks[1], (DEC_BATCH, DEC_SEQ, D_MODEL), f),
        "norm_g": 1.0 + nrm(ks[2], (DEPTH, D_MODEL), 0.02),
        "w_in_even": nrm(ks[3], (N_EVEN, D_MODEL, IN_EVEN), D_MODEL ** -0.5),
        "conv_w": nrm(ks[4], (N_EVEN, CONV_W, W_A), CONV_W ** -0.5),
        "conv_b": nrm(ks[5], (N_EVEN, W_A), 0.02),
        "conv_ln_g": 1.0 + nrm(ks[6], (N_EVEN, W_A), 0.02),
        "conv_ln_b": nrm(ks[7], (N_EVEN, W_A), 0.02),
        "q_norm_g": 1.0 + nrm(ks[8], (N_EVEN, DH_B), 0.02),
        "k_norm_g": 1.0 + nrm(ks[9], (N_EVEN, DH_B), 0.02),
        "lam_q1": nrm(ks[10], (N_EVEN, DH_B), 0.1),
        "lam_k1": nrm(ks[11], (N_EVEN, DH_B), 0.1),
        "lam_q2": nrm(ks[12], (N_EVEN, DH_B), 0.1),
        "lam_k2": nrm(ks[13], (N_EVEN, DH_B), 0.1),
        "subln_g": 1.0 + nrm(ks[14], (N_EVEN, 2 * DH_B), 0.02),
        "rel_bias": nrm(ks[15], (N_BUCKETS, H_B), 0.5),
        "w_out_even": nrm(ks[16], (N_EVEN, W_A + W_B, D_MODEL), (W_A + W_B) ** -0.5),
        "w_in_odd": nrm(ks[17], (N_ODD, D_MODEL, IN_ODD), D_MODEL ** -0.5),
        "sgu_ln_g": 1.0 + nrm(ks[18], (N_ODD, W_D), 0.02),
        "sgu_ln_b": nrm(ks[19], (N_ODD, W_D), 0.02),
        "sgu_w": nrm(ks[20], (N_ODD, G_D, CHUNK, CHUNK), CHUNK ** -0.5),
        "sgu_b": 1.0 + nrm(ks[21], (N_ODD, G_D, CHUNK), 0.01),
        "w_out_odd": nrm(ks[22], (N_ODD, W_C + W_D, D_MODEL), (W_C + W_D) ** -0.5),
    }


def reference(x_prompt, x_sample, norm_g, w_in_even, conv_w, conv_b, conv_ln_g, conv_ln_b,
              q_norm_g, k_norm_g, lam_q1, lam_k1, lam_q2, lam_k2, subln_g, rel_bias, w_out_even,
              w_in_odd, sgu_ln_g, sgu_ln_b, sgu_w, sgu_b, w_out_odd):
    y_prompt = trunk(x_prompt, norm_g, w_in_even, conv_w, conv_b, conv_ln_g, conv_ln_b,
                     q_norm_g, k_norm_g, lam_q1, lam_k1, lam_q2, lam_k2, subln_g, rel_bias,
                     w_out_even, w_in_odd, sgu_ln_g, sgu_ln_b, sgu_w, sgu_b, w_out_odd)
    y_sample = trunk(x_sample, norm_g, w_in_even, conv_w, conv_b, conv_ln_g, conv_ln_b,
                     q_norm_g, k_norm_g, lam_q1, lam_k1, lam_q2, lam_k2, subln_g, rel_bias,
                     w_out_even, w_in_odd, sgu_ln_g, sgu_ln_b, sgu_w, sgu_b, w_out_odd)
    return (y_prompt, y_sample)
```

```python
import functools
import math

import numpy as np
import jax
import jax.numpy as jnp
from jax import lax
from jax.experimental import pallas as pl
from jax.experimental.pallas import tpu as pltpu

F32 = jnp.float32
BF16 = jnp.bfloat16

EPS = 1e-6
LOG2E = math.log2(math.e)

D_MODEL = 2048
W_HALF = D_MODEL // 2
DH = 128
N_HEADS = 4
DV = 2 * DH
CONV_W = 31
CONV_PAD = CONV_W // 2
N_BUCKETS = 32
MAX_DIST = 128
FFT_INNER = 128
GROUP_C = 128
SGU_GROUPS = 4
SGU_CHUNK = 128
SGU_DG = W_HALF // SGU_GROUPS

VMEM_LIMIT_V7X = 56 * 1024 * 1024
HALO = 16
SUBLANES = 8


def _cparams(sem):
    return pltpu.CompilerParams(dimension_semantics=sem, vmem_limit_bytes=VMEM_LIMIT_V7X)


def _silu(x):
    return x * jax.nn.sigmoid(x)


def _inproj_kernel(x_ref, g_ref, w_ref, qg_ref, kg_ref, *rest, segs, q_scale):
    nseg = len(segs)
    outs = rest[:nseg]
    h_ref = rest[nseg]
    j = pl.program_id(1)

    @pl.when(j == 0)
    def _():
        x = x_ref[...]
        ms = jnp.mean(x * x, axis=-1, keepdims=True)
        h_ref[...] = (x * lax.rsqrt(ms + EPS) * g_ref[...]).astype(BF16)

    acc = jnp.dot(h_ref[...], w_ref[...], preferred_element_type=F32)

    def qk_norm(a, gain, scale):
        parts = []
        for c in range(a.shape[1] // DH):
            blk = a[:, c * DH:(c + 1) * DH]
            ms = jnp.mean(blk * blk, axis=-1, keepdims=True)
            y = blk * lax.rsqrt(ms + EPS) * gain
            if scale != 1.0:
                y = y * scale
            parts.append(y)
        return jnp.concatenate(parts, axis=-1)

    for s, (lo, hi, kind) in enumerate(segs):
        @pl.when((j >= lo) & (j < hi))
        def _(s=s, kind=kind):
            if kind == "silu":
                y = _silu(acc)
            elif kind == "qnorm":
                y = qk_norm(acc, qg_ref[...], q_scale)
            elif kind == "knorm":
                y = qk_norm(acc, kg_ref[...], 1.0)
            else:
                y = acc
            outs[s][...] = y.astype(BF16)


def _inproj(x2d, g, w_bf16, qg, kg, seg_defs, *, tm, tn, q_scale=1.0):
    n, d = x2d.shape
    p = w_bf16.shape[1]
    segs, lo = [], 0
    for width, kind in seg_defs:
        nt = width // tn
        segs.append((lo, lo + nt, kind))
        lo += nt
    assert lo * tn == p and n % tm == 0

    def out_map(lo_, hi_):
        return lambda i, j: (i, jnp.clip(j - lo_, 0, hi_ - lo_ - 1))

    out_specs = [pl.BlockSpec((tm, tn), out_map(lo_, hi_)) for lo_, hi_, _ in segs]
    out_shape = [jax.ShapeDtypeStruct((n, w), BF16) for w, _ in seg_defs]
    kern = functools.partial(_inproj_kernel, segs=tuple(segs), q_scale=q_scale)
    return pl.pallas_call(
        kern,
        grid=(n // tm, p // tn),
        in_specs=[
            pl.BlockSpec((tm, d), lambda i, j: (i, 0)),
            pl.BlockSpec((1, d), lambda i, j: (0, 0)),
            pl.BlockSpec((d, tn), lambda i, j: (0, j)),
            pl.BlockSpec((1, DH), lambda i, j: (0, 0)),
            pl.BlockSpec((1, DH), lambda i, j: (0, 0)),
        ],
        out_specs=out_specs,
        out_shape=out_shape,
        scratch_shapes=[pltpu.VMEM((tm, d), BF16)],
        compiler_params=_cparams(("parallel", "arbitrary")),
        name="inproj",
    )(x2d, g, w_bf16, qg, kg)


def _outproj_kernel(ma_ref, mb_ref, wa_ref, wb_ref, x_ref, o_ref):
    acc = jnp.dot(ma_ref[...], wa_ref[...], preferred_element_type=F32)
    acc = acc + jnp.dot(mb_ref[...], wb_ref[...], preferred_element_type=F32)
    o_ref[...] = x_ref[...] + acc


def _outproj(mix_a, mix_b, w_bf16, x2d, *, tm):
    n, d = x2d.shape
    half = mix_a.shape[1]
    const = pl.Buffered(1)
    return pl.pallas_call(
        _outproj_kernel,
        grid=(n // tm,),
        in_specs=[
            pl.BlockSpec((tm, half), lambda i: (i, 0)),
            pl.BlockSpec((tm, half), lambda i: (i, 0)),
            pl.BlockSpec((half, d), lambda i: (0, 0), pipeline_mode=const),
            pl.BlockSpec((half, d), lambda i: (1, 0), pipeline_mode=const),
            pl.BlockSpec((tm, d), lambda i: (i, 0)),
        ],
        out_specs=pl.BlockSpec((tm, d), lambda i: (i, 0)),
        out_shape=jax.ShapeDtypeStruct((n, d), F32),
        compiler_params=_cparams(("parallel",)),
        name="outproj",
    )(mix_a, mix_b, w_bf16, w_bf16, x2d)


def _conv_kernel(lo_ref, hi_ref, plo_ref, phi_ref, nlo_ref, nhi_ref, gate_ref,
                 w_ref, b_ref, lg_ref, lb_ref, o_ref, buf_ref, y_ref, *, ts, tiles_per_seq, rc, nr):
    i = pl.program_id(0)
    first = (i % tiles_per_seq) == 0
    last = (i % tiles_per_seq) == tiles_per_seq - 1

    def glu(a, b):
        return a.astype(F32) * jax.nn.sigmoid(b.astype(F32))

    buf_ref[HALO:HALO + ts, :] = glu(lo_ref[...], hi_ref[...])
    buf_ref[0:HALO, :] = jnp.where(first, 0.0, glu(plo_ref[...], phi_ref[...]))
    buf_ref[HALO + ts:2 * HALO + ts, :] = jnp.where(last, 0.0, glu(nlo_ref[...], nhi_ref[...]))

    base0 = HALO - CONV_PAD
    c = o_ref.shape[1]
    lw = c // 2

    def conv_chunk(r, carry):
        row = pl.multiple_of(r * rc, rc)
        for col in range(0, c, lw):
            y = jnp.zeros((rc, lw), F32)
            for rr in range(SUBLANES):
                pr = jnp.zeros((rc + SUBLANES, lw), F32)
                for a in range((base0 + CONV_W - 1) // SUBLANES + 1):
                    k = SUBLANES * a + rr - base0
                    if 0 <= k < CONV_W:
                        win = buf_ref[pl.ds(row + SUBLANES * a, rc + SUBLANES), col:col + lw]
                        pr = pr + win * w_ref[k:k + 1, col:col + lw]
                if rr:
                    pr = pltpu.roll(pr, rc + SUBLANES - rr, axis=0)
                y = y + pr[:rc]
            y_ref[pl.ds(row, rc), col:col + lw] = y + b_ref[:, col:col + lw]
        return carry

    lax.fori_loop(0, ts // rc, conv_chunk, 0)

    def chunk(r, carry):
        row = pl.multiple_of(r * nr, nr)
        y = y_ref[pl.ds(row, nr), :]
        mu = jnp.mean(y, axis=-1, keepdims=True)
        yc = y - mu
        var = jnp.mean(yc * yc, axis=-1, keepdims=True)
        z = yc * lax.rsqrt(var + EPS) * lg_ref[...] + lb_ref[...]
        out = _silu(z) * gate_ref[pl.ds(row, nr), :].astype(F32)
        o_ref[pl.ds(row, nr), :] = out.astype(BF16)
        return carry

    lax.fori_loop(0, ts // nr, chunk, 0)


def _conv_module(a_in, gate, conv_w, conv_b, ln_g, ln_b, *, seq, ts, rc=32, nr=16):
    n = a_in.shape[0]
    c = gate.shape[1]
    tps = seq // ts
    hb = ts // HALO
    nhb = n // HALO

    def prev_map(col):
        return lambda i: (jnp.maximum(i * hb - 1, 0), col)

    def next_map(col):
        return lambda i: (jnp.minimum((i + 1) * hb, nhb - 1), col)

    vec = pl.BlockSpec((1, c), lambda i: (0, 0))
    kern = functools.partial(_conv_kernel, ts=ts, tiles_per_seq=tps, rc=rc, nr=nr)
    return pl.pallas_call(
        kern,
        grid=(n // ts,),
        in_specs=[
            pl.BlockSpec((ts, c), lambda i: (i, 0)),
            pl.BlockSpec((ts, c), lambda i: (i, 1)),
            pl.BlockSpec((HALO, c), prev_map(0)),
            pl.BlockSpec((HALO, c), prev_map(1)),
            pl.BlockSpec((HALO, c), next_map(0)),
            pl.BlockSpec((HALO, c), next_map(1)),
            pl.BlockSpec((ts, c), lambda i: (i, 0)),
            pl.BlockSpec((CONV_W, c), lambda i: (0, 0)),
            vec, vec, vec,
        ],
        out_specs=pl.BlockSpec((ts, c), lambda i: (i, 0)),
        out_shape=jax.ShapeDtypeStruct((n, c), BF16),
        scratch_shapes=[pltpu.VMEM((ts + 2 * HALO, c), F32), pltpu.VMEM((ts, c), F32)],
        compiler_params=_cparams(("parallel",)),
        name="conv_module",
    )(a_in, a_in, a_in, a_in, a_in, a_in, gate, conv_w, conv_b, ln_g, ln_b)


def _bucket_thresholds():
    nb = N_BUCKETS // 2
    max_exact = nb // 2
    n = np.arange(1, 4 * MAX_DIST, dtype=np.float64)
    large = max_exact + (np.log(n / max_exact) / math.log(MAX_DIST / max_exact)
                         * (nb - max_exact)).astype(np.int64)
    large = np.minimum(large, nb - 1)
    thr = [int(n[np.argmax(large >= b)]) for b in range(max_exact + 1, nb)]
    return max_exact, thr


def _bias_kernel(rb_ref, o_ref, *, tq, tk, lo_diag):
    h = pl.program_id(0)
    d = pl.program_id(1) + lo_diag
    row = lax.broadcasted_iota(jnp.int32, (tq, tk), 0)
    col = lax.broadcasted_iota(jnp.int32, (tq, tk), 1)
    rel = col - row + d * tk
    n = jnp.abs(rel)
    max_exact, thr = _bucket_thresholds()
    bucket = jnp.minimum(n, max_exact)
    for t in thr:
        bucket = bucket + jnp.where(n >= t, 1, 0)
    bucket = bucket + jnp.where(rel > 0, N_BUCKETS // 2, 0)
    val = jnp.zeros((tq, tk), F32)
    for b in range(N_BUCKETS):
        val = jnp.where(bucket == b, rb_ref[b, h], val)
    o_ref[0, 0] = val * LOG2E


def _bias_tiles(rel_bias, *, tq, tk):
    nd = tq // tk + 4
    kern = functools.partial(_bias_kernel, tq=tq, tk=tk, lo_diag=-2)
    return pl.pallas_call(
        kern,
        grid=(N_HEADS, nd),
        in_specs=[pl.BlockSpec(memory_space=pltpu.SMEM)],
        out_specs=pl.BlockSpec((1, 1, tq, tk), lambda h, d: (h, d, 0, 0)),
        out_shape=jax.ShapeDtypeStruct((N_HEADS, nd, tq, tk), F32),
        compiler_params=_cparams(("parallel", "parallel")),
        name="bias_tiles",
    )(rel_bias)


def _attn_kernel(q_ref, k_ref, v_ref, bias_ref, gate_ref, lq1_ref, lk1_ref, lq2_ref, lk2_ref,
                 sg_ref, o_ref, acc1_ref, acc2_ref, *, tq, tk, nkv, lam_init):
    i = pl.program_id(2)
    q = q_ref[...]
    q1 = q[:, :DH]
    q2 = q[:, DH:]
    acc1_ref[...] = jnp.zeros_like(acc1_ref)
    acc2_ref[...] = jnp.zeros_like(acc2_ref)
    ratio = tq // tk
    nd = ratio + 4
    contract_last = (((1,), (1,)), ((), ()))

    def one_softmax(s, m, l, acc_ref, vv):
        mn = jnp.maximum(m, jnp.max(s, axis=-1, keepdims=True))
        p = jnp.exp2(s - mn)
        a = jnp.exp2(m - mn)
        l = a * l + jnp.sum(p, axis=-1, keepdims=True)
        acc_ref[...] = a * acc_ref[...] + jnp.dot(p.astype(BF16), vv, preferred_element_type=F32)
        return mn, l

    def body(j, carry):
        m1, l1, m2, l2 = carry
        off = pl.multiple_of(j * tk, tk)
        kk = k_ref[pl.ds(off, tk), :]
        vv = v_ref[pl.ds(off, tk), :]
        d = jnp.clip(j - i * ratio, -2, ratio + 1) + 2
        bias = bias_ref[0, d]
        s1 = lax.dot_general(q1, kk[:, :DH], contract_last, preferred_element_type=F32) + bias
        s2 = lax.dot_general(q2, kk[:, DH:], contract_last, preferred_element_type=F32) + bias
        m1, l1 = one_softmax(s1, m1, l1, acc1_ref, vv)
        m2, l2 = one_softmax(s2, m2, l2, acc2_ref, vv)
        return m1, l1, m2, l2

    neg = jnp.full((tq, 1), -1e30, F32)
    zero = jnp.zeros((tq, 1), F32)
    m1, l1, m2, l2 = lax.fori_loop(0, nkv, body, (neg, zero, neg, zero))

    lam = (jnp.exp(jnp.sum(lq1_ref[...] * lk1_ref[...], axis=-1, keepdims=True))
           - jnp.exp(jnp.sum(lq2_ref[...] * lk2_ref[...], axis=-1, keepdims=True)) + lam_init)
    o = acc1_ref[...] * (1.0 / l1) - lam * (acc2_ref[...] * (1.0 / l2))
    ms = jnp.mean(o * o, axis=-1, keepdims=True)
    y = o * lax.rsqrt(ms + EPS) * sg_ref[...] * (1.0 - lam_init)
    o_ref[...] = (y * gate_ref[...].astype(F32)).astype(BF16)


def _diff_attention(q, k, v, bias_tiles, gate, lq1, lk1, lq2, lk2, subln_g, *, batch, seq, tq, tk,
                    lam_init):
    n = q.shape[0]
    nq = seq // tq
    nd = tq // tk + 4
    vec = pl.BlockSpec((1, DH), lambda b, h, i: (0, 0))
    kern = functools.partial(_attn_kernel, tq=tq, tk=tk, nkv=seq // tk, lam_init=lam_init)
    return pl.pallas_call(
        kern,
        grid=(batch, N_HEADS, nq),
        in_specs=[
            pl.BlockSpec((tq, DV), lambda b, h, i: (b * nq + i, h)),
            pl.BlockSpec((seq, DV), lambda b, h, i: (b, h)),
            pl.BlockSpec((seq, DV), lambda b, h, i: (b, h)),
            pl.BlockSpec((1, nd, tq, tk), lambda b, h, i: (h, 0, 0, 0)),
            pl.BlockSpec((tq, DV), lambda b, h, i: (b * nq + i, h)),
            vec, vec, vec, vec,
            pl.BlockSpec((1, DV), lambda b, h, i: (0, 0)),
        ],
        out_specs=pl.BlockSpec((tq, DV), lambda b, h, i: (b * nq + i, h)),
        out_shape=jax.ShapeDtypeStruct((n, N_HEADS * DV), BF16),
        scratch_shapes=[pltpu.VMEM((tq, DV), F32), pltpu.VMEM((tq, DV), F32)],
        compiler_params=_cparams(("parallel", "parallel", "parallel")),
        name="diff_attention",
    )(q, k, v, bias_tiles, gate, lq1, lk1, lq2, lk2, subln_g)


def _fft1_weights(seq):
    r = seq // FFT_INNER
    f1 = np.arange(r, dtype=np.float64)[None, :, None]
    t1 = np.arange(r, dtype=np.float64)[None, None, :]
    t2 = np.arange(FFT_INNER, dtype=np.float64)[:, None, None]
    ang = 2.0 * np.pi * f1 * (FFT_INNER * t1 + t2) / seq
    w = np.stack([np.cos(ang), -np.sin(ang)], axis=2)
    return jnp.asarray(w.reshape(FFT_INNER, 2 * r, r), dtype=BF16)


def _fft2_weights(seq):
    idx = np.arange(FFT_INNER, dtype=np.float64)
    ang = 2.0 * np.pi * np.outer(idx, idx) / FFT_INNER
    c, s = np.cos(ang), np.sin(ang)
    w2 = np.block([[c, s], [-s, c]])
    scale = 1.0 / math.sqrt(seq * GROUP_C)
    wc = np.concatenate([c, s], axis=0) * scale
    return jnp.asarray(w2, dtype=BF16), jnp.asarray(wc, dtype=BF16)


def _fft1_kernel(x_ref, w_ref, o_ref, *, t2t, c):
    for kk in range(t2t):
        xs = x_ref[0, :, kk * c:(kk + 1) * c]
        o_ref[0, :, kk * c:(kk + 1) * c] = jnp.dot(
            w_ref[kk], xs, preferred_element_type=F32).astype(BF16)


def _fft2_kernel(b_ref, w2_ref, wc_ref, gate_ref, o_ref, *, f1t, c):
    for kk in range(f1t):
        u = jnp.dot(w2_ref[...], b_ref[0, kk], preferred_element_type=F32).astype(BF16)
        ur = u[:FFT_INNER]
        ui = u[FFT_INNER:]
        for g in range(c // GROUP_C):
            sl = slice(g * GROUP_C, (g + 1) * GROUP_C)
            lhs = jnp.concatenate([ur[:, sl], ui[:, sl]], axis=-1)
            y = jnp.dot(lhs, wc_ref[...], preferred_element_type=F32)
            col = slice(kk * c + g * GROUP_C, kk * c + (g + 1) * GROUP_C)
            o_ref[0, :, col] = (y * gate_ref[0, :, col].astype(F32)).astype(BF16)


def _fnet(c_in, gate, *, batch, seq, t2t=8, f1t=4):
    c = c_in.shape[1]
    r = seq // FFT_INNER
    w1 = _fft1_weights(seq)
    w2, wc = _fft2_weights(seq)
    x = c_in.reshape(batch, r, FFT_INNER * c)
    stage1 = pl.pallas_call(
        functools.partial(_fft1_kernel, t2t=t2t, c=c),
        grid=(batch, FFT_INNER // t2t),
        in_specs=[
            pl.BlockSpec((1, r, t2t * c), lambda b, t: (b, 0, t)),
            pl.BlockSpec((t2t, 2 * r, r), lambda b, t: (t, 0, 0)),
        ],
        out_specs=pl.BlockSpec((1, 2 * r, t2t * c), lambda b, t: (b, 0, t)),
        out_shape=jax.ShapeDtypeStruct((batch, 2 * r, FFT_INNER * c), BF16),
        compiler_params=_cparams(("parallel", "parallel")),
        name="fft_stage1",
    )(x, w1)
    bmat = stage1.reshape(batch * r, 2 * FFT_INNER, c)
    g2 = gate.reshape(batch, FFT_INNER, r * c)
    const = pl.Buffered(1)
    out = pl.pallas_call(
        functools.partial(_fft2_kernel, f1t=f1t, c=c),
        grid=(batch, r // f1t),
        in_specs=[
            pl.BlockSpec((1, f1t, 2 * FFT_INNER, c), lambda b, f: (b, f, 0, 0)),
            pl.BlockSpec((2 * FFT_INNER, 2 * FFT_INNER), lambda b, f: (0, 0), pipeline_mode=const),
            pl.BlockSpec((2 * GROUP_C, GROUP_C), lambda b, f: (0, 0), pipeline_mode=const),
            pl.BlockSpec((1, FFT_INNER, f1t * c), lambda b, f: (b, 0, f)),
        ],
        out_specs=pl.BlockSpec((1, FFT_INNER, f1t * c), lambda b, f: (b, 0, f)),
        out_shape=jax.ShapeDtypeStruct((batch, FFT_INNER, r * c), BF16),
        compiler_params=_cparams(("parallel", "parallel")),
        name="fft_stage2",
    )(bmat.reshape(batch, r, 2 * FFT_INNER, c), w2, wc, g2)
    return out.reshape(batch * seq, c)


def _sgu_kernel(u_ref, v_ref, gate_ref, lg_ref, lb_ref, ws_ref, bt_ref, o_ref, *, tr):
    v = v_ref[...].astype(F32)
    mu = jnp.mean(v, axis=-1, keepdims=True)
    vc = v - mu
    var = jnp.mean(vc * vc, axis=-1, keepdims=True)
    vn = (vc * lax.rsqrt(var + EPS) * lg_ref[...] + lb_ref[...]).astype(BF16)
    for g in range(SGU_GROUPS):
        cols = slice(g * SGU_DG, (g + 1) * SGU_DG)
        bcol = jnp.broadcast_to(bt_ref[:, g:g + 1], (SGU_CHUNK, SGU_DG))
        for n in range(tr // SGU_CHUNK):
            rows = slice(n * SGU_CHUNK, (n + 1) * SGU_CHUNK)
            sv = jnp.dot(ws_ref[g], vn[rows, cols], preferred_element_type=F32) + bcol
            out = u_ref[rows, cols].astype(F32) * sv * gate_ref[rows, cols].astype(F32)
            o_ref[rows, cols] = out.astype(BF16)


def _sgu(u, v, gate, ln_g, ln_b, ws_bf16, b_t, *, tr):
    n, c = u.shape
    blk = pl.BlockSpec((tr, c), lambda i: (i, 0))
    vec = pl.BlockSpec((1, c), lambda i: (0, 0))
    return pl.pallas_call(
        functools.partial(_sgu_kernel, tr=tr),
        grid=(n // tr,),
        in_specs=[
            blk, blk, blk, vec, vec,
            pl.BlockSpec((SGU_GROUPS, SGU_CHUNK, SGU_CHUNK), lambda i: (0, 0, 0)),
            pl.BlockSpec((SGU_CHUNK, SGU_GROUPS), lambda i: (0, 0)),
        ],
        out_specs=blk,
        out_shape=jax.ShapeDtypeStruct((n, c), BF16),
        compiler_params=_cparams(("parallel",)),
        name="sgu",
    )(u, v, gate, ln_g, ln_b, ws_bf16, b_t)


EVEN_SEGS = ((2 * W_HALF, "none"), (W_HALF, "silu"), (W_HALF, "qnorm"), (W_HALF, "knorm"),
             (W_HALF, "none"), (W_HALF, "silu"))
ODD_SEGS = ((W_HALF, "none"), (W_HALF, "silu"), (W_HALF, "none"), (W_HALF, "none"),
            (W_HALF, "silu"))


def _trunk(x, p, bias_tiles, *, tq, tk):
    batch, seq, d = x.shape
    x2d = x.reshape(batch * seq, d)
    row = lambda a: a.reshape(1, -1)

    lam_init = 0.8 - 0.6 * math.exp(-0.3 * 0)
    a_in, a_gate, q, k, v, b_gate = _inproj(
        x2d, row(p["norm_g"][0]), p["w_in_even"], row(p["q_norm_g"]), row(p["k_norm_g"]),
        EVEN_SEGS, tm=512, tn=512, q_scale=LOG2E / math.sqrt(DH))
    mix_a = _conv_module(a_in, a_gate, p["conv_w"], row(p["conv_b"]), row(p["conv_ln_g"]),
                         row(p["conv_ln_b"]), seq=seq, ts=256)
    mix_b = _diff_attention(q, k, v, bias_tiles, b_gate, row(p["lam_q1"]), row(p["lam_k1"]),
                            row(p["lam_q2"]), row(p["lam_k2"]), row(p["subln_g"]),
                            batch=batch, seq=seq, tq=tq, tk=tk, lam_init=lam_init)
    x1 = _outproj(mix_a, mix_b, p["w_out_even"], x2d, tm=512)

    ones = jnp.ones((1, DH), F32)
    c_in, c_gate, u, v2, d_gate = _inproj(
        x1, row(p["norm_g"][1]), p["w_in_odd"], ones, ones, ODD_SEGS, tm=512, tn=512)
    mix_c = _fnet(c_in, c_gate, batch=batch, seq=seq)
    mix_d = _sgu(u, v2, d_gate, row(p["sgu_ln_g"]), row(p["sgu_ln_b"]), p["sgu_w"],
                 p["sgu_b"].T, tr=512)
    y = _outproj(mix_c, mix_d, p["w_out_odd"], x1, tm=512)
    return y.reshape(batch, seq, d)


def kernel(x_prompt, x_sample, norm_g, w_in_even, conv_w, conv_b, conv_ln_g, conv_ln_b,
           q_norm_g, k_norm_g, lam_q1, lam_k1, lam_q2, lam_k2, subln_g, rel_bias, w_out_even,
           w_in_odd, sgu_ln_g, sgu_ln_b, sgu_w, sgu_b, w_out_odd):
    p = dict(
        norm_g=norm_g, w_in_even=w_in_even[0].astype(BF16), conv_w=conv_w[0], conv_b=conv_b[0],
        conv_ln_g=conv_ln_g[0], conv_ln_b=conv_ln_b[0], q_norm_g=q_norm_g[0],
        k_norm_g=k_norm_g[0], lam_q1=lam_q1[0], lam_k1=lam_k1[0], lam_q2=lam_q2[0],
        lam_k2=lam_k2[0], subln_g=subln_g[0], w_out_even=w_out_even[0].astype(BF16),
        w_in_odd=w_in_odd[0].astype(BF16), sgu_ln_g=sgu_ln_g[0], sgu_ln_b=sgu_ln_b[0],
        sgu_w=sgu_w[0].astype(BF16), sgu_b=sgu_b[0], w_out_odd=w_out_odd[0].astype(BF16))
    tq = tk = 512
    bias_tiles = _bias_tiles(rel_bias, tq=tq, tk=tk)
    y_prompt = _trunk(x_prompt, p, bias_tiles, tq=tq, tk=tk)
    y_sample = _trunk(x_sample, p, bias_tiles, tq=tq, tk=tk)
    return (y_prompt, y_sample)
```

```python
import functools
import math

import numpy as np
import jax
import jax.numpy as jnp
from jax import lax
from jax.experimental import pallas as pl
from jax.experimental.pallas import tpu as pltpu

F32 = jnp.float32
BF16 = jnp.bfloat16

EPS = 1e-6
LOG2E = math.log2(math.e)

D_MODEL = 2048
W_HALF = D_MODEL // 2
DH = 128
N_HEADS = 4
DV = 2 * DH
CONV_W = 31
CONV_PAD = CONV_W // 2
N_BUCKETS = 32
MAX_DIST = 128
FFT_INNER = 128
GROUP_C = 128
SGU_GROUPS = 4
SGU_CHUNK = 128
SGU_DG = W_HALF // SGU_GROUPS

EVEN_A_LO, EVEN_A_HI, EVEN_A_GATE, EVEN_Q, EVEN_K, EVEN_V, EVEN_B_GATE = range(7)
ODD_C_IN, ODD_C_GATE, ODD_U, ODD_V, ODD_D_GATE = range(5)

VMEM_LIMIT_V7X = 56 * 1024 * 1024
HALO = 16
SUBLANES = 8


def _cparams(sem):
    return pltpu.CompilerParams(dimension_semantics=sem, vmem_limit_bytes=VMEM_LIMIT_V7X)


def _silu(x):
    return x * jax.nn.sigmoid(x)


def _inproj_kernel(x_ref, g_ref, w_ref, o_ref, h_ref):
    @pl.when(pl.program_id(1) == 0)
    def _():
        x = x_ref[...]
        ms = jnp.mean(x * x, axis=-1, keepdims=True)
        h_ref[...] = (x * lax.rsqrt(ms + EPS) * g_ref[...]).astype(BF16)

    o_ref[...] = jnp.dot(h_ref[...], w_ref[...], preferred_element_type=F32).astype(BF16)


def _inproj(x2d, g, w_bf16, *, tm, tn):
    n, d = x2d.shape
    p = w_bf16.shape[1]
    return pl.pallas_call(
        _inproj_kernel,
        grid=(n // tm, p // tn),
        in_specs=[
            pl.BlockSpec((tm, d), lambda i, j: (i, 0)),
            pl.BlockSpec((1, d), lambda i, j: (0, 0)),
            pl.BlockSpec((d, tn), lambda i, j: (0, j)),
        ],
        out_specs=pl.BlockSpec((tm, tn), lambda i, j: (i, j)),
        out_shape=jax.ShapeDtypeStruct((n, p), BF16),
        scratch_shapes=[pltpu.VMEM((tm, d), BF16)],
        compiler_params=_cparams(("parallel", "arbitrary")),
        name="inproj",
    )(x2d, g, w_bf16)


def _outproj_kernel(ma_ref, mb_ref, wa_ref, wb_ref, x_ref, o_ref):
    acc = jnp.dot(ma_ref[...], wa_ref[...], preferred_element_type=F32)
    acc = acc + jnp.dot(mb_ref[...], wb_ref[...], preferred_element_type=F32)
    o_ref[...] = x_ref[...] + acc


def _outproj(mix_a, mix_b, w_bf16, x2d, *, tm):
    n, d = x2d.shape
    half = mix_a.shape[1]
    const = pl.Buffered(1)
    return pl.pallas_call(
        _outproj_kernel,
        grid=(n // tm,),
        in_specs=[
            pl.BlockSpec((tm, half), lambda i: (i, 0)),
            pl.BlockSpec((tm, half), lambda i: (i, 0)),
            pl.BlockSpec((half, d), lambda i: (0, 0), pipeline_mode=const),
            pl.BlockSpec((half, d), lambda i: (1, 0), pipeline_mode=const),
            pl.BlockSpec((tm, d), lambda i: (i, 0)),
        ],
        out_specs=pl.BlockSpec((tm, d), lambda i: (i, 0)),
        out_shape=jax.ShapeDtypeStruct((n, d), F32),
        compiler_params=_cparams(("parallel",)),
        name="outproj",
    )(mix_a, mix_b, w_bf16, w_bf16, x2d)


def _conv_kernel(lo_ref, hi_ref, plo_ref, phi_ref, nlo_ref, nhi_ref, gate_ref,
                 w_ref, b_ref, lg_ref, lb_ref, o_ref, buf_ref, y_ref, *, ts, tiles_per_seq, rc, nr):
    i = pl.program_id(0)
    first = (i % tiles_per_seq) == 0
    last = (i % tiles_per_seq) == tiles_per_seq - 1

    def glu(a, b):
        return a.astype(F32) * jax.nn.sigmoid(b.astype(F32))

    buf_ref[HALO:HALO + ts, :] = glu(lo_ref[...], hi_ref[...])
    buf_ref[0:HALO, :] = jnp.where(first, 0.0, glu(plo_ref[...], phi_ref[...]))
    buf_ref[HALO + ts:2 * HALO + ts, :] = jnp.where(last, 0.0, glu(nlo_ref[...], nhi_ref[...]))

    base0 = HALO - CONV_PAD
    c = o_ref.shape[1]
    lw = c // 2

    def conv_chunk(r, carry):
        row = pl.multiple_of(r * rc, rc)
        for col in range(0, c, lw):
            y = jnp.zeros((rc, lw), F32)
            for rr in range(SUBLANES):
                pr = jnp.zeros((rc + SUBLANES, lw), F32)
                for a in range((base0 + CONV_W - 1) // SUBLANES + 1):
                    k = SUBLANES * a + rr - base0
                    if 0 <= k < CONV_W:
                        win = buf_ref[pl.ds(row + SUBLANES * a, rc + SUBLANES), col:col + lw]
                        pr = pr + win * w_ref[k:k + 1, col:col + lw]
                if rr:
                    pr = pltpu.roll(pr, rc + SUBLANES - rr, axis=0)
                y = y + pr[:rc]
            y_ref[pl.ds(row, rc), col:col + lw] = y + b_ref[:, col:col + lw]
        return carry

    lax.fori_loop(0, ts // rc, conv_chunk, 0)

    def chunk(r, carry):
        row = pl.multiple_of(r * nr, nr)
        y = y_ref[pl.ds(row, nr), :]
        mu = jnp.mean(y, axis=-1, keepdims=True)
        yc = y - mu
        var = jnp.mean(yc * yc, axis=-1, keepdims=True)
        z = yc * lax.rsqrt(var + EPS) * lg_ref[...] + lb_ref[...]
        out = _silu(z) * _silu(gate_ref[pl.ds(row, nr), :].astype(F32))
        o_ref[pl.ds(row, nr), :] = out.astype(BF16)
        return carry

    lax.fori_loop(0, ts // nr, chunk, 0)


def _conv_module(p, conv_w, conv_b, ln_g, ln_b, *, seq, ts, rc=32, nr=16):
    n = p.shape[0]
    c = W_HALF
    tps = seq // ts
    hb = ts // HALO
    nhb = n // HALO

    def prev_map(col):
        return lambda i: (jnp.maximum(i * hb - 1, 0), col)

    def next_map(col):
        return lambda i: (jnp.minimum((i + 1) * hb, nhb - 1), col)

    vec = pl.BlockSpec((1, c), lambda i: (0, 0))
    kern = functools.partial(_conv_kernel, ts=ts, tiles_per_seq=tps, rc=rc, nr=nr)
    return pl.pallas_call(
        kern,
        grid=(n // ts,),
        in_specs=[
            pl.BlockSpec((ts, c), lambda i: (i, EVEN_A_LO)),
            pl.BlockSpec((ts, c), lambda i: (i, EVEN_A_HI)),
            pl.BlockSpec((HALO, c), prev_map(EVEN_A_LO)),
            pl.BlockSpec((HALO, c), prev_map(EVEN_A_HI)),
            pl.BlockSpec((HALO, c), next_map(EVEN_A_LO)),
            pl.BlockSpec((HALO, c), next_map(EVEN_A_HI)),
            pl.BlockSpec((ts, c), lambda i: (i, EVEN_A_GATE)),
            pl.BlockSpec((CONV_W, c), lambda i: (0, 0)),
            vec, vec, vec,
        ],
        out_specs=pl.BlockSpec((ts, c), lambda i: (i, 0)),
        out_shape=jax.ShapeDtypeStruct((n, c), BF16),
        scratch_shapes=[pltpu.VMEM((ts + 2 * HALO, c), F32), pltpu.VMEM((ts, c), F32)],
        compiler_params=_cparams(("parallel",)),
        name="conv_module",
    )(p, p, p, p, p, p, p, conv_w, conv_b, ln_g, ln_b)


def _bucket_thresholds():
    nb = N_BUCKETS // 2
    max_exact = nb // 2
    n = np.arange(1, 4 * MAX_DIST, dtype=np.float64)
    large = max_exact + (np.log(n / max_exact) / math.log(MAX_DIST / max_exact)
                         * (nb - max_exact)).astype(np.int64)
    large = np.minimum(large, nb - 1)
    thr = [int(n[np.argmax(large >= b)]) for b in range(max_exact + 1, nb)]
    return max_exact, thr


def _bias_kernel(rb_ref, o_ref, *, tq, tk, lo_diag):
    h = pl.program_id(0)
    d = pl.program_id(1) + lo_diag
    row = lax.broadcasted_iota(jnp.int32, (tq, tk), 0)
    col = lax.broadcasted_iota(jnp.int32, (tq, tk), 1)
    rel = col - row + d * tk
    n = jnp.abs(rel)
    max_exact, thr = _bucket_thresholds()
    bucket = jnp.minimum(n, max_exact)
    for t in thr:
        bucket = bucket + jnp.where(n >= t, 1, 0)
    bucket = bucket + jnp.where(rel > 0, N_BUCKETS // 2, 0)
    val = jnp.zeros((tq, tk), F32)
    for b in range(N_BUCKETS):
        val = jnp.where(bucket == b, rb_ref[b, h], val)
    o_ref[0, 0] = val * LOG2E


N_NEAR = 3


def _bias_tiles(rel_bias, *, tq, tk):
    assert tq == tk and tk + 1 >= _bucket_thresholds()[1][-1]
    nd = N_NEAR
    kern = functools.partial(_bias_kernel, tq=tq, tk=tk, lo_diag=-(N_NEAR // 2))
    return pl.pallas_call(
        kern,
        grid=(N_HEADS, nd),
        in_specs=[pl.BlockSpec(memory_space=pltpu.SMEM)],
        out_specs=pl.BlockSpec((1, 1, tq, tk), lambda h, d: (h, d, 0, 0)),
        out_shape=jax.ShapeDtypeStruct((N_HEADS, nd, tq, tk), F32),
        compiler_params=_cparams(("parallel", "parallel")),
        name="bias_tiles",
    )(rel_bias)


def _sub_head_norm(x, gain, scale):
    parts = []
    for t in range(2):
        blk = x[:, t * DH:(t + 1) * DH]
        ms = jnp.mean(blk * blk, axis=-1, keepdims=True)
        parts.append(blk * lax.rsqrt(ms + EPS) * gain * scale)
    return parts


def _attn_kernel(rb_ref, q_ref, k_ref, v_ref, bias_ref, gate_ref, qg_ref, kg_ref, lq1_ref, lk1_ref,
                 lq2_ref, lk2_ref, sg_ref, o_ref, kn_ref, acc_ref, s_ref,
                 *, tq, tk, nkv, lam_init, q_scale):
    h = pl.program_id(1)
    i = pl.program_id(2)

    @pl.when(i == 0)
    def _():
        def norm_rows(r, carry):
            rows = pl.ds(pl.multiple_of(r * tk, tk), tk)
            k1, k2 = _sub_head_norm(k_ref[rows, :].astype(F32), kg_ref[...], 1.0)
            kn_ref[rows, :] = jnp.concatenate([k1, k2], axis=-1).astype(BF16)
            return carry
        lax.fori_loop(0, nkv, norm_rows, 0)

    qs = [y.astype(BF16) for y in _sub_head_norm(q_ref[...].astype(F32), qg_ref[...], q_scale)]
    acc_ref[...] = jnp.zeros_like(acc_ref)
    contract_last = (((1,), (1,)), ((), ()))
    far_left = rb_ref[N_BUCKETS // 2 - 1, h] * LOG2E
    far_right = rb_ref[N_BUCKETS - 1, h] * LOG2E
    near_w = N_NEAR // 2

    def scores(j, slot):
        off = pl.multiple_of(j * tk, tk)
        kk = kn_ref[pl.ds(off, tk), :]
        for t in range(2):
            s_ref[slot, t] = lax.dot_general(qs[t], kk[:, t * DH:(t + 1) * DH], contract_last,
                                             preferred_element_type=F32)

    def add_near_bias(j, slot):
        @pl.when(jnp.abs(j - i) <= near_w)
        def _():
            bias = bias_ref[0, jnp.clip(j - i, -near_w, near_w) + near_w]
            for t in range(2):
                s_ref[slot, t] = s_ref[slot, t] + bias

    def softmax_pv(j, slot, ml):
        off = pl.multiple_of(j * tk, tk)
        vv = v_ref[pl.ds(off, tk), :]
        c = jnp.where(j < i - near_w, far_left, jnp.where(j > i + near_w, far_right, 0.0))
        out = []
        for t in range(2):
            m, l = ml[2 * t], ml[2 * t + 1]
            s = s_ref[slot, t]
            mn = jnp.maximum(m, jnp.max(s, axis=-1, keepdims=True) + c)
            p = jnp.exp2(s - (mn - c))
            a = jnp.exp2(m - mn)
            l = a * l + jnp.sum(p, axis=-1, keepdims=True)
            acc_ref[t] = a * acc_ref[t] + jnp.dot(p.astype(BF16), vv, preferred_element_type=F32)
            out += [mn, l]
        return tuple(out)

    scores(0, 0)

    def body(jj, ml):
        j = 2 * jj
        add_near_bias(j, 0)
        scores(j + 1, 1)
        ml = softmax_pv(j, 0, ml)
        add_near_bias(j + 1, 1)
        scores(j + 2, 0)
        return softmax_pv(j + 1, 1, ml)

    neg = jnp.full((tq, 1), -1e30, F32)
    zero = jnp.zeros((tq, 1), F32)
    ml = lax.fori_loop(0, nkv // 2 - 1, body, (neg, zero, neg, zero))
    add_near_bias(nkv - 2, 0)
    scores(nkv - 1, 1)
    ml = softmax_pv(nkv - 2, 0, ml)
    add_near_bias(nkv - 1, 1)
    _, l1, _, l2 = softmax_pv(nkv - 1, 1, ml)

    lam = (jnp.exp(jnp.sum(lq1_ref[...] * lk1_ref[...], axis=-1, keepdims=True))
           - jnp.exp(jnp.sum(lq2_ref[...] * lk2_ref[...], axis=-1, keepdims=True)) + lam_init)
    o = acc_ref[0] * (1.0 / l1) - lam * (acc_ref[1] * (1.0 / l2))
    ms = jnp.mean(o * o, axis=-1, keepdims=True)
    y = o * lax.rsqrt(ms + EPS) * sg_ref[...] * (1.0 - lam_init)
    o_ref[...] = (y * _silu(gate_ref[...].astype(F32))).astype(BF16)


def _diff_attention(rel_bias, p, bias_tiles, q_gain, k_gain, lq1, lk1, lq2, lk2, subln_g, *, batch,
                    seq, tq, tk, lam_init):
    n = p.shape[0]
    nq = seq // tq
    nd = bias_tiles.shape[1]
    nkv = seq // tk
    assert tq == tk and nkv % 2 == 0 and nkv >= 2
    per_head = W_HALF // DV

    def q_rows(seg):
        return pl.BlockSpec((tq, DV), lambda b, h, i: (b * nq + i, seg * per_head + h))

    def kv_rows(seg):
        return pl.BlockSpec((seq, DV), lambda b, h, i: (b, seg * per_head + h))

    vec = pl.BlockSpec((1, DH), lambda b, h, i: (0, 0))
    kern = functools.partial(_attn_kernel, tq=tq, tk=tk, nkv=nkv, lam_init=lam_init,
                             q_scale=LOG2E / math.sqrt(DH))
    return pl.pallas_call(
        kern,
        grid=(batch, N_HEADS, nq),
        in_specs=[
            pl.BlockSpec(memory_space=pltpu.SMEM),
            q_rows(EVEN_Q), kv_rows(EVEN_K), kv_rows(EVEN_V),
            pl.BlockSpec((1, nd, tq, tk), lambda b, h, i: (h, 0, 0, 0)),
            q_rows(EVEN_B_GATE),
            vec, vec, vec, vec, vec, vec,
            pl.BlockSpec((1, DV), lambda b, h, i: (0, 0)),
        ],
        out_specs=pl.BlockSpec((tq, DV), lambda b, h, i: (b * nq + i, h)),
        out_shape=jax.ShapeDtypeStruct((n, N_HEADS * DV), BF16),
        scratch_shapes=[
            pltpu.VMEM((seq, DV), BF16),
            pltpu.VMEM((2, tq, DV), F32),
            pltpu.VMEM((2, 2, tq, tk), F32),
        ],
        compiler_params=_cparams(("parallel", "parallel", "arbitrary")),
        name="diff_attention",
    )(rel_bias, p, p, p, bias_tiles, p, q_gain, k_gain, lq1, lk1, lq2, lk2, subln_g)


def _fft1_weights(seq):
    r = seq // FFT_INNER
    f1 = np.arange(r, dtype=np.float64)[None, :, None]
    t1 = np.arange(r, dtype=np.float64)[None, None, :]
    t2 = np.arange(FFT_INNER, dtype=np.float64)[:, None, None]
    ang = 2.0 * np.pi * f1 * (FFT_INNER * t1 + t2) / seq
    w = np.stack([np.cos(ang), -np.sin(ang)], axis=2)
    return jnp.asarray(w.reshape(FFT_INNER, 2 * r, r), dtype=BF16)


def _fft2_weights(seq):
    idx = np.arange(FFT_INNER, dtype=np.float64)
    ang = 2.0 * np.pi * np.outer(idx, idx) / FFT_INNER
    c, s = np.cos(ang), np.sin(ang)
    w2 = np.block([[c, s], [-s, c]])
    scale = 1.0 / math.sqrt(seq * GROUP_C)
    wc = np.concatenate([c, s], axis=0) * scale
    return jnp.asarray(w2, dtype=BF16), jnp.asarray(wc, dtype=BF16)


def _fft1_kernel(x_ref, w_ref, o_ref, *, t2t, c):
    for kk in range(t2t):
        xs = x_ref[0, :, kk * c:(kk + 1) * c]
        o_ref[0, :, kk * c:(kk + 1) * c] = jnp.dot(
            w_ref[kk], xs, preferred_element_type=F32).astype(BF16)


def _fft2_kernel(b_ref, w2_ref, wc_ref, gate_ref, o_ref, *, f1t, c):
    for kk in range(f1t):
        u = jnp.dot(w2_ref[...], b_ref[0, kk], preferred_element_type=F32).astype(BF16)
        ur = u[:FFT_INNER]
        ui = u[FFT_INNER:]
        for g in range(c // GROUP_C):
            sl = slice(g * GROUP_C, (g + 1) * GROUP_C)
            lhs = jnp.concatenate([ur[:, sl], ui[:, sl]], axis=-1)
            y = jnp.dot(lhs, wc_ref[...], preferred_element_type=F32)
            col = slice(kk * c + g * GROUP_C, kk * c + (g + 1) * GROUP_C)
            o_ref[0, :, col] = (y * _silu(gate_ref[0, :, col].astype(F32))).astype(BF16)


def _fnet(c_in, gate, *, batch, seq, t2t=8, f1t=4):
    c = c_in.shape[1]
    r = seq // FFT_INNER
    w1 = _fft1_weights(seq)
    w2, wc = _fft2_weights(seq)
    x = c_in.reshape(batch, r, FFT_INNER * c)
    stage1 = pl.pallas_call(
        functools.partial(_fft1_kernel, t2t=t2t, c=c),
        grid=(batch, FFT_INNER // t2t),
        in_specs=[
            pl.BlockSpec((1, r, t2t * c), lambda b, t: (b, 0, t)),
            pl.BlockSpec((t2t, 2 * r, r), lambda b, t: (t, 0, 0)),
        ],
        out_specs=pl.BlockSpec((1, 2 * r, t2t * c), lambda b, t: (b, 0, t)),
        out_shape=jax.ShapeDtypeStruct((batch, 2 * r, FFT_INNER * c), BF16),
        compiler_params=_cparams(("parallel", "parallel")),
        name="fft_stage1",
    )(x, w1)
    bmat = stage1.reshape(batch, r, 2 * FFT_INNER, c)
    g2 = gate.reshape(batch, FFT_INNER, r * c)
    const = pl.Buffered(1)
    out = pl.pallas_call(
        functools.partial(_fft2_kernel, f1t=f1t, c=c),
        grid=(batch, r // f1t),
        in_specs=[
            pl.BlockSpec((1, f1t, 2 * FFT_INNER, c), lambda b, f: (b, f, 0, 0)),
            pl.BlockSpec((2 * FFT_INNER, 2 * FFT_INNER), lambda b, f: (0, 0), pipeline_mode=const),
            pl.BlockSpec((2 * GROUP_C, GROUP_C), lambda b, f: (0, 0), pipeline_mode=const),
            pl.BlockSpec((1, FFT_INNER, f1t * c), lambda b, f: (b, 0, f)),
        ],
        out_specs=pl.BlockSpec((1, FFT_INNER, f1t * c), lambda b, f: (b, 0, f)),
        out_shape=jax.ShapeDtypeStruct((batch, FFT_INNER, r * c), BF16),
        compiler_params=_cparams(("parallel", "parallel")),
        name="fft_stage2",
    )(bmat, w2, wc, g2)
    return out.reshape(batch * seq, c)


def _sgu_kernel(u_ref, v_ref, gate_ref, lg_ref, lb_ref, ws_ref, bt_ref, o_ref, *, tr):
    v = v_ref[...].astype(F32)
    mu = jnp.mean(v, axis=-1, keepdims=True)
    vc = v - mu
    var = jnp.mean(vc * vc, axis=-1, keepdims=True)
    vn = (vc * lax.rsqrt(var + EPS) * lg_ref[...] + lb_ref[...]).astype(BF16)
    for g in range(SGU_GROUPS):
        cols = slice(g * SGU_DG, (g + 1) * SGU_DG)
        bcol = jnp.broadcast_to(bt_ref[:, g:g + 1], (SGU_CHUNK, SGU_DG))
        for n in range(tr // SGU_CHUNK):
            rows = slice(n * SGU_CHUNK, (n + 1) * SGU_CHUNK)
            sv = jnp.dot(ws_ref[g], vn[rows, cols], preferred_element_type=F32) + bcol
            out = u_ref[rows, cols].astype(F32) * sv * _silu(gate_ref[rows, cols].astype(F32))
            o_ref[rows, cols] = out.astype(BF16)


def _sgu(p, ln_g, ln_b, ws_bf16, b_t, *, tr):
    n = p.shape[0]
    c = W_HALF

    def seg(s):
        return pl.BlockSpec((tr, c), lambda i: (i, s))

    vec = pl.BlockSpec((1, c), lambda i: (0, 0))
    return pl.pallas_call(
        functools.partial(_sgu_kernel, tr=tr),
        grid=(n // tr,),
        in_specs=[
            seg(ODD_U), seg(ODD_V), seg(ODD_D_GATE), vec, vec,
            pl.BlockSpec((SGU_GROUPS, SGU_CHUNK, SGU_CHUNK), lambda i: (0, 0, 0)),
            pl.BlockSpec((SGU_CHUNK, SGU_GROUPS), lambda i: (0, 0)),
        ],
        out_specs=pl.BlockSpec((tr, c), lambda i: (i, 0)),
        out_shape=jax.ShapeDtypeStruct((n, c), BF16),
        compiler_params=_cparams(("parallel",)),
        name="sgu",
    )(p, p, p, ln_g, ln_b, ws_bf16, b_t)


def _trunk(x, p, bias_tiles, *, tq, tk):
    batch, seq, d = x.shape
    x2d = x.reshape(batch * seq, d)
    row = lambda a: a.reshape(1, -1)

    lam_init = 0.8 - 0.6 * math.exp(-0.3 * 0)
    pe = _inproj(x2d, row(p["norm_g"][0]), p["w_in_even"], tm=1024, tn=512)
    mix_a = _conv_module(pe, p["conv_w"], row(p["conv_b"]), row(p["conv_ln_g"]),
                         row(p["conv_ln_b"]), seq=seq, ts=256)
    mix_b = _diff_attention(p["rel_bias"], pe, bias_tiles, row(p["q_norm_g"]), row(p["k_norm_g"]),
                            row(p["lam_q1"]), row(p["lam_k1"]), row(p["lam_q2"]), row(p["lam_k2"]),
                            row(p["subln_g"]), batch=batch, seq=seq, tq=tq, tk=tk,
                            lam_init=lam_init)
    x1 = _outproj(mix_a, mix_b, p["w_out_even"], x2d, tm=512)

    po = _inproj(x1, row(p["norm_g"][1]), p["w_in_odd"], tm=1024, tn=512)
    c_in = po[:, ODD_C_IN * W_HALF:(ODD_C_IN + 1) * W_HALF]
    c_gate = po[:, ODD_C_GATE * W_HALF:(ODD_C_GATE + 1) * W_HALF]
    mix_c = _fnet(c_in, c_gate, batch=batch, seq=seq)
    mix_d = _sgu(po, row(p["sgu_ln_g"]), row(p["sgu_ln_b"]), p["sgu_w"], p["sgu_b"].T, tr=512)
    y = _outproj(mix_c, mix_d, p["w_out_odd"], x1, tm=512)
    return y.reshape(batch, seq, d)


def kernel(x_prompt, x_sample, norm_g, w_in_even, conv_w, conv_b, conv_ln_g, conv_ln_b,
           q_norm_g, k_norm_g, lam_q1, lam_k1, lam_q2, lam_k2, subln_g, rel_bias, w_out_even,
           w_in_odd, sgu_ln_g, sgu_ln_b, sgu_w, sgu_b, w_out_odd):
    p = dict(
        norm_g=norm_g, w_in_even=w_in_even[0].astype(BF16), conv_w=conv_w[0], conv_b=conv_b[0],
        conv_ln_g=conv_ln_g[0], conv_ln_b=conv_ln_b[0], q_norm_g=q_norm_g[0],
        k_norm_g=k_norm_g[0], lam_q1=lam_q1[0], lam_k1=lam_k1[0], lam_q2=lam_q2[0],
        lam_k2=lam_k2[0], subln_g=subln_g[0], w_out_even=w_out_even[0].astype(BF16),
        w_in_odd=w_in_odd[0].astype(BF16), sgu_ln_g=sgu_ln_g[0], sgu_ln_b=sgu_ln_b[0],
        sgu_w=sgu_w[0].astype(BF16), sgu_b=sgu_b[0], w_out_odd=w_out_odd[0].astype(BF16),
        rel_bias=rel_bias)
    tq = tk = 512
    bias_tiles = _bias_tiles(rel_bias, tq=tq, tk=tk)
    y_prompt = _trunk(x_prompt, p, bias_tiles, tq=tq, tk=tk)
    y_sample = _trunk(x_sample, p, bias_tiles, tq=tq, tk=tk)
    return (y_prompt, y_sample)
```

```python
import functools
import math

import numpy as np
import jax
import jax.numpy as jnp
from jax import lax
from jax.experimental import pallas as pl
from jax.experimental.pallas import tpu as pltpu

F32 = jnp.float32
BF16 = jnp.bfloat16

EPS = 1e-6
LOG2E = math.log2(math.e)

D_MODEL = 2048
W_HALF = D_MODEL // 2
DH = 128
N_HEADS = 4
DV = 2 * DH
CONV_W = 31
CONV_PAD = CONV_W // 2
N_BUCKETS = 32
MAX_DIST = 128
FFT_INNER = 128
GROUP_C = 128
SGU_GROUPS = 4
SGU_CHUNK = 128
SGU_DG = W_HALF // SGU_GROUPS

EVEN_A_LO, EVEN_A_HI, EVEN_A_GATE, EVEN_Q, EVEN_K, EVEN_V, EVEN_B_GATE = range(7)
ODD_C_IN, ODD_C_GATE, ODD_U, ODD_V, ODD_D_GATE = range(5)

VMEM_LIMIT_V7X = 56 * 1024 * 1024
HALO = 16
SUBLANES = 8


def _cparams(sem):
    return pltpu.CompilerParams(dimension_semantics=sem, vmem_limit_bytes=VMEM_LIMIT_V7X)


def _sigmoid(x):
    return 0.5 * jnp.tanh(0.5 * x) + 0.5


def _silu(x):
    return x * _sigmoid(x)


def _inproj_kernel(x_ref, g_ref, w_ref, o_ref, h_ref):
    @pl.when(pl.program_id(1) == 0)
    def _():
        x = x_ref[...]
        ms = jnp.mean(x * x, axis=-1, keepdims=True)
        h_ref[...] = (x * lax.rsqrt(ms + EPS) * g_ref[...]).astype(BF16)

    o_ref[...] = jnp.dot(h_ref[...], w_ref[...], preferred_element_type=F32).astype(BF16)


def _inproj(x2d, g, w_bf16, *, tm, tn):
    n, d = x2d.shape
    p = w_bf16.shape[1]
    return pl.pallas_call(
        _inproj_kernel,
        grid=(n // tm, p // tn),
        in_specs=[
            pl.BlockSpec((tm, d), lambda i, j: (i, 0)),
            pl.BlockSpec((1, d), lambda i, j: (0, 0)),
            pl.BlockSpec((d, tn), lambda i, j: (0, j)),
        ],
        out_specs=pl.BlockSpec((tm, tn), lambda i, j: (i, j)),
        out_shape=jax.ShapeDtypeStruct((n, p), BF16),
        scratch_shapes=[pltpu.VMEM((tm, d), BF16)],
        compiler_params=_cparams(("parallel", "arbitrary")),
        name="inproj",
    )(x2d, g, w_bf16)


def _outproj_kernel(ma_ref, mb_ref, wa_ref, wb_ref, x_ref, o_ref):
    acc = jnp.dot(ma_ref[...], wa_ref[...], preferred_element_type=F32)
    acc = acc + jnp.dot(mb_ref[...], wb_ref[...], preferred_element_type=F32)
    o_ref[...] = x_ref[...] + acc


def _outproj(mix_a, mix_b, w_bf16, x2d, *, tm):
    n, d = x2d.shape
    half = mix_a.shape[1]
    const = pl.Buffered(1)
    return pl.pallas_call(
        _outproj_kernel,
        grid=(n // tm,),
        in_specs=[
            pl.BlockSpec((tm, half), lambda i: (i, 0)),
            pl.BlockSpec((tm, half), lambda i: (i, 0)),
            pl.BlockSpec((half, d), lambda i: (0, 0), pipeline_mode=const),
            pl.BlockSpec((half, d), lambda i: (1, 0), pipeline_mode=const),
            pl.BlockSpec((tm, d), lambda i: (i, 0)),
        ],
        out_specs=pl.BlockSpec((tm, d), lambda i: (i, 0)),
        out_shape=jax.ShapeDtypeStruct((n, d), F32),
        compiler_params=_cparams(("parallel",)),
        name="outproj",
    )(mix_a, mix_b, w_bf16, w_bf16, x2d)


def _conv_kernel(lo_ref, hi_ref, plo_ref, phi_ref, nlo_ref, nhi_ref, gate_ref,
                 w_ref, b_ref, lg_ref, lb_ref, o_ref, buf_ref, y_ref, *, ts, tiles_per_seq, rc, nr):
    i = pl.program_id(0)
    first = (i % tiles_per_seq) == 0
    last = (i % tiles_per_seq) == tiles_per_seq - 1

    def glu(a, b):
        return a.astype(F32) * _sigmoid(b.astype(F32))

    buf_ref[HALO:HALO + ts, :] = glu(lo_ref[...], hi_ref[...])
    buf_ref[0:HALO, :] = jnp.where(first, 0.0, glu(plo_ref[...], phi_ref[...]))
    buf_ref[HALO + ts:2 * HALO + ts, :] = jnp.where(last, 0.0, glu(nlo_ref[...], nhi_ref[...]))

    base0 = HALO - CONV_PAD
    c = o_ref.shape[1]
    lw = c // 2

    def conv_chunk(r, carry):
        row = pl.multiple_of(r * rc, rc)
        for col in range(0, c, lw):
            y = jnp.zeros((rc, lw), F32)
            for rr in range(SUBLANES):
                pr = jnp.zeros((rc + SUBLANES, lw), F32)
                for a in range((base0 + CONV_W - 1) // SUBLANES + 1):
                    k = SUBLANES * a + rr - base0
                    if 0 <= k < CONV_W:
                        win = buf_ref[pl.ds(row + SUBLANES * a, rc + SUBLANES), col:col + lw]
                        pr = pr + win * w_ref[k:k + 1, col:col + lw]
                if rr:
                    pr = pltpu.roll(pr, rc + SUBLANES - rr, axis=0)
                y = y + pr[:rc]
            y_ref[pl.ds(row, rc), col:col + lw] = y + b_ref[:, col:col + lw]
        return carry

    lax.fori_loop(0, ts // rc, conv_chunk, 0)

    def chunk(r, carry):
        row = pl.multiple_of(r * nr, nr)
        y = y_ref[pl.ds(row, nr), :]
        mu = jnp.mean(y, axis=-1, keepdims=True)
        yc = y - mu
        var = jnp.mean(yc * yc, axis=-1, keepdims=True)
        z = yc * lax.rsqrt(var + EPS) * lg_ref[...] + lb_ref[...]
        out = _silu(z) * _silu(gate_ref[pl.ds(row, nr), :].astype(F32))
        o_ref[pl.ds(row, nr), :] = out.astype(BF16)
        return carry

    lax.fori_loop(0, ts // nr, chunk, 0, unroll=2)


def _conv_module(p, conv_w, conv_b, ln_g, ln_b, *, seq, ts, rc=32, nr=32):
    n = p.shape[0]
    c = W_HALF
    tps = seq // ts
    hb = ts // HALO
    nhb = n // HALO

    def prev_map(col):
        return lambda i: (jnp.maximum(i * hb - 1, 0), col)

    def next_map(col):
        return lambda i: (jnp.minimum((i + 1) * hb, nhb - 1), col)

    vec = pl.BlockSpec((1, c), lambda i: (0, 0))
    kern = functools.partial(_conv_kernel, ts=ts, tiles_per_seq=tps, rc=rc, nr=nr)
    return pl.pallas_call(
        kern,
        grid=(n // ts,),
        in_specs=[
            pl.BlockSpec((ts, c), lambda i: (i, EVEN_A_LO)),
            pl.BlockSpec((ts, c), lambda i: (i, EVEN_A_HI)),
            pl.BlockSpec((HALO, c), prev_map(EVEN_A_LO)),
            pl.BlockSpec((HALO, c), prev_map(EVEN_A_HI)),
            pl.BlockSpec((HALO, c), next_map(EVEN_A_LO)),
            pl.BlockSpec((HALO, c), next_map(EVEN_A_HI)),
            pl.BlockSpec((ts, c), lambda i: (i, EVEN_A_GATE)),
            pl.BlockSpec((CONV_W, c), lambda i: (0, 0)),
            vec, vec, vec,
        ],
        out_specs=pl.BlockSpec((ts, c), lambda i: (i, 0)),
        out_shape=jax.ShapeDtypeStruct((n, c), BF16),
        scratch_shapes=[pltpu.VMEM((ts + 2 * HALO, c), F32), pltpu.VMEM((ts, c), F32)],
        compiler_params=_cparams(("parallel",)),
        name="conv_module",
    )(p, p, p, p, p, p, p, conv_w, conv_b, ln_g, ln_b)


def _bucket_thresholds():
    nb = N_BUCKETS // 2
    max_exact = nb // 2
    n = np.arange(1, 4 * MAX_DIST, dtype=np.float64)
    large = max_exact + (np.log(n / max_exact) / math.log(MAX_DIST / max_exact)
                         * (nb - max_exact)).astype(np.int64)
    large = np.minimum(large, nb - 1)
    thr = [int(n[np.argmax(large >= b)]) for b in range(max_exact + 1, nb)]
    return max_exact, thr


def _bias_kernel(rb_ref, o_ref, *, tq, tk, lo_diag):
    h = pl.program_id(0)
    d = pl.program_id(1) + lo_diag
    row = lax.broadcasted_iota(jnp.int32, (tq, tk), 0)
    col = lax.broadcasted_iota(jnp.int32, (tq, tk), 1)
    rel = col - row + d * tk
    n = jnp.abs(rel)
    max_exact, thr = _bucket_thresholds()
    bucket = jnp.minimum(n, max_exact)
    for t in thr:
        bucket = bucket + jnp.where(n >= t, 1, 0)
    bucket = bucket + jnp.where(rel > 0, N_BUCKETS // 2, 0)
    val = jnp.zeros((tq, tk), F32)
    for b in range(N_BUCKETS):
        val = jnp.where(bucket == b, rb_ref[b, h], val)
    o_ref[0, 0] = val * LOG2E


EXP2_SAFE_RANGE = 100.0
N_DIAG = 5
FAR = N_DIAG // 2


def _bias_tiles(rel_bias, *, tq, tk):
    assert tq == tk and tk + 1 >= _bucket_thresholds()[1][-1]
    nd = N_DIAG
    kern = functools.partial(_bias_kernel, tq=tq, tk=tk, lo_diag=-FAR)
    return pl.pallas_call(
        kern,
        grid=(N_HEADS, nd),
        in_specs=[pl.BlockSpec(memory_space=pltpu.SMEM)],
        out_specs=pl.BlockSpec((1, 1, tq, tk), lambda h, d: (h, d, 0, 0)),
        out_shape=jax.ShapeDtypeStruct((N_HEADS, nd, tq, tk), F32),
        compiler_params=_cparams(("parallel", "parallel")),
        name="bias_tiles",
    )(rel_bias)


def _sub_head_norm(x, gain, scale):
    parts = []
    for t in range(2):
        blk = x[:, t * DH:(t + 1) * DH]
        ms = jnp.mean(blk * blk, axis=-1, keepdims=True)
        parts.append(blk * lax.rsqrt(ms + EPS) * gain * scale)
    return parts


def _attn_kernel(rb_ref, q_ref, k_ref, v_ref, bias_ref, gate_ref, qg_ref, kg_ref, lq1_ref, lk1_ref,
                 lq2_ref, lk2_ref, sg_ref, o_ref, kx_ref, kmax_ref, qx_ref, acc_ref, ls_ref,
                 *, tq, tk, nkv, lam_init, q_scale, unroll):
    h = pl.program_id(1)
    i = pl.program_id(2)
    contract_last = (((1,), (1,)), ((), ()))
    unit_col = jnp.where(lax.broadcasted_iota(jnp.int32, (tk, DH), 1) == 0, 1.0, 0.0)

    @pl.when(i == 0)
    def _():
        def norm_rows(r, kmax2):
            rows = pl.ds(pl.multiple_of(r * tk, tk), tk)
            ks = _sub_head_norm(k_ref[rows, :].astype(F32), kg_ref[...], 1.0)
            out = []
            for t in range(2):
                kb = ks[t].astype(BF16)
                kx_ref[t, rows, :DH] = kb
                kx_ref[t, rows, DH:] = unit_col.astype(BF16)
                kf = kb.astype(F32)
                n2 = jnp.sum(kf * kf, axis=-1, keepdims=True)
                out.append(jnp.maximum(kmax2[t], jnp.max(n2, axis=0, keepdims=True)))
            return tuple(out)
        zero11 = jnp.zeros((1, 1), F32)
        k2 = lax.fori_loop(0, nkv, norm_rows, (zero11, zero11))
        for t in range(2):
            kmax_ref[t] = jnp.sqrt(k2[t])

    bmax = rb_ref[0, h]
    bmin = rb_ref[0, h]
    for b in range(1, N_BUCKETS):
        bmax = jnp.maximum(bmax, rb_ref[b, h])
        bmin = jnp.minimum(bmin, rb_ref[b, h])
    bmax = bmax * LOG2E
    bmin = bmin * LOG2E

    qn = _sub_head_norm(q_ref[...].astype(F32), qg_ref[...], q_scale)
    spread = jnp.zeros((1, 1), F32)
    for t in range(2):
        qb = qn[t].astype(BF16)
        qf = qb.astype(F32)
        bound = jnp.sqrt(jnp.sum(qf * qf, axis=-1, keepdims=True)) * kmax_ref[t]
        qx_ref[t, :, :DH] = qb
        qx_ref[t, :, DH:] = (unit_col * (-(bound + bmax))).astype(BF16)
        spread = jnp.maximum(spread, jnp.max(bound, axis=0, keepdims=True))
    lowest_exponent = 2.0 * spread[0, 0] + (bmax - bmin)
    acc_ref[...] = jnp.zeros_like(acc_ref)

    def shifted_scores(j, t):
        rows = pl.ds(pl.multiple_of(j * tk, tk), tk)
        bias = bias_ref[0, jnp.clip(j - i, -FAR, FAR) + FAR]
        return lax.dot_general(qx_ref[t], kx_ref[t, rows, :], contract_last,
                               preferred_element_type=F32) + bias

    def bounded():
        ls_ref[...] = jnp.zeros_like(ls_ref)

        def body(jj, carry):
            for u in range(unroll):
                j = unroll * jj + u
                vv = v_ref[pl.ds(pl.multiple_of(j * tk, tk), tk), :]
                for t in range(2):
                    p = jnp.exp2(shifted_scores(j, t))
                    part = p[:, :DH]
                    for c in range(1, tk // DH):
                        part = part + p[:, c * DH:(c + 1) * DH]
                    ls_ref[t] = ls_ref[t] + part
                    acc_ref[t] = acc_ref[t] + jnp.dot(p.astype(BF16), vv,
                                                      preferred_element_type=F32)
            return carry

        lax.fori_loop(0, nkv // unroll, body, 0)
        return tuple(jnp.sum(ls_ref[t], axis=-1, keepdims=True) for t in range(2))

    def running_max():
        def body(j, ml):
            vv = v_ref[pl.ds(pl.multiple_of(j * tk, tk), tk), :]
            out = []
            for t in range(2):
                m, l = ml[2 * t], ml[2 * t + 1]
                s = shifted_scores(j, t)
                mn = jnp.maximum(m, jnp.max(s, axis=-1, keepdims=True))
                p = jnp.exp2(s - mn)
                a = jnp.exp2(m - mn)
                l = a * l + jnp.sum(p, axis=-1, keepdims=True)
                acc_ref[t] = a * acc_ref[t] + jnp.dot(p.astype(BF16), vv,
                                                      preferred_element_type=F32)
                out += [mn, l]
            return tuple(out)

        neg = jnp.full((tq, 1), -jnp.inf, F32)
        zero = jnp.zeros((tq, 1), F32)
        _, l1, _, l2 = lax.fori_loop(0, nkv, body, (neg, zero, neg, zero))
        return l1, l2

    l1, l2 = lax.cond(lowest_exponent <= EXP2_SAFE_RANGE, bounded, running_max)

    lam =(jnp.exp(jnp.sum(lq1_ref[...] * lk1_ref[...], axis=-1, keepdims=True))
           - jnp.exp(jnp.sum(lq2_ref[...] * lk2_ref[...], axis=-1, keepdims=True)) + lam_init)
    o = acc_ref[0] * (1.0 / l1) - lam * (acc_ref[1] * (1.0 / l2))
    ms = jnp.mean(o * o, axis=-1, keepdims=True)
    y = o * lax.rsqrt(ms + EPS) * sg_ref[...] * (1.0 - lam_init)
    o_ref[...] = (y * _silu(gate_ref[...].astype(F32))).astype(BF16)


def _diff_attention(rel_bias, p, bias_tiles, q_gain, k_gain, lq1, lk1, lq2, lk2, subln_g, *, batch,
                    seq, tq, tk, lam_init):
    n = p.shape[0]
    nq = seq // tq
    nd = bias_tiles.shape[1]
    nkv = seq // tk
    unroll = 4
    assert tq == tk and nkv % unroll == 0
    per_head = W_HALF // DV

    def q_rows(seg):
        return pl.BlockSpec((tq, DV), lambda b, h, i: (b * nq + i, seg * per_head + h))

    def kv_rows(seg):
        return pl.BlockSpec((seq, DV), lambda b, h, i: (b, seg * per_head + h))

    vec = pl.BlockSpec((1, DH), lambda b, h, i: (0, 0))
    kern = functools.partial(_attn_kernel, tq=tq, tk=tk, nkv=nkv, lam_init=lam_init,
                             q_scale=LOG2E / math.sqrt(DH), unroll=unroll)
    return pl.pallas_call(
        kern,
        grid=(batch, N_HEADS, nq),
        in_specs=[
            pl.BlockSpec(memory_space=pltpu.SMEM),
            q_rows(EVEN_Q), kv_rows(EVEN_K), kv_rows(EVEN_V),
            pl.BlockSpec((1, nd, tq, tk), lambda b, h, i: (h, 0, 0, 0)),
            q_rows(EVEN_B_GATE),
            vec, vec, vec, vec, vec, vec,
            pl.BlockSpec((1, DV), lambda b, h, i: (0, 0)),
        ],
        out_specs=pl.BlockSpec((tq, DV), lambda b, h, i: (b * nq + i, h)),
        out_shape=jax.ShapeDtypeStruct((n, N_HEADS * DV), BF16),
        scratch_shapes=[
            pltpu.VMEM((2, seq, 2 * DH), BF16),
            pltpu.VMEM((2, 1, 1), F32),
            pltpu.VMEM((2, tq, 2 * DH), BF16),
            pltpu.VMEM((2, tq, DV), F32),
            pltpu.VMEM((2, tq, DH), F32),
        ],
        compiler_params=_cparams(("parallel", "parallel", "arbitrary")),
        name="diff_attention",
    )(rel_bias, p, p, p, bias_tiles, p, q_gain, k_gain, lq1, lk1, lq2, lk2, subln_g)


def _fft1_weights(seq):
    r = seq // FFT_INNER
    f1 = np.arange(r, dtype=np.float64)[None, :, None]
    t1 = np.arange(r, dtype=np.float64)[None, None, :]
    t2 = np.arange(FFT_INNER, dtype=np.float64)[:, None, None]
    ang = 2.0 * np.pi * f1 * (FFT_INNER * t1 + t2) / seq
    w = np.stack([np.cos(ang), -np.sin(ang)], axis=2)
    return jnp.asarray(w.reshape(FFT_INNER, 2 * r, r), dtype=BF16)


def _fft2_weights(seq):
    idx = np.arange(FFT_INNER, dtype=np.float64)
    ang = 2.0 * np.pi * np.outer(idx, idx) / FFT_INNER
    c, s = np.cos(ang), np.sin(ang)
    w2 = np.block([[c, s], [-s, c]])
    scale = 1.0 / math.sqrt(seq * GROUP_C)
    wc = np.concatenate([c, s], axis=0) * scale
    return jnp.asarray(w2, dtype=BF16), jnp.asarray(wc, dtype=BF16)


def _fft1_kernel(x_ref, w_ref, o_ref, *, t2t, c):
    for kk in range(t2t):
        xs = x_ref[0, :, kk * c:(kk + 1) * c]
        o_ref[0, :, kk * c:(kk + 1) * c] = jnp.dot(
            w_ref[kk], xs, preferred_element_type=F32).astype(BF16)


def _fft2_kernel(b_ref, w2_ref, wc_ref, gate_ref, o_ref, *, f1t, c):
    for kk in range(f1t):
        u = jnp.dot(w2_ref[...], b_ref[0, kk], preferred_element_type=F32).astype(BF16)
        ur = u[:FFT_INNER]
        ui = u[FFT_INNER:]
        for g in range(c // GROUP_C):
            sl = slice(g * GROUP_C, (g + 1) * GROUP_C)
            lhs = jnp.concatenate([ur[:, sl], ui[:, sl]], axis=-1)
            y = jnp.dot(lhs, wc_ref[...], preferred_element_type=F32)
            col = slice(kk * c + g * GROUP_C, kk * c + (g + 1) * GROUP_C)
            o_ref[0, :, col] = (y * _silu(gate_ref[0, :, col].astype(F32))).astype(BF16)


def _fnet(c_in, gate, *, batch, seq, t2t=8, f1t=4):
    c = c_in.shape[1]
    r = seq // FFT_INNER
    w1 = _fft1_weights(seq)
    w2, wc = _fft2_weights(seq)
    x = c_in.reshape(batch, r, FFT_INNER * c)
    stage1 = pl.pallas_call(
        functools.partial(_fft1_kernel, t2t=t2t, c=c),
        grid=(batch, FFT_INNER // t2t),
        in_specs=[
            pl.BlockSpec((1, r, t2t * c), lambda b, t: (b, 0, t)),
            pl.BlockSpec((t2t, 2 * r, r), lambda b, t: (t, 0, 0)),
        ],
        out_specs=pl.BlockSpec((1, 2 * r, t2t * c), lambda b, t: (b, 0, t)),
        out_shape=jax.ShapeDtypeStruct((batch, 2 * r, FFT_INNER * c), BF16),
        compiler_params=_cparams(("parallel", "parallel")),
        name="fft_stage1",
    )(x, w1)
    bmat = stage1.reshape(batch, r, 2 * FFT_INNER, c)
    g2 = gate.reshape(batch, FFT_INNER, r * c)
    const = pl.Buffered(1)
    out = pl.pallas_call(
        functools.partial(_fft2_kernel, f1t=f1t, c=c),
        grid=(batch, r // f1t),
        in_specs=[
            pl.BlockSpec((1, f1t, 2 * FFT_INNER, c), lambda b, f: (b, f, 0, 0)),
            pl.BlockSpec((2 * FFT_INNER, 2 * FFT_INNER), lambda b, f: (0, 0), pipeline_mode=const),
            pl.BlockSpec((2 * GROUP_C, GROUP_C), lambda b, f: (0, 0), pipeline_mode=const),
            pl.BlockSpec((1, FFT_INNER, f1t * c), lambda b, f: (b, 0, f)),
        ],
        out_specs=pl.BlockSpec((1, FFT_INNER, f1t * c), lambda b, f: (b, 0, f)),
        out_shape=jax.ShapeDtypeStruct((batch, FFT_INNER, r * c), BF16),
        compiler_params=_cparams(("parallel", "parallel")),
        name="fft_stage2",
    )(bmat, w2, wc, g2)
    return out.reshape(batch * seq, c)


def _sgu_kernel(u_ref, v_ref, gate_ref, lg_ref, lb_ref, ws_ref, bt_ref, o_ref, *, tr):
    v = v_ref[...].astype(F32)
    mu = jnp.mean(v, axis=-1, keepdims=True)
    vc = v - mu
    var = jnp.mean(vc * vc, axis=-1, keepdims=True)
    vn = (vc * lax.rsqrt(var + EPS) * lg_ref[...] + lb_ref[...]).astype(BF16)
    for g in range(SGU_GROUPS):
        cols = slice(g * SGU_DG, (g + 1) * SGU_DG)
        bcol = jnp.broadcast_to(bt_ref[:, g:g + 1], (SGU_CHUNK, SGU_DG))
        for n in range(tr // SGU_CHUNK):
            rows = slice(n * SGU_CHUNK, (n + 1) * SGU_CHUNK)
            sv = jnp.dot(ws_ref[g], vn[rows, cols], preferred_element_type=F32) + bcol
            out = u_ref[rows, cols].astype(F32) * sv * _silu(gate_ref[rows, cols].astype(F32))
            o_ref[rows, cols] = out.astype(BF16)


def _sgu(p, ln_g, ln_b, ws_bf16, b_t, *, tr):
    n = p.shape[0]
    c = W_HALF

    def seg(s):
        return pl.BlockSpec((tr, c), lambda i: (i, s))

    vec = pl.BlockSpec((1, c), lambda i: (0, 0))
    return pl.pallas_call(
        functools.partial(_sgu_kernel, tr=tr),
        grid=(n // tr,),
        in_specs=[
            seg(ODD_U), seg(ODD_V), seg(ODD_D_GATE), vec, vec,
            pl.BlockSpec((SGU_GROUPS, SGU_CHUNK, SGU_CHUNK), lambda i: (0, 0, 0)),
            pl.BlockSpec((SGU_CHUNK, SGU_GROUPS), lambda i: (0, 0)),
        ],
        out_specs=pl.BlockSpec((tr, c), lambda i: (i, 0)),
        out_shape=jax.ShapeDtypeStruct((n, c), BF16),
        compiler_params=_cparams(("parallel",)),
        name="sgu",
    )(p, p, p, ln_g, ln_b, ws_bf16, b_t)


def _trunk(x, p, bias_tiles, *, tq, tk):
    batch, seq, d = x.shape
    x2d = x.reshape(batch * seq, d)
    row = lambda a: a.reshape(1, -1)

    lam_init = 0.8 - 0.6 * math.exp(-0.3 * 0)
    pe = _inproj(x2d, row(p["norm_g"][0]), p["w_in_even"], tm=1024, tn=512)
    mix_a = _conv_module(pe, p["conv_w"], row(p["conv_b"]), row(p["conv_ln_g"]),
                         row(p["conv_ln_b"]), seq=seq, ts=256)
    mix_b = _diff_attention(p["rel_bias"], pe, bias_tiles, row(p["q_norm_g"]), row(p["k_norm_g"]),
                            row(p["lam_q1"]), row(p["lam_k1"]), row(p["lam_q2"]), row(p["lam_k2"]),
                            row(p["subln_g"]), batch=batch, seq=seq, tq=tq, tk=tk,
                            lam_init=lam_init)
    x1 = _outproj(mix_a, mix_b, p["w_out_even"], x2d, tm=512)

    po = _inproj(x1, row(p["norm_g"][1]), p["w_in_odd"], tm=1024, tn=512)
    c_in = po[:, ODD_C_IN * W_HALF:(ODD_C_IN + 1) * W_HALF]
    c_gate = po[:, ODD_C_GATE * W_HALF:(ODD_C_GATE + 1) * W_HALF]
    mix_c = _fnet(c_in, c_gate, batch=batch, seq=seq)
    mix_d = _sgu(po, row(p["sgu_ln_g"]), row(p["sgu_ln_b"]), p["sgu_w"], p["sgu_b"].T, tr=512)
    y = _outproj(mix_c, mix_d, p["w_out_odd"], x1, tm=512)
    return y.reshape(batch, seq, d)


def kernel(x_prompt, x_sample, norm_g, w_in_even, conv_w, conv_b, conv_ln_g, conv_ln_b,
           q_norm_g, k_norm_g, lam_q1, lam_k1, lam_q2, lam_k2, subln_g, rel_bias, w_out_even,
           w_in_odd, sgu_ln_g, sgu_ln_b, sgu_w, sgu_b, w_out_odd):
    p = dict(
        norm_g=norm_g, w_in_even=w_in_even[0].astype(BF16), conv_w=conv_w[0], conv_b=conv_b[0],
        conv_ln_g=conv_ln_g[0], conv_ln_b=conv_ln_b[0], q_norm_g=q_norm_g[0],
        k_norm_g=k_norm_g[0], lam_q1=lam_q1[0], lam_k1=lam_k1[0], lam_q2=lam_q2[0],
        lam_k2=lam_k2[0], subln_g=subln_g[0], w_out_even=w_out_even[0].astype(BF16),
        w_in_odd=w_in_odd[0].astype(BF16), sgu_ln_g=sgu_ln_g[0], sgu_ln_b=sgu_ln_b[0],
        sgu_w=sgu_w[0].astype(BF16), sgu_b=sgu_b[0], w_out_odd=w_out_odd[0].astype(BF16),
        rel_bias=rel_bias)
    tq = tk = 512
    bias_tiles = _bias_tiles(rel_bias, tq=tq, tk=tk)
    y_prompt = _trunk(x_prompt, p, bias_tiles, tq=tq, tk=tk)
    y_sample = _trunk(x_sample, p, bias_tiles, tq=tq, tk=tk)
    return (y_prompt, y_sample)
```

```python
import functools
import math

import numpy as np
import jax
import jax.numpy as jnp
from jax import lax
from jax.experimental import pallas as pl
from jax.experimental.pallas import tpu as pltpu

F32 = jnp.float32
BF16 = jnp.bfloat16

EPS = 1e-6
LOG2E = math.log2(math.e)

D_MODEL = 2048
W_HALF = D_MODEL // 2
DH = 128
N_HEADS = 4
DV = 2 * DH
CONV_W = 31
CONV_PAD = CONV_W // 2
N_BUCKETS = 32
MAX_DIST = 128
FFT_INNER = 128
GROUP_C = 128
SGU_GROUPS = 4
SGU_CHUNK = 128
SGU_DG = W_HALF // SGU_GROUPS

EVEN_A_LO, EVEN_A_HI, EVEN_A_GATE, EVEN_Q, EVEN_K, EVEN_V, EVEN_B_GATE = range(7)
ODD_C_IN, ODD_C_GATE, ODD_U, ODD_V, ODD_D_GATE = range(5)

VMEM_LIMIT_V7X = 56 * 1024 * 1024
HALO = 16
SUBLANES = 8


def _cparams(sem):
    return pltpu.CompilerParams(dimension_semantics=sem, vmem_limit_bytes=VMEM_LIMIT_V7X)


def _sigmoid(x):
    return 0.5 * jnp.tanh(0.5 * x) + 0.5


def _silu(x):
    return x * _sigmoid(x)


def _inproj_kernel(x_ref, g_ref, w_ref, o_ref, h_ref):
    @pl.when(pl.program_id(1) == 0)
    def _():
        x = x_ref[...]
        ms = jnp.mean(x * x, axis=-1, keepdims=True)
        h_ref[...] = (x * lax.rsqrt(ms + EPS) * g_ref[...]).astype(BF16)

    o_ref[...] = jnp.dot(h_ref[...], w_ref[...], preferred_element_type=F32).astype(BF16)


def _inproj(x2d, g, w_bf16, *, tm, tn):
    n, d = x2d.shape
    p = w_bf16.shape[1]
    return pl.pallas_call(
        _inproj_kernel,
        grid=(n // tm, p // tn),
        in_specs=[
            pl.BlockSpec((tm, d), lambda i, j: (i, 0)),
            pl.BlockSpec((1, d), lambda i, j: (0, 0)),
            pl.BlockSpec((d, tn), lambda i, j: (0, j)),
        ],
        out_specs=pl.BlockSpec((tm, tn), lambda i, j: (i, j)),
        out_shape=jax.ShapeDtypeStruct((n, p), BF16),
        scratch_shapes=[pltpu.VMEM((tm, d), BF16)],
        compiler_params=_cparams(("parallel", "arbitrary")),
        name="inproj",
    )(x2d, g, w_bf16)


def _outproj_kernel(ma_ref, mb_ref, wa_ref, wb_ref, x_ref, o_ref):
    acc = jnp.dot(ma_ref[...], wa_ref[...], preferred_element_type=F32)
    acc = acc + jnp.dot(mb_ref[...], wb_ref[...], preferred_element_type=F32)
    o_ref[...] = x_ref[...] + acc


def _outproj(mix_a, mix_b, w_bf16, x2d, *, tm):
    n, d = x2d.shape
    half = mix_a.shape[1]
    const = pl.Buffered(1)
    return pl.pallas_call(
        _outproj_kernel,
        grid=(n // tm,),
        in_specs=[
            pl.BlockSpec((tm, half), lambda i: (i, 0)),
            pl.BlockSpec((tm, half), lambda i: (i, 0)),
            pl.BlockSpec((half, d), lambda i: (0, 0), pipeline_mode=const),
            pl.BlockSpec((half, d), lambda i: (1, 0), pipeline_mode=const),
            pl.BlockSpec((tm, d), lambda i: (i, 0)),
        ],
        out_specs=pl.BlockSpec((tm, d), lambda i: (i, 0)),
        out_shape=jax.ShapeDtypeStruct((n, d), F32),
        compiler_params=_cparams(("parallel",)),
        name="outproj",
    )(mix_a, mix_b, w_bf16, w_bf16, x2d)


def _conv_kernel(lo_ref, hi_ref, plo_ref, phi_ref, nlo_ref, nhi_ref, gate_ref,
                 w_ref, b_ref, lg_ref, lb_ref, o_ref, xs_ref, wb_ref, y_ref,
                 *, ts, tiles_per_seq, rc, nr):
    i = pl.program_id(0)
    first = (i % tiles_per_seq) == 0
    last = (i % tiles_per_seq) == tiles_per_seq - 1
    c = o_ref.shape[1]
    lw = c // 2
    padded = ts + 2 * HALO

    def glu(a, b):
        return a.astype(F32) * _sigmoid(b.astype(F32))

    xs_ref[0, HALO:HALO + ts, :] = glu(lo_ref[...], hi_ref[...])
    xs_ref[0, 0:HALO, :] = jnp.where(first, 0.0, glu(plo_ref[...], phi_ref[...]))
    xs_ref[0, HALO + ts:padded, :] = jnp.where(last, 0.0, glu(nlo_ref[...], nhi_ref[...]))
    xs_ref[0, padded:padded + SUBLANES, :] = jnp.zeros((SUBLANES, c), F32)
    for k in range(CONV_W):
        wb_ref[k] = jnp.broadcast_to(w_ref[k:k + 1, :], (SUBLANES, c))

    def shift_chunk(q, carry):
        row = pl.multiple_of(q * rc, rc)
        for col in range(0, c, lw):
            win = xs_ref[0, pl.ds(row, rc + SUBLANES), col:col + lw]
            for r in range(1, SUBLANES):
                xs_ref[r, pl.ds(row, rc), col:col + lw] = pltpu.roll(
                    win, rc + SUBLANES - r, axis=0)[:rc]
        return carry

    lax.fori_loop(0, padded // rc, shift_chunk, 0)

    base0 = HALO - CONV_PAD
    groups = rc // SUBLANES

    def conv_chunk(q, carry):
        row = pl.multiple_of(q * rc, rc)
        for col in range(0, c, lw):
            acc = [jnp.zeros((SUBLANES, lw), F32) for _ in range(groups)]
            for k in range(CONV_W):
                a, r = divmod(base0 + k, SUBLANES)
                wk = wb_ref[k, :, col:col + lw]
                for g in range(groups):
                    x = xs_ref[r, pl.ds(row + SUBLANES * (a + g), SUBLANES), col:col + lw]
                    acc[g] = acc[g] + x * wk
            bias = b_ref[:, col:col + lw]
            for g in range(groups):
                y_ref[pl.ds(row + SUBLANES * g, SUBLANES), col:col + lw] = acc[g] + bias
        return carry

    lax.fori_loop(0, ts // rc, conv_chunk, 0)

    def chunk(r, carry):
        row = pl.multiple_of(r * nr, nr)
        y = y_ref[pl.ds(row, nr), :]
        mu = jnp.mean(y, axis=-1, keepdims=True)
        yc = y - mu
        var = jnp.mean(yc * yc, axis=-1, keepdims=True)
        z = yc * lax.rsqrt(var + EPS) * lg_ref[...] + lb_ref[...]
        out = _silu(z) * _silu(gate_ref[pl.ds(row, nr), :].astype(F32))
        o_ref[pl.ds(row, nr), :] = out.astype(BF16)
        return carry

    lax.fori_loop(0, ts // nr, chunk, 0, unroll=2)


def _conv_module(p, conv_w, conv_b, ln_g, ln_b, *, seq, ts, rc=32, nr=32):
    n = p.shape[0]
    c = W_HALF
    tps = seq // ts
    hb = ts // HALO
    nhb = n // HALO

    def prev_map(col):
        return lambda i: (jnp.maximum(i * hb - 1, 0), col)

    def next_map(col):
        return lambda i: (jnp.minimum((i + 1) * hb, nhb - 1), col)

    vec = pl.BlockSpec((1, c), lambda i: (0, 0))
    kern = functools.partial(_conv_kernel, ts=ts, tiles_per_seq=tps, rc=rc, nr=nr)
    return pl.pallas_call(
        kern,
        grid=(n // ts,),
        in_specs=[
            pl.BlockSpec((ts, c), lambda i: (i, EVEN_A_LO)),
            pl.BlockSpec((ts, c), lambda i: (i, EVEN_A_HI)),
            pl.BlockSpec((HALO, c), prev_map(EVEN_A_LO)),
            pl.BlockSpec((HALO, c), prev_map(EVEN_A_HI)),
            pl.BlockSpec((HALO, c), next_map(EVEN_A_LO)),
            pl.BlockSpec((HALO, c), next_map(EVEN_A_HI)),
            pl.BlockSpec((ts, c), lambda i: (i, EVEN_A_GATE)),
            pl.BlockSpec((CONV_W, c), lambda i: (0, 0)),
            vec, vec, vec,
        ],
        out_specs=pl.BlockSpec((ts, c), lambda i: (i, 0)),
        out_shape=jax.ShapeDtypeStruct((n, c), BF16),
        scratch_shapes=[pltpu.VMEM((SUBLANES, ts + 2 * HALO + SUBLANES, c), F32),
                        pltpu.VMEM((CONV_W, SUBLANES, c), F32),
                        pltpu.VMEM((ts, c), F32)],
        compiler_params=_cparams(("parallel",)),
        name="conv_module",
    )(p, p, p, p, p, p, p, conv_w, conv_b, ln_g, ln_b)


def _bucket_thresholds():
    nb = N_BUCKETS // 2
    max_exact = nb // 2
    n = np.arange(1, 4 * MAX_DIST, dtype=np.float64)
    large = max_exact + (np.log(n / max_exact) / math.log(MAX_DIST / max_exact)
                         * (nb - max_exact)).astype(np.int64)
    large = np.minimum(large, nb - 1)
    thr = [int(n[np.argmax(large >= b)]) for b in range(max_exact + 1, nb)]
    return max_exact, thr


def _bias_kernel(rb_ref, o_ref, *, tq, tk, lo_diag):
    h = pl.program_id(0)
    d = pl.program_id(1) + lo_diag
    row = lax.broadcasted_iota(jnp.int32, (tq, tk), 0)
    col = lax.broadcasted_iota(jnp.int32, (tq, tk), 1)
    rel = col - row + d * tk
    n = jnp.abs(rel)
    max_exact, thr = _bucket_thresholds()
    bucket = jnp.minimum(n, max_exact)
    for t in thr:
        bucket = bucket + jnp.where(n >= t, 1, 0)
    bucket = bucket + jnp.where(rel > 0, N_BUCKETS // 2, 0)
    val = jnp.zeros((tq, tk), F32)
    for b in range(N_BUCKETS):
        val = jnp.where(bucket == b, rb_ref[b, h], val)
    o_ref[0, 0] = val * LOG2E


EXP2_SAFE_RANGE = 100.0
N_DIAG = 5
FAR = N_DIAG // 2


def _bias_tiles(rel_bias, *, tq, tk):
    assert tq == tk and tk + 1 >= _bucket_thresholds()[1][-1]
    nd = N_DIAG
    kern = functools.partial(_bias_kernel, tq=tq, tk=tk, lo_diag=-FAR)
    return pl.pallas_call(
        kern,
        grid=(N_HEADS, nd),
        in_specs=[pl.BlockSpec(memory_space=pltpu.SMEM)],
        out_specs=pl.BlockSpec((1, 1, tq, tk), lambda h, d: (h, d, 0, 0)),
        out_shape=jax.ShapeDtypeStruct((N_HEADS, nd, tq, tk), F32),
        compiler_params=_cparams(("parallel", "parallel")),
        name="bias_tiles",
    )(rel_bias)


def _sub_head_norm(x, gain, scale):
    parts = []
    for t in range(2):
        blk = x[:, t * DH:(t + 1) * DH]
        ms = jnp.mean(blk * blk, axis=-1, keepdims=True)
        parts.append(blk * lax.rsqrt(ms + EPS) * gain * scale)
    return parts


def _attn_kernel(rb_ref, q_ref, k_ref, v_ref, bias_ref, gate_ref, qg_ref, kg_ref, lq1_ref, lk1_ref,
                 lq2_ref, lk2_ref, sg_ref, o_ref, kx_ref, kmax_ref, qx_ref, acc_ref, ls_ref,
                 *, tq, tk, nkv, lam_init, q_scale, unroll):
    h = pl.program_id(1)
    i = pl.program_id(2)
    contract_last = (((1,), (1,)), ((), ()))
    unit_col = jnp.where(lax.broadcasted_iota(jnp.int32, (tk, DH), 1) == 0, 1.0, 0.0)

    @pl.when(i == 0)
    def _():
        def norm_rows(r, kmax2):
            rows = pl.ds(pl.multiple_of(r * tk, tk), tk)
            ks = _sub_head_norm(k_ref[rows, :].astype(F32), kg_ref[...], 1.0)
            out = []
            for t in range(2):
                kb = ks[t].astype(BF16)
                kx_ref[t, rows, :DH] = kb
                kx_ref[t, rows, DH:] = unit_col.astype(BF16)
                kf = kb.astype(F32)
                n2 = jnp.sum(kf * kf, axis=-1, keepdims=True)
                out.append(jnp.maximum(kmax2[t], jnp.max(n2, axis=0, keepdims=True)))
            return tuple(out)
        zero11 = jnp.zeros((1, 1), F32)
        k2 = lax.fori_loop(0, nkv, norm_rows, (zero11, zero11))
        for t in range(2):
            kmax_ref[t] = jnp.sqrt(k2[t])

    bmax = rb_ref[0, h]
    bmin = rb_ref[0, h]
    for b in range(1, N_BUCKETS):
        bmax = jnp.maximum(bmax, rb_ref[b, h])
        bmin = jnp.minimum(bmin, rb_ref[b, h])
    bmax = bmax * LOG2E
    bmin = bmin * LOG2E

    qn = _sub_head_norm(q_ref[...].astype(F32), qg_ref[...], q_scale)
    spread = jnp.zeros((1, 1), F32)
    for t in range(2):
        qb = qn[t].astype(BF16)
        qf = qb.astype(F32)
        bound = jnp.sqrt(jnp.sum(qf * qf, axis=-1, keepdims=True)) * kmax_ref[t]
        qx_ref[t, :, :DH] = qb
        qx_ref[t, :, DH:] = (unit_col * (-(bound + bmax))).astype(BF16)
        spread = jnp.maximum(spread, jnp.max(bound, axis=0, keepdims=True))
    lowest_exponent = 2.0 * spread[0, 0] + (bmax - bmin)
    acc_ref[...] = jnp.zeros_like(acc_ref)

    def shifted_scores(j, t):
        rows = pl.ds(pl.multiple_of(j * tk, tk), tk)
        bias = bias_ref[0, jnp.clip(j - i, -FAR, FAR) + FAR]
        return lax.dot_general(qx_ref[t], kx_ref[t, rows, :], contract_last,
                               preferred_element_type=F32) + bias

    def bounded():
        ls_ref[...] = jnp.zeros_like(ls_ref)

        def body(jj, carry):
            for u in range(unroll):
                j = unroll * jj + u
                vv = v_ref[pl.ds(pl.multiple_of(j * tk, tk), tk), :]
                for t in range(2):
                    p = jnp.exp2(shifted_scores(j, t))
                    part = p[:, :DH]
                    for c in range(1, tk // DH):
                        part = part + p[:, c * DH:(c + 1) * DH]
                    ls_ref[t] = ls_ref[t] + part
                    acc_ref[t] = acc_ref[t] + jnp.dot(p.astype(BF16), vv,
                                                      preferred_element_type=F32)
            return carry

        lax.fori_loop(0, nkv // unroll, body, 0)
        return tuple(jnp.sum(ls_ref[t], axis=-1, keepdims=True) for t in range(2))

    def running_max():
        def body(j, ml):
            vv = v_ref[pl.ds(pl.multiple_of(j * tk, tk), tk), :]
            out = []
            for t in range(2):
                m, l = ml[2 * t], ml[2 * t + 1]
                s = shifted_scores(j, t)
                mn = jnp.maximum(m, jnp.max(s, axis=-1, keepdims=True))
                p = jnp.exp2(s - mn)
                a = jnp.exp2(m - mn)
                l = a * l + jnp.sum(p, axis=-1, keepdims=True)
                acc_ref[t] = a * acc_ref[t] + jnp.dot(p.astype(BF16), vv,
                                                      preferred_element_type=F32)
                out += [mn, l]
            return tuple(out)

        neg = jnp.full((tq, 1), -jnp.inf, F32)
        zero = jnp.zeros((tq, 1), F32)
        _, l1, _, l2 = lax.fori_loop(0, nkv, body, (neg, zero, neg, zero))
        return l1, l2

    l1, l2 = lax.cond(lowest_exponent <= EXP2_SAFE_RANGE, bounded, running_max)

    lam =(jnp.exp(jnp.sum(lq1_ref[...] * lk1_ref[...], axis=-1, keepdims=True))
           - jnp.exp(jnp.sum(lq2_ref[...] * lk2_ref[...], axis=-1, keepdims=True)) + lam_init)
    o = acc_ref[0] * (1.0 / l1) - lam * (acc_ref[1] * (1.0 / l2))
    ms = jnp.mean(o * o, axis=-1, keepdims=True)
    y = o * lax.rsqrt(ms + EPS) * sg_ref[...] * (1.0 - lam_init)
    o_ref[...] = (y * _silu(gate_ref[...].astype(F32))).astype(BF16)


def _diff_attention(rel_bias, p, bias_tiles, q_gain, k_gain, lq1, lk1, lq2, lk2, subln_g, *, batch,
                    seq, tq, tk, lam_init):
    n = p.shape[0]
    nq = seq // tq
    nd = bias_tiles.shape[1]
    nkv = seq // tk
    unroll = 4
    assert tq == tk and nkv % unroll == 0
    per_head = W_HALF // DV

    def q_rows(seg):
        return pl.BlockSpec((tq, DV), lambda b, h, i: (b * nq + i, seg * per_head + h))

    def kv_rows(seg):
        return pl.BlockSpec((seq, DV), lambda b, h, i: (b, seg * per_head + h))

    vec = pl.BlockSpec((1, DH), lambda b, h, i: (0, 0))
    kern = functools.partial(_attn_kernel, tq=tq, tk=tk, nkv=nkv, lam_init=lam_init,
                             q_scale=LOG2E / math.sqrt(DH), unroll=unroll)
    return pl.pallas_call(
        kern,
        grid=(batch, N_HEADS, nq),
        in_specs=[
            pl.BlockSpec(memory_space=pltpu.SMEM),
            q_rows(EVEN_Q), kv_rows(EVEN_K), kv_rows(EVEN_V),
            pl.BlockSpec((1, nd, tq, tk), lambda b, h, i: (h, 0, 0, 0)),
            q_rows(EVEN_B_GATE),
            vec, vec, vec, vec, vec, vec,
            pl.BlockSpec((1, DV), lambda b, h, i: (0, 0)),
        ],
        out_specs=pl.BlockSpec((tq, DV), lambda b, h, i: (b * nq + i, h)),
        out_shape=jax.ShapeDtypeStruct((n, N_HEADS * DV), BF16),
        scratch_shapes=[
            pltpu.VMEM((2, seq, 2 * DH), BF16),
            pltpu.VMEM((2, 1, 1), F32),
            pltpu.VMEM((2, tq, 2 * DH), BF16),
            pltpu.VMEM((2, tq, DV), F32),
            pltpu.VMEM((2, tq, DH), F32),
        ],
        compiler_params=_cparams(("parallel", "parallel", "arbitrary")),
        name="diff_attention",
    )(rel_bias, p, p, p, bias_tiles, p, q_gain, k_gain, lq1, lk1, lq2, lk2, subln_g)


ROWS_BF16 = 16


def _fnet_constants(seq, ch):
    r = seq // FFT_INNER
    two_pi = 2.0 * np.pi
    ang_r = two_pi * np.outer(np.arange(r), np.arange(r)) / r
    base = np.stack([np.cos(ang_r), -np.sin(ang_r)], axis=1).reshape(2 * r, r)
    w1 = np.kron(base, np.eye(ROWS_BF16))
    ang_tw = two_pi * np.outer(np.arange(FFT_INNER), np.arange(r)) / seq
    tw = np.stack([np.cos(ang_tw), np.sin(ang_tw)])
    tw = tw.reshape(2, FFT_INNER, r // ROWS_BF16, ROWS_BF16).transpose(2, 0, 1, 3)
    ang_i = two_pi * np.outer(np.arange(FFT_INNER), np.arange(FFT_INNER)) / FFT_INNER
    c, s = np.cos(ang_i), np.sin(ang_i)
    w2 = np.block([[c, s], [-s, c]])
    scale = 1.0 / math.sqrt(seq * GROUP_C)
    eye_g = np.eye(ch // GROUP_C)
    bdc = np.kron(eye_g, c * scale)
    bds = np.kron(eye_g, s * scale)
    perm = np.zeros((ROWS_BF16 * ROWS_BF16,) * 2)
    for b in range(ROWS_BF16):
        for f in range(ROWS_BF16):
            perm[b * ROWS_BF16 + f, f * ROWS_BF16 + b] = 1.0
    as_bf16 = lambda a: jnp.asarray(a, dtype=BF16)
    return (as_bf16(w1), jnp.asarray(tw, dtype=F32), as_bf16(w2), as_bf16(bdc), as_bf16(bds),
            as_bf16(perm))


def _fft1_kernel(x_ref, w_ref, o_ref):
    _, r, rows, c = x_ref.shape
    x = x_ref[0].reshape(r * rows, c)
    y = jnp.dot(w_ref[...], x, preferred_element_type=F32).astype(BF16)
    o_ref[0] = y.reshape(r, 2, rows, c)


def _fft2_kernel(a_ref, tw_ref, w2_ref, bdc_ref, bds_ref, perm_ref, gate_ref, o_ref, u_ref, y_ref):
    nf = a_ref.shape[1]
    ch = a_ref.shape[-1]
    for kk in range(nf):
        br = a_ref[0, kk, 0].astype(F32)
        bi = a_ref[0, kk, 1].astype(F32)
        cw = tw_ref[0, 0][:, kk:kk + 1]
        sw = tw_ref[0, 1][:, kk:kk + 1]
        x = jnp.concatenate([br * cw + bi * sw, bi * cw - br * sw], axis=0).astype(BF16)
        u = jnp.dot(w2_ref[...], x, preferred_element_type=F32).astype(BF16)
        u_ref[0, kk * FFT_INNER:(kk + 1) * FFT_INNER, :] = u[:FFT_INNER]
        u_ref[1, kk * FFT_INNER:(kk + 1) * FFT_INNER, :] = u[FFT_INNER:]
    y_ref[...] = (jnp.dot(u_ref[0], bdc_ref[...], preferred_element_type=F32)
                  + jnp.dot(u_ref[1], bds_ref[...], preferred_element_type=F32)).astype(BF16)
    for a in range(FFT_INNER // ROWS_BF16):
        lo = a * ROWS_BF16
        piece = jnp.concatenate(
            [y_ref[k * FFT_INNER + lo:k * FFT_INNER + lo + ROWS_BF16, :] for k in range(nf)], axis=0)
        z = jnp.dot(perm_ref[...], piece, preferred_element_type=F32)
        g = gate_ref[0, lo:lo + ROWS_BF16].reshape(ROWS_BF16 * nf, ch).astype(F32)
        o_ref[0, lo:lo + ROWS_BF16] = (z * _silu(g)).astype(BF16).reshape(ROWS_BF16, nf, ch)


def _fnet(p, *, batch, seq, ch=512):
    c = W_HALF
    r = seq // FFT_INNER
    nf = ROWS_BF16
    assert r % nf == 0 and c % ch == 0
    w1, tw, w2, bdc, bds, perm = _fnet_constants(seq, ch)
    const = pl.Buffered(1)
    stage1 = pl.pallas_call(
        _fft1_kernel,
        grid=(batch, FFT_INNER // nf),
        in_specs=[
            pl.BlockSpec((1, r, nf, c), lambda b, t: (b, 0, t, ODD_C_IN)),
            pl.BlockSpec((2 * r * nf, r * nf), lambda b, t: (0, 0), pipeline_mode=const),
        ],
        out_specs=pl.BlockSpec((1, r, 2, nf, c), lambda b, t: (b, 0, 0, t, 0)),
        out_shape=jax.ShapeDtypeStruct((batch, r, 2, FFT_INNER, c), BF16),
        compiler_params=_cparams(("parallel", "parallel")),
        name="fft_stage1",
    )(p.reshape(batch, r, FFT_INNER, p.shape[1]), w1)
    gate_cols = ODD_C_GATE * (c // ch)
    cmat = lambda shape: pl.BlockSpec(shape, lambda b, f, j: (0,) * len(shape), pipeline_mode=const)
    out = pl.pallas_call(
        _fft2_kernel,
        grid=(batch, r // nf, c // ch),
        in_specs=[
            pl.BlockSpec((1, nf, 2, FFT_INNER, ch), lambda b, f, j: (b, f, 0, 0, j)),
            pl.BlockSpec((1, 2, FFT_INNER, nf), lambda b, f, j: (f, 0, 0, 0)),
            cmat((2 * FFT_INNER, 2 * FFT_INNER)), cmat((ch, ch)), cmat((ch, ch)),
            cmat((nf * nf, nf * nf)),
            pl.BlockSpec((1, FFT_INNER, nf, ch), lambda b, f, j: (b, 0, f, gate_cols + j)),
        ],
        out_specs=pl.BlockSpec((1, FFT_INNER, nf, ch), lambda b, f, j: (b, 0, f, j)),
        out_shape=jax.ShapeDtypeStruct((batch, FFT_INNER, r, c), BF16),
        scratch_shapes=[pltpu.VMEM((2, nf * FFT_INNER, ch), BF16),
                        pltpu.VMEM((nf * FFT_INNER, ch), BF16)],
        compiler_params=_cparams(("parallel", "parallel", "parallel")),
        name="fft_stage2",
    )(stage1, tw, w2, bdc, bds, perm, p.reshape(batch, FFT_INNER, r, p.shape[1]))
    return out.reshape(batch * seq, c)


def _sgu_kernel(u_ref, v_ref, gate_ref, lg_ref, lb_ref, ws_ref, bt_ref, o_ref, *, tr):
    v = v_ref[...].astype(F32)
    mu = jnp.mean(v, axis=-1, keepdims=True)
    vc = v - mu
    var = jnp.mean(vc * vc, axis=-1, keepdims=True)
    vn = (vc * lax.rsqrt(var + EPS) * lg_ref[...] + lb_ref[...]).astype(BF16)
    for g in range(SGU_GROUPS):
        cols = slice(g * SGU_DG, (g + 1) * SGU_DG)
        bcol = jnp.broadcast_to(bt_ref[:, g:g + 1], (SGU_CHUNK, SGU_DG))
        for n in range(tr // SGU_CHUNK):
            rows = slice(n * SGU_CHUNK, (n + 1) * SGU_CHUNK)
            sv = jnp.dot(ws_ref[g], vn[rows, cols], preferred_element_type=F32) + bcol
            out = u_ref[rows, cols].astype(F32) * sv * _silu(gate_ref[rows, cols].astype(F32))
            o_ref[rows, cols] = out.astype(BF16)


def _sgu(p, ln_g, ln_b, ws_bf16, b_t, *, tr):
    n = p.shape[0]
    c = W_HALF

    def seg(s):
        return pl.BlockSpec((tr, c), lambda i: (i, s))

    vec = pl.BlockSpec((1, c), lambda i: (0, 0))
    return pl.pallas_call(
        functools.partial(_sgu_kernel, tr=tr),
        grid=(n // tr,),
        in_specs=[
            seg(ODD_U), seg(ODD_V), seg(ODD_D_GATE), vec, vec,
            pl.BlockSpec((SGU_GROUPS, SGU_CHUNK, SGU_CHUNK), lambda i: (0, 0, 0)),
            pl.BlockSpec((SGU_CHUNK, SGU_GROUPS), lambda i: (0, 0)),
        ],
        out_specs=pl.BlockSpec((tr, c), lambda i: (i, 0)),
        out_shape=jax.ShapeDtypeStruct((n, c), BF16),
        compiler_params=_cparams(("parallel",)),
        name="sgu",
    )(p, p, p, ln_g, ln_b, ws_bf16, b_t)


def _trunk(x, p, bias_tiles, *, tq, tk):
    batch, seq, d = x.shape
    x2d = x.reshape(batch * seq, d)
    row = lambda a: a.reshape(1, -1)

    lam_init = 0.8 - 0.6 * math.exp(-0.3 * 0)
    pe = _inproj(x2d, row(p["norm_g"][0]), p["w_in_even"], tm=1024, tn=1024)
    mix_a = _conv_module(pe, p["conv_w"], row(p["conv_b"]), row(p["conv_ln_g"]),
                         row(p["conv_ln_b"]), seq=seq, ts=512)
    mix_b = _diff_attention(p["rel_bias"], pe, bias_tiles, row(p["q_norm_g"]), row(p["k_norm_g"]),
                            row(p["lam_q1"]), row(p["lam_k1"]), row(p["lam_q2"]), row(p["lam_k2"]),
                            row(p["subln_g"]), batch=batch, seq=seq, tq=tq, tk=tk,
                            lam_init=lam_init)
    x1 = _outproj(mix_a, mix_b, p["w_out_even"], x2d, tm=512)

    po = _inproj(x1, row(p["norm_g"][1]), p["w_in_odd"], tm=1024, tn=1024)
    mix_c = _fnet(po, batch=batch, seq=seq)
    mix_d = _sgu(po, row(p["sgu_ln_g"]), row(p["sgu_ln_b"]), p["sgu_w"], p["sgu_b"].T, tr=512)
    y = _outproj(mix_c, mix_d, p["w_out_odd"], x1, tm=512)
    return y.reshape(batch, seq, d)


def kernel(x_prompt, x_sample, norm_g, w_in_even, conv_w, conv_b, conv_ln_g, conv_ln_b,
           q_norm_g, k_norm_g, lam_q1, lam_k1, lam_q2, lam_k2, subln_g, rel_bias, w_out_even,
           w_in_odd, sgu_ln_g, sgu_ln_b, sgu_w, sgu_b, w_out_odd):
    p = dict(
        norm_g=norm_g, w_in_even=w_in_even[0].astype(BF16), conv_w=conv_w[0], conv_b=conv_b[0],
        conv_ln_g=conv_ln_g[0], conv_ln_b=conv_ln_b[0], q_norm_g=q_norm_g[0],
        k_norm_g=k_norm_g[0], lam_q1=lam_q1[0], lam_k1=lam_k1[0], lam_q2=lam_q2[0],
        lam_k2=lam_k2[0], subln_g=subln_g[0], w_out_even=w_out_even[0].astype(BF16),
        w_in_odd=w_in_odd[0].astype(BF16), sgu_ln_g=sgu_ln_g[0], sgu_ln_b=sgu_ln_b[0],
        sgu_w=sgu_w[0].astype(BF16), sgu_b=sgu_b[0], w_out_odd=w_out_odd[0].astype(BF16),
        rel_bias=rel_bias)
    tq = tk = 512
    bias_tiles = _bias_tiles(rel_bias, tq=tq, tk=tk)
    y_prompt = _trunk(x_prompt, p, bias_tiles, tq=tq, tk=tk)
    y_sample = _trunk(x_sample, p, bias_tiles, tq=tq, tk=tk)
    return (y_prompt, y_sample)
```

```python
import functools
import math

import numpy as np
import jax
import jax.numpy as jnp
from jax import lax
from jax.experimental import pallas as pl
from jax.experimental.pallas import tpu as pltpu

F32 = jnp.float32
BF16 = jnp.bfloat16

EPS = 1e-6
LOG2E = math.log2(math.e)

D_MODEL = 2048
W_HALF = D_MODEL // 2
DH = 128
N_HEADS = 4
DV = 2 * DH
CONV_W = 31
CONV_PAD = CONV_W // 2
N_BUCKETS = 32
MAX_DIST = 128
FFT_INNER = 128
GROUP_C = 128
SGU_GROUPS = 4
SGU_CHUNK = 128
SGU_DG = W_HALF // SGU_GROUPS

EVEN_A_LO, EVEN_A_HI, EVEN_A_GATE, EVEN_Q, EVEN_K, EVEN_V, EVEN_B_GATE = range(7)
ODD_C_IN, ODD_C_GATE, ODD_U, ODD_V, ODD_D_GATE = range(5)

VMEM_LIMIT_V7X = 56 * 1024 * 1024
HALO = 16
SUBLANES = 8


def _cparams(sem):
    return pltpu.CompilerParams(dimension_semantics=sem, vmem_limit_bytes=VMEM_LIMIT_V7X)


def _sigmoid(x):
    return 0.5 * jnp.tanh(0.5 * x) + 0.5


def _silu(x):
    return x * _sigmoid(x)


def _inproj_kernel(x_ref, g_ref, w_ref, o_ref, h_ref):
    @pl.when(pl.program_id(1) == 0)
    def _():
        x = x_ref[...]
        ms = jnp.mean(x * x, axis=-1, keepdims=True)
        h_ref[...] = (x * lax.rsqrt(ms + EPS) * g_ref[...]).astype(BF16)

    o_ref[...] = jnp.dot(h_ref[...], w_ref[...], preferred_element_type=F32).astype(BF16)


def _inproj(x2d, g, w_bf16, *, tm, tn):
    n, d = x2d.shape
    p = w_bf16.shape[1]
    return pl.pallas_call(
        _inproj_kernel,
        grid=(n // tm, p // tn),
        in_specs=[
            pl.BlockSpec((tm, d), lambda i, j: (i, 0)),
            pl.BlockSpec((1, d), lambda i, j: (0, 0)),
            pl.BlockSpec((d, tn), lambda i, j: (0, j)),
        ],
        out_specs=pl.BlockSpec((tm, tn), lambda i, j: (i, j)),
        out_shape=jax.ShapeDtypeStruct((n, p), BF16),
        scratch_shapes=[pltpu.VMEM((tm, d), BF16)],
        compiler_params=_cparams(("parallel", "arbitrary")),
        name="inproj",
    )(x2d, g, w_bf16)


def _outproj_kernel(ma_ref, mb_ref, wa_ref, wb_ref, x_ref, o_ref):
    acc = jnp.dot(ma_ref[...], wa_ref[...], preferred_element_type=F32)
    acc = acc + jnp.dot(mb_ref[...], wb_ref[...], preferred_element_type=F32)
    o_ref[...] = x_ref[...] + acc


def _outproj(mix_a, mix_b, w_bf16, x2d, *, tm):
    n, d = x2d.shape
    half = mix_a.shape[1]
    const = pl.Buffered(1)
    return pl.pallas_call(
        _outproj_kernel,
        grid=(n // tm,),
        in_specs=[
            pl.BlockSpec((tm, half), lambda i: (i, 0)),
            pl.BlockSpec((tm, half), lambda i: (i, 0)),
            pl.BlockSpec((half, d), lambda i: (0, 0), pipeline_mode=const),
            pl.BlockSpec((half, d), lambda i: (1, 0), pipeline_mode=const),
            pl.BlockSpec((tm, d), lambda i: (i, 0)),
        ],
        out_specs=pl.BlockSpec((tm, d), lambda i: (i, 0)),
        out_shape=jax.ShapeDtypeStruct((n, d), F32),
        compiler_params=_cparams(("parallel",)),
        name="outproj",
    )(mix_a, mix_b, w_bf16, w_bf16, x2d)


def _conv_kernel(lo_ref, hi_ref, plo_ref, phi_ref, nlo_ref, nhi_ref, gate_ref,
                 w_ref, b_ref, lg_ref, lb_ref, o_ref, xs_ref, wb_ref, y_ref,
                 *, ts, tiles_per_seq, rc, nr):
    i = pl.program_id(0)
    first = (i % tiles_per_seq) == 0
    last = (i % tiles_per_seq) == tiles_per_seq - 1
    c = o_ref.shape[1]
    lw = c // 2
    padded = ts + 2 * HALO

    def glu(a, b):
        return a.astype(F32) * _sigmoid(b.astype(F32))

    xs_ref[0, HALO:HALO + ts, :] = glu(lo_ref[...], hi_ref[...])
    xs_ref[0, 0:HALO, :] = jnp.where(first, 0.0, glu(plo_ref[...], phi_ref[...]))
    xs_ref[0, HALO + ts:padded, :] = jnp.where(last, 0.0, glu(nlo_ref[...], nhi_ref[...]))
    xs_ref[0, padded:padded + SUBLANES, :] = jnp.zeros((SUBLANES, c), F32)
    for k in range(CONV_W):
        wb_ref[k] = jnp.broadcast_to(w_ref[k:k + 1, :], (SUBLANES, c))

    def shift_chunk(q, carry):
        row = pl.multiple_of(q * rc, rc)
        for col in range(0, c, lw):
            win = xs_ref[0, pl.ds(row, rc + SUBLANES), col:col + lw]
            for r in range(1, SUBLANES):
                xs_ref[r, pl.ds(row, rc), col:col + lw] = pltpu.roll(
                    win, rc + SUBLANES - r, axis=0)[:rc]
        return carry

    lax.fori_loop(0, padded // rc, shift_chunk, 0)

    base0 = HALO - CONV_PAD
    groups = rc // SUBLANES

    def conv_chunk(q, carry):
        row = pl.multiple_of(q * rc, rc)
        for col in range(0, c, lw):
            acc = [jnp.zeros((SUBLANES, lw), F32) for _ in range(groups)]
            for k in range(CONV_W):
                a, r = divmod(base0 + k, SUBLANES)
                wk = wb_ref[k, :, col:col + lw]
                for g in range(groups):
                    x = xs_ref[r, pl.ds(row + SUBLANES * (a + g), SUBLANES), col:col + lw]
                    acc[g] = acc[g] + x * wk
            bias = b_ref[:, col:col + lw]
            for g in range(groups):
                y_ref[pl.ds(row + SUBLANES * g, SUBLANES), col:col + lw] = acc[g] + bias
        return carry

    lax.fori_loop(0, ts // rc, conv_chunk, 0)

    def chunk(r, carry):
        row = pl.multiple_of(r * nr, nr)
        y = y_ref[pl.ds(row, nr), :]
        mu = jnp.mean(y, axis=-1, keepdims=True)
        yc = y - mu
        var = jnp.mean(yc * yc, axis=-1, keepdims=True)
        z = yc * lax.rsqrt(var + EPS) * lg_ref[...] + lb_ref[...]
        out = _silu(z) * _silu(gate_ref[pl.ds(row, nr), :].astype(F32))
        o_ref[pl.ds(row, nr), :] = out.astype(BF16)
        return carry

    lax.fori_loop(0, ts // nr, chunk, 0, unroll=2)


def _conv_module(p, conv_w, conv_b, ln_g, ln_b, *, seq, ts, rc=32, nr=32):
    n = p.shape[0]
    c = W_HALF
    tps = seq // ts
    hb = ts // HALO
    nhb = n // HALO

    def prev_map(col):
        return lambda i: (jnp.maximum(i * hb - 1, 0), col)

    def next_map(col):
        return lambda i: (jnp.minimum((i + 1) * hb, nhb - 1), col)

    vec = pl.BlockSpec((1, c), lambda i: (0, 0))
    kern = functools.partial(_conv_kernel, ts=ts, tiles_per_seq=tps, rc=rc, nr=nr)
    return pl.pallas_call(
        kern,
        grid=(n // ts,),
        in_specs=[
            pl.BlockSpec((ts, c), lambda i: (i, EVEN_A_LO)),
            pl.BlockSpec((ts, c), lambda i: (i, EVEN_A_HI)),
            pl.BlockSpec((HALO, c), prev_map(EVEN_A_LO)),
            pl.BlockSpec((HALO, c), prev_map(EVEN_A_HI)),
            pl.BlockSpec((HALO, c), next_map(EVEN_A_LO)),
            pl.BlockSpec((HALO, c), next_map(EVEN_A_HI)),
            pl.BlockSpec((ts, c), lambda i: (i, EVEN_A_GATE)),
            pl.BlockSpec((CONV_W, c), lambda i: (0, 0)),
            vec, vec, vec,
        ],
        out_specs=pl.BlockSpec((ts, c), lambda i: (i, 0)),
        out_shape=jax.ShapeDtypeStruct((n, c), BF16),
        scratch_shapes=[pltpu.VMEM((SUBLANES, ts + 2 * HALO + SUBLANES, c), F32),
                        pltpu.VMEM((CONV_W, SUBLANES, c), F32),
                        pltpu.VMEM((ts, c), F32)],
        compiler_params=_cparams(("parallel",)),
        name="conv_module",
    )(p, p, p, p, p, p, p, conv_w, conv_b, ln_g, ln_b)


def _bucket_thresholds():
    nb = N_BUCKETS // 2
    max_exact = nb // 2
    n = np.arange(1, 4 * MAX_DIST, dtype=np.float64)
    large = max_exact + (np.log(n / max_exact) / math.log(MAX_DIST / max_exact)
                         * (nb - max_exact)).astype(np.int64)
    large = np.minimum(large, nb - 1)
    thr = [int(n[np.argmax(large >= b)]) for b in range(max_exact + 1, nb)]
    return max_exact, thr


def _bias_kernel(rb_ref, qg_ref, kg_ref, o_ref, range_ref, *, tk, lo_diag, q_scale):
    h = pl.program_id(0)
    d = pl.program_id(1) + lo_diag
    gq = jnp.abs(qg_ref[0])
    gk = jnp.abs(kg_ref[0])
    for c in range(1, DH):
        gq = jnp.maximum(gq, jnp.abs(qg_ref[c]))
        gk = jnp.maximum(gk, jnp.abs(kg_ref[c]))
    qk_bound = gq * gk * (DH * q_scale * BF16_SLACK)
    bmax = rb_ref[0, h]
    bmin = rb_ref[0, h]
    for b in range(1, N_BUCKETS):
        bmax = jnp.maximum(bmax, rb_ref[b, h])
        bmin = jnp.minimum(bmin, rb_ref[b, h])
    shift = qk_bound + bmax * LOG2E
    range_ref[h] = 2.0 * qk_bound + (bmax - bmin) * LOG2E

    @pl.when(jnp.abs(d) < FAR)
    def _():
        row = lax.broadcasted_iota(jnp.int32, (tk, tk), 0)
        col = lax.broadcasted_iota(jnp.int32, (tk, tk), 1)
        rel = col - row + d * tk
        n = jnp.abs(rel)
        max_exact, thr = _bucket_thresholds()
        bucket = jnp.minimum(n, max_exact)
        for t in thr:
            bucket = bucket + jnp.where(n >= t, 1, 0)
        bucket = bucket + jnp.where(rel > 0, N_BUCKETS // 2, 0)
        val = jnp.zeros((tk, tk), F32)
        for b in range(N_BUCKETS):
            val = jnp.where(bucket == b, rb_ref[b, h], val)
        o_ref[0, 0] = val * LOG2E - shift

    @pl.when(d <= -FAR)
    def _():
        o_ref[0, 0] = jnp.full((tk, tk), rb_ref[N_BUCKETS // 2 - 1, h] * LOG2E - shift, F32)

    @pl.when(d >= FAR)
    def _():
        o_ref[0, 0] = jnp.full((tk, tk), rb_ref[N_BUCKETS - 1, h] * LOG2E - shift, F32)


EXP2_SAFE_RANGE = 100.0
BF16_SLACK = 1.01
N_DIAG = 5
FAR = N_DIAG // 2


def _bias_tiles(rel_bias, q_gain, k_gain, *, tk, q_scale):
    assert tk + 1 >= _bucket_thresholds()[1][-1]
    nd = N_DIAG
    smem = pl.BlockSpec(memory_space=pltpu.SMEM)
    kern = functools.partial(_bias_kernel, tk=tk, lo_diag=-FAR, q_scale=q_scale)
    return pl.pallas_call(
        kern,
        grid=(N_HEADS, nd),
        in_specs=[smem, smem, smem],
        out_specs=[pl.BlockSpec((1, 1, tk, tk), lambda h, d: (h, d, 0, 0)), smem],
        out_shape=[jax.ShapeDtypeStruct((N_HEADS, nd, tk, tk), F32),
                   jax.ShapeDtypeStruct((N_HEADS,), F32)],
        compiler_params=_cparams(("arbitrary", "arbitrary")),
        name="bias_tiles",
    )(rel_bias, q_gain, k_gain)


def _sub_head_norm(x, gain, scale):
    parts = []
    for t in range(2):
        blk = x[:, t * DH:(t + 1) * DH]
        ms = jnp.mean(blk * blk, axis=-1, keepdims=True)
        parts.append(blk * lax.rsqrt(ms + EPS) * gain * scale)
    return parts


def _attn_kernel(range_ref, q_ref, k_ref, v_ref, bias_ref, gate_ref, qg_ref, kg_ref, lq1_ref, lk1_ref,
                 lq2_ref, lk2_ref, sg_ref, o_ref, kx_ref, qx_ref, acc_ref, ls_ref, l_ref,
                 *, tq, tk, nkv, lam_init, q_scale, unroll):
    h = pl.program_id(1)
    i = pl.program_id(2)
    contract_last = (((1,), (1,)), ((), ()))

    @pl.when(i == 0)
    def _():
        def norm_rows(r, carry):
            rows = pl.ds(pl.multiple_of(r * tk, tk), tk)
            ks = _sub_head_norm(k_ref[rows, :].astype(F32), kg_ref[...], 1.0)
            kx_ref[rows, :] = jnp.concatenate(ks, axis=-1).astype(BF16)
            return carry
        lax.fori_loop(0, nkv, norm_rows, 0)

    qn = _sub_head_norm(q_ref[...].astype(F32), qg_ref[...], q_scale)
    qx_ref[...] = jnp.concatenate(qn, axis=-1).astype(BF16)
    acc_ref[...] = jnp.zeros_like(acc_ref)

    nsub = tq // tk

    def shifted_scores(j, t):
        rows = pl.ds(pl.multiple_of(j * tk, tk), tk)
        s = lax.dot_general(qx_ref[:, t * DH:(t + 1) * DH], kx_ref[rows, t * DH:(t + 1) * DH],
                            contract_last, preferred_element_type=F32)
        parts = []
        for a in range(nsub):
            diag = jnp.clip(j - (i * nsub + a), -FAR, FAR) + FAR
            parts.append(s[a * tk:(a + 1) * tk] + bias_ref[0, diag])
        return jnp.concatenate(parts, axis=0)

    def bounded():
        ls_ref[...] = jnp.zeros_like(ls_ref)

        def body(jj, carry):
            for u in range(unroll):
                j = unroll * jj + u
                vv = v_ref[pl.ds(pl.multiple_of(j * tk, tk), tk), :]
                for t in range(2):
                    p = jnp.exp2(shifted_scores(j, t))
                    part = p[:, :DH]
                    for c in range(1, tk // DH):
                        part = part + p[:, c * DH:(c + 1) * DH]
                    ls_ref[t] = ls_ref[t] + part
                    acc_ref[t] = acc_ref[t] + jnp.dot(p.astype(BF16), vv,
                                                      preferred_element_type=F32)
            return carry

        lax.fori_loop(0, nkv // unroll, body, 0)
        for t in range(2):
            l_ref[t] = jnp.sum(ls_ref[t], axis=-1, keepdims=True)

    def running_max():
        def body(j, ml):
            vv = v_ref[pl.ds(pl.multiple_of(j * tk, tk), tk), :]
            out = []
            for t in range(2):
                m, l = ml[2 * t], ml[2 * t + 1]
                s = shifted_scores(j, t)
                mn = jnp.maximum(m, jnp.max(s, axis=-1, keepdims=True))
                p = jnp.exp2(s - mn)
                a = jnp.exp2(m - mn)
                l = a * l + jnp.sum(p, axis=-1, keepdims=True)
                acc_ref[t] = a * acc_ref[t] + jnp.dot(p.astype(BF16), vv,
                                                      preferred_element_type=F32)
                out += [mn, l]
            return tuple(out)

        neg = jnp.full((tq, 1), -jnp.inf, F32)
        zero = jnp.zeros((tq, 1), F32)
        _, l1, _, l2 = lax.fori_loop(0, nkv, body, (neg, zero, neg, zero))
        l_ref[0] = l1
        l_ref[1] = l2

    lax.cond(range_ref[h] <= EXP2_SAFE_RANGE, bounded, running_max)

    lam =(jnp.exp(jnp.sum(lq1_ref[...] * lk1_ref[...], axis=-1, keepdims=True))
           - jnp.exp(jnp.sum(lq2_ref[...] * lk2_ref[...], axis=-1, keepdims=True)) + lam_init)
    o = acc_ref[0] * (1.0 / l_ref[0]) - lam * (acc_ref[1] * (1.0 / l_ref[1]))
    ms = jnp.mean(o * o, axis=-1, keepdims=True)
    y = o * lax.rsqrt(ms + EPS) * sg_ref[...] * (1.0 - lam_init)
    o_ref[...] = (y * _silu(gate_ref[...].astype(F32))).astype(BF16)


def _diff_attention(exp_range, p, bias_tiles, q_gain, k_gain, lq1, lk1, lq2, lk2, subln_g, *, batch,
                    seq, tq, tk, lam_init, q_scale):
    n = p.shape[0]
    nq = seq // tq
    nd = bias_tiles.shape[1]
    nkv = seq // tk
    unroll = 2
    assert tq % tk == 0 and bias_tiles.shape[2:] == (tk, tk) and nkv % unroll == 0
    per_head = W_HALF // DV

    def q_rows(seg):
        return pl.BlockSpec((tq, DV), lambda b, h, i: (b * nq + i, seg * per_head + h))

    def kv_rows(seg):
        return pl.BlockSpec((seq, DV), lambda b, h, i: (b, seg * per_head + h))

    vec = pl.BlockSpec((1, DH), lambda b, h, i: (0, 0))
    kern = functools.partial(_attn_kernel, tq=tq, tk=tk, nkv=nkv, lam_init=lam_init,
                             q_scale=q_scale, unroll=unroll)
    return pl.pallas_call(
        kern,
        grid=(batch, N_HEADS, nq),
        in_specs=[
            pl.BlockSpec(memory_space=pltpu.SMEM),
            q_rows(EVEN_Q), kv_rows(EVEN_K), kv_rows(EVEN_V),
            pl.BlockSpec((1, nd, tk, tk), lambda b, h, i: (h, 0, 0, 0)),
            q_rows(EVEN_B_GATE),
            vec, vec, vec, vec, vec, vec,
            pl.BlockSpec((1, DV), lambda b, h, i: (0, 0)),
        ],
        out_specs=pl.BlockSpec((tq, DV), lambda b, h, i: (b * nq + i, h)),
        out_shape=jax.ShapeDtypeStruct((n, N_HEADS * DV), BF16),
        scratch_shapes=[
            pltpu.VMEM((seq, DV), BF16),
            pltpu.VMEM((tq, DV), BF16),
            pltpu.VMEM((2, tq, DV), F32),
            pltpu.VMEM((2, tq, DH), F32),
            pltpu.VMEM((2, tq, 1), F32),
        ],
        compiler_params=_cparams(("parallel", "parallel", "arbitrary")),
        name="diff_attention",
    )(exp_range, p, p, p, bias_tiles, p, q_gain, k_gain, lq1, lk1, lq2, lk2, subln_g)


ROWS_BF16 = 16


def _fnet_constants(seq, ch):
    r = seq // FFT_INNER
    two_pi = 2.0 * np.pi
    ang_r = two_pi * np.outer(np.arange(r), np.arange(r)) / r
    base = np.stack([np.cos(ang_r), -np.sin(ang_r)], axis=1).reshape(2 * r, r)
    w1 = np.kron(base, np.eye(ROWS_BF16))
    ang_tw = two_pi * np.outer(np.arange(FFT_INNER), np.arange(r)) / seq
    tw = np.stack([np.cos(ang_tw), np.sin(ang_tw)])
    tw = tw.reshape(2, FFT_INNER, r // ROWS_BF16, ROWS_BF16).transpose(2, 0, 1, 3)
    ang_i = two_pi * np.outer(np.arange(FFT_INNER), np.arange(FFT_INNER)) / FFT_INNER
    c, s = np.cos(ang_i), np.sin(ang_i)
    w2 = np.block([[c, s], [-s, c]])
    scale = 1.0 / math.sqrt(seq * GROUP_C)
    eye_g = np.eye(ch // GROUP_C)
    bdc = np.kron(eye_g, c * scale)
    bds = np.kron(eye_g, s * scale)
    perm = np.zeros((ROWS_BF16 * ROWS_BF16,) * 2)
    for b in range(ROWS_BF16):
        for f in range(ROWS_BF16):
            perm[b * ROWS_BF16 + f, f * ROWS_BF16 + b] = 1.0
    as_bf16 = lambda a: jnp.asarray(a, dtype=BF16)
    return (as_bf16(w1), jnp.asarray(tw, dtype=F32), as_bf16(w2), as_bf16(bdc), as_bf16(bds),
            as_bf16(perm))


def _fft1_kernel(x_ref, w_ref, o_ref):
    _, r, rows, c = x_ref.shape
    x = x_ref[0].reshape(r * rows, c)
    y = jnp.dot(w_ref[...], x, preferred_element_type=F32).astype(BF16)
    o_ref[0] = y.reshape(r, 2, rows, c)


def _fft2_kernel(a_ref, tw_ref, w2_ref, bdc_ref, bds_ref, perm_ref, gate_ref, o_ref, u_ref, y_ref):
    nf = a_ref.shape[1]
    ch = a_ref.shape[-1]
    for kk in range(nf):
        br = a_ref[0, kk, 0].astype(F32)
        bi = a_ref[0, kk, 1].astype(F32)
        cw = tw_ref[0, 0][:, kk:kk + 1]
        sw = tw_ref[0, 1][:, kk:kk + 1]
        x = jnp.concatenate([br * cw + bi * sw, bi * cw - br * sw], axis=0).astype(BF16)
        u = jnp.dot(w2_ref[...], x, preferred_element_type=F32).astype(BF16)
        u_ref[0, kk * FFT_INNER:(kk + 1) * FFT_INNER, :] = u[:FFT_INNER]
        u_ref[1, kk * FFT_INNER:(kk + 1) * FFT_INNER, :] = u[FFT_INNER:]
    y_ref[...] = (jnp.dot(u_ref[0], bdc_ref[...], preferred_element_type=F32)
                  + jnp.dot(u_ref[1], bds_ref[...], preferred_element_type=F32)).astype(BF16)
    for a in range(FFT_INNER // ROWS_BF16):
        lo = a * ROWS_BF16
        piece = jnp.concatenate(
            [y_ref[k * FFT_INNER + lo:k * FFT_INNER + lo + ROWS_BF16, :] for k in range(nf)], axis=0)
        z = jnp.dot(perm_ref[...], piece, preferred_element_type=F32)
        g = gate_ref[0, lo:lo + ROWS_BF16].reshape(ROWS_BF16 * nf, ch).astype(F32)
        o_ref[0, lo:lo + ROWS_BF16] = (z * _silu(g)).astype(BF16).reshape(ROWS_BF16, nf, ch)


def _fnet(p, *, batch, seq, ch=512):
    c = W_HALF
    r = seq // FFT_INNER
    nf = ROWS_BF16
    assert r % nf == 0 and c % ch == 0
    w1, tw, w2, bdc, bds, perm = _fnet_constants(seq, ch)
    const = pl.Buffered(1)
    stage1 = pl.pallas_call(
        _fft1_kernel,
        grid=(batch, FFT_INNER // nf),
        in_specs=[
            pl.BlockSpec((1, r, nf, c), lambda b, t: (b, 0, t, ODD_C_IN)),
            pl.BlockSpec((2 * r * nf, r * nf), lambda b, t: (0, 0), pipeline_mode=const),
        ],
        out_specs=pl.BlockSpec((1, r, 2, nf, c), lambda b, t: (b, 0, 0, t, 0)),
        out_shape=jax.ShapeDtypeStruct((batch, r, 2, FFT_INNER, c), BF16),
        compiler_params=_cparams(("parallel", "parallel")),
        name="fft_stage1",
    )(p.reshape(batch, r, FFT_INNER, p.shape[1]), w1)
    gate_cols = ODD_C_GATE * (c // ch)
    cmat = lambda shape: pl.BlockSpec(shape, lambda b, f, j: (0,) * len(shape), pipeline_mode=const)
    out = pl.pallas_call(
        _fft2_kernel,
        grid=(batch, r // nf, c // ch),
        in_specs=[
            pl.BlockSpec((1, nf, 2, FFT_INNER, ch), lambda b, f, j: (b, f, 0, 0, j)),
            pl.BlockSpec((1, 2, FFT_INNER, nf), lambda b, f, j: (f, 0, 0, 0)),
            cmat((2 * FFT_INNER, 2 * FFT_INNER)), cmat((ch, ch)), cmat((ch, ch)),
            cmat((nf * nf, nf * nf)),
            pl.BlockSpec((1, FFT_INNER, nf, ch), lambda b, f, j: (b, 0, f, gate_cols + j)),
        ],
        out_specs=pl.BlockSpec((1, FFT_INNER, nf, ch), lambda b, f, j: (b, 0, f, j)),
        out_shape=jax.ShapeDtypeStruct((batch, FFT_INNER, r, c), BF16),
        scratch_shapes=[pltpu.VMEM((2, nf * FFT_INNER, ch), BF16),
                        pltpu.VMEM((nf * FFT_INNER, ch), BF16)],
        compiler_params=_cparams(("parallel", "parallel", "parallel")),
        name="fft_stage2",
    )(stage1, tw, w2, bdc, bds, perm, p.reshape(batch, FFT_INNER, r, p.shape[1]))
    return out.reshape(batch * seq, c)


def _sgu_kernel(u_ref, v_ref, gate_ref, lg_ref, lb_ref, ws_ref, bt_ref, o_ref, *, tr):
    v = v_ref[...].astype(F32)
    mu = jnp.mean(v, axis=-1, keepdims=True)
    vc = v - mu
    var = jnp.mean(vc * vc, axis=-1, keepdims=True)
    vn = (vc * lax.rsqrt(var + EPS) * lg_ref[...] + lb_ref[...]).astype(BF16)
    for g in range(SGU_GROUPS):
        cols = slice(g * SGU_DG, (g + 1) * SGU_DG)
        bcol = jnp.broadcast_to(bt_ref[:, g:g + 1], (SGU_CHUNK, SGU_DG))
        for n in range(tr // SGU_CHUNK):
            rows = slice(n * SGU_CHUNK, (n + 1) * SGU_CHUNK)
            sv = jnp.dot(ws_ref[g], vn[rows, cols], preferred_element_type=F32) + bcol
            out = u_ref[rows, cols].astype(F32) * sv * _silu(gate_ref[rows, cols].astype(F32))
            o_ref[rows, cols] = out.astype(BF16)


def _sgu(p, ln_g, ln_b, ws_bf16, b_t, *, tr):
    n = p.shape[0]
    c = W_HALF

    def seg(s):
        return pl.BlockSpec((tr, c), lambda i: (i, s))

    vec = pl.BlockSpec((1, c), lambda i: (0, 0))
    return pl.pallas_call(
        functools.partial(_sgu_kernel, tr=tr),
        grid=(n // tr,),
        in_specs=[
            seg(ODD_U), seg(ODD_V), seg(ODD_D_GATE), vec, vec,
            pl.BlockSpec((SGU_GROUPS, SGU_CHUNK, SGU_CHUNK), lambda i: (0, 0, 0)),
            pl.BlockSpec((SGU_CHUNK, SGU_GROUPS), lambda i: (0, 0)),
        ],
        out_specs=pl.BlockSpec((tr, c), lambda i: (i, 0)),
        out_shape=jax.ShapeDtypeStruct((n, c), BF16),
        compiler_params=_cparams(("parallel",)),
        name="sgu",
    )(p, p, p, ln_g, ln_b, ws_bf16, b_t)


QK_SCALE_LOG2 = LOG2E / math.sqrt(DH)


def _attention_bias(p, *, tk):
    return _bias_tiles(p["rel_bias"], p["q_norm_g"], p["k_norm_g"], tk=tk, q_scale=QK_SCALE_LOG2)


def _trunk(x, p, bias_tiles, exp_range, *, tq, tk):
    batch, seq, d = x.shape
    x2d = x.reshape(batch * seq, d)
    row = lambda a: a.reshape(1, -1)

    lam_init = 0.8 - 0.6 * math.exp(-0.3 * 0)
    pe = _inproj(x2d, row(p["norm_g"][0]), p["w_in_even"], tm=1024, tn=1024)
    mix_a = _conv_module(pe, p["conv_w"], row(p["conv_b"]), row(p["conv_ln_g"]),
                         row(p["conv_ln_b"]), seq=seq, ts=512)
    mix_b = _diff_attention(exp_range, pe, bias_tiles, row(p["q_norm_g"]), row(p["k_norm_g"]),
                            row(p["lam_q1"]), row(p["lam_k1"]), row(p["lam_q2"]), row(p["lam_k2"]),
                            row(p["subln_g"]), batch=batch, seq=seq, tq=tq, tk=tk,
                            lam_init=lam_init, q_scale=QK_SCALE_LOG2)
    x1 = _outproj(mix_a, mix_b, p["w_out_even"], x2d, tm=512)

    po = _inproj(x1, row(p["norm_g"][1]), p["w_in_odd"], tm=1024, tn=1024)
    mix_c = _fnet(po, batch=batch, seq=seq)
    mix_d = _sgu(po, row(p["sgu_ln_g"]), row(p["sgu_ln_b"]), p["sgu_w"], p["sgu_b"].T, tr=512)
    y = _outproj(mix_c, mix_d, p["w_out_odd"], x1, tm=512)
    return y.reshape(batch, seq, d)


def kernel(x_prompt, x_sample, norm_g, w_in_even, conv_w, conv_b, conv_ln_g, conv_ln_b,
           q_norm_g, k_norm_g, lam_q1, lam_k1, lam_q2, lam_k2, subln_g, rel_bias, w_out_even,
           w_in_odd, sgu_ln_g, sgu_ln_b, sgu_w, sgu_b, w_out_odd):
    p = dict(
        norm_g=norm_g, w_in_even=w_in_even[0].astype(BF16), conv_w=conv_w[0], conv_b=conv_b[0],
        conv_ln_g=conv_ln_g[0], conv_ln_b=conv_ln_b[0], q_norm_g=q_norm_g[0],
        k_norm_g=k_norm_g[0], lam_q1=lam_q1[0], lam_k1=lam_k1[0], lam_q2=lam_q2[0],
        lam_k2=lam_k2[0], subln_g=subln_g[0], w_out_even=w_out_even[0].astype(BF16),
        w_in_odd=w_in_odd[0].astype(BF16), sgu_ln_g=sgu_ln_g[0], sgu_ln_b=sgu_ln_b[0],
        sgu_w=sgu_w[0].astype(BF16), sgu_b=sgu_b[0], w_out_odd=w_out_odd[0].astype(BF16),
        rel_bias=rel_bias)
    tq, tk = 1024, 512
    bias_tiles, exp_range = _attention_bias(p, tk=tk)
    y_prompt = _trunk(x_prompt, p, bias_tiles, exp_range, tq=tq, tk=tk)
    y_sample = _trunk(x_sample, p, bias_tiles, exp_range, tq=tq, tk=tk)
    return (y_prompt, y_sample)
```

```python
import functools
import math

import numpy as np
import jax
import jax.numpy as jnp
from jax import lax
from jax.experimental import pallas as pl
from jax.experimental.pallas import tpu as pltpu

F32 = jnp.float32
BF16 = jnp.bfloat16

EPS = 1e-6
LOG2E = math.log2(math.e)

D_MODEL = 2048
W_HALF = D_MODEL // 2
DH = 128
N_HEADS = 4
DV = 2 * DH
CONV_W = 31
CONV_PAD = CONV_W // 2
N_BUCKETS = 32
MAX_DIST = 128
FFT_INNER = 128
GROUP_C = 128
SGU_GROUPS = 4
SGU_CHUNK = 128
SGU_DG = W_HALF // SGU_GROUPS

EVEN_A_LO, EVEN_A_HI, EVEN_A_GATE, EVEN_Q, EVEN_K, EVEN_V, EVEN_B_GATE = range(7)
ODD_C_IN, ODD_C_GATE, ODD_U, ODD_V, ODD_D_GATE = range(5)

VMEM_LIMIT_V7X = 56 * 1024 * 1024
HALO = 16
SUBLANES = 8


def _cparams(sem):
    return pltpu.CompilerParams(dimension_semantics=sem, vmem_limit_bytes=VMEM_LIMIT_V7X)


def _sigmoid(x):
    return 0.5 * jnp.tanh(0.5 * x) + 0.5


def _silu(x):
    return x * _sigmoid(x)


def _inproj_kernel(x_ref, g_ref, w_ref, o_ref, h_ref):
    @pl.when(pl.program_id(1) == 0)
    def _():
        x = x_ref[...]
        ms = jnp.mean(x * x, axis=-1, keepdims=True)
        h_ref[...] = (x * lax.rsqrt(ms + EPS) * g_ref[...]).astype(BF16)

    o_ref[...] = jnp.dot(h_ref[...], w_ref[...], preferred_element_type=F32).astype(BF16)


def _inproj(x2d, g, w_bf16, *, tm, tn):
    n, d = x2d.shape
    p = w_bf16.shape[1]
    return pl.pallas_call(
        _inproj_kernel,
        grid=(n // tm, p // tn),
        in_specs=[
            pl.BlockSpec((tm, d), lambda i, j: (i, 0)),
            pl.BlockSpec((1, d), lambda i, j: (0, 0)),
            pl.BlockSpec((d, tn), lambda i, j: (0, j)),
        ],
        out_specs=pl.BlockSpec((tm, tn), lambda i, j: (i, j)),
        out_shape=jax.ShapeDtypeStruct((n, p), BF16),
        scratch_shapes=[pltpu.VMEM((tm, d), BF16)],
        compiler_params=_cparams(("parallel", "arbitrary")),
        name="inproj",
    )(x2d, g, w_bf16)


def _outproj_kernel(ma_ref, mb_ref, wa_ref, wb_ref, x_ref, o_ref):
    acc = jnp.dot(ma_ref[...], wa_ref[...], preferred_element_type=F32)
    acc = acc + jnp.dot(mb_ref[...], wb_ref[...], preferred_element_type=F32)
    o_ref[...] = x_ref[...] + acc


def _outproj(mix_a, mix_b, w_bf16, x2d, *, tm):
    n, d = x2d.shape
    half = mix_a.shape[1]
    const = pl.Buffered(1)
    return pl.pallas_call(
        _outproj_kernel,
        grid=(n // tm,),
        in_specs=[
            pl.BlockSpec((tm, half), lambda i: (i, 0)),
            pl.BlockSpec((tm, half), lambda i: (i, 0)),
            pl.BlockSpec((half, d), lambda i: (0, 0), pipeline_mode=const),
            pl.BlockSpec((half, d), lambda i: (1, 0), pipeline_mode=const),
            pl.BlockSpec((tm, d), lambda i: (i, 0)),
        ],
        out_specs=pl.BlockSpec((tm, d), lambda i: (i, 0)),
        out_shape=jax.ShapeDtypeStruct((n, d), F32),
        compiler_params=_cparams(("parallel",)),
        name="outproj",
    )(mix_a, mix_b, w_bf16, w_bf16, x2d)


CONV_TS = 256
CONV_RC = 32
CONV_NR = 32


class ConvRefs:
    def __init__(self, lo, hi, plo, phi, nlo, nhi, gate, w, b, lg, lb, out, xs, wb, y):
        self.lo, self.hi, self.plo, self.phi, self.nlo, self.nhi = lo, hi, plo, phi, nlo, nhi
        self.gate, self.w, self.b, self.lg, self.lb, self.out = gate, w, b, lg, lb, out
        self.xs, self.wb, self.y = xs, wb, y


def _conv_prepare(cr, first, last):
    ts, c = cr.y.shape
    lw = c // 2
    padded = ts + 2 * HALO

    def glu(a, b):
        return a.astype(F32) * _sigmoid(b.astype(F32))

    cr.xs[0, HALO:HALO + ts, :] = glu(cr.lo[...], cr.hi[...])
    cr.xs[0, 0:HALO, :] = jnp.where(first, 0.0, glu(cr.plo[...], cr.phi[...]))
    cr.xs[0, HALO + ts:padded, :] = jnp.where(last, 0.0, glu(cr.nlo[...], cr.nhi[...]))
    cr.xs[0, padded:padded + SUBLANES, :] = jnp.zeros((SUBLANES, c), F32)
    for k in range(CONV_W):
        cr.wb[k] = jnp.broadcast_to(cr.w[k:k + 1, :], (SUBLANES, c))

    def shift_chunk(q, carry):
        row = pl.multiple_of(q * CONV_RC, CONV_RC)
        for col in range(0, c, lw):
            win = cr.xs[0, pl.ds(row, CONV_RC + SUBLANES), col:col + lw]
            for r in range(1, SUBLANES):
                cr.xs[r, pl.ds(row, CONV_RC), col:col + lw] = pltpu.roll(
                    win, CONV_RC + SUBLANES - r, axis=0)[:CONV_RC]
        return carry

    lax.fori_loop(0, padded // CONV_RC, shift_chunk, 0)


CONV_LANE_PARTS = 2


def _conv_taps(cr, q, parts=range(CONV_LANE_PARTS)):
    c = cr.y.shape[1]
    lw = c // CONV_LANE_PARTS
    base0 = HALO - CONV_PAD
    groups = CONV_RC // SUBLANES
    row = pl.multiple_of(q * CONV_RC, CONV_RC)
    for col in [part * lw for part in parts]:
        acc = [jnp.zeros((SUBLANES, lw), F32) for _ in range(groups)]
        for k in range(CONV_W):
            a, r = divmod(base0 + k, SUBLANES)
            wk = cr.wb[k, :, col:col + lw]
            for g in range(groups):
                x = cr.xs[r, pl.ds(row + SUBLANES * (a + g), SUBLANES), col:col + lw]
                acc[g] = acc[g] + x * wk
        bias = cr.b[:, col:col + lw]
        for g in range(groups):
            cr.y[pl.ds(row + SUBLANES * g, SUBLANES), col:col + lw] = acc[g] + bias


def _conv_finish(cr):
    ts = cr.y.shape[0]

    def chunk(r, carry):
        row = pl.multiple_of(r * CONV_NR, CONV_NR)
        y = cr.y[pl.ds(row, CONV_NR), :]
        mu = jnp.mean(y, axis=-1, keepdims=True)
        yc = y - mu
        var = jnp.mean(yc * yc, axis=-1, keepdims=True)
        z = yc * lax.rsqrt(var + EPS) * cr.lg[...] + cr.lb[...]
        out = _silu(z) * _silu(cr.gate[pl.ds(row, CONV_NR), :].astype(F32))
        cr.out[pl.ds(row, CONV_NR), :] = out.astype(BF16)
        return carry

    lax.fori_loop(0, ts // CONV_NR, chunk, 0, unroll=2)


def _conv_specs(index_of_step, n_rows):
    c = W_HALF
    hb = CONV_TS // HALO
    nhb = n_rows // HALO

    def rows(col):
        return pl.BlockSpec((CONV_TS, c), lambda *g: (index_of_step(*g), col))

    def prev(col):
        return pl.BlockSpec((HALO, c), lambda *g: (jnp.maximum(index_of_step(*g) * hb - 1, 0), col))

    def nxt(col):
        return pl.BlockSpec(
            (HALO, c), lambda *g: (jnp.minimum((index_of_step(*g) + 1) * hb, nhb - 1), col))

    vec = pl.BlockSpec((1, c), lambda *g: (0, 0))
    in_specs = [rows(EVEN_A_LO), rows(EVEN_A_HI), prev(EVEN_A_LO), prev(EVEN_A_HI),
                nxt(EVEN_A_LO), nxt(EVEN_A_HI), rows(EVEN_A_GATE),
                pl.BlockSpec((CONV_W, c), lambda *g: (0, 0)), vec, vec, vec]
    out_spec = pl.BlockSpec((CONV_TS, c), lambda *g: (index_of_step(*g), 0))
    scratch = [pltpu.VMEM((SUBLANES, CONV_TS + 2 * HALO + SUBLANES, c), F32),
               pltpu.VMEM((CONV_W, SUBLANES, c), F32),
               pltpu.VMEM((CONV_TS, c), F32)]
    return in_specs, out_spec, scratch


def _bucket_thresholds():
    nb = N_BUCKETS // 2
    max_exact = nb // 2
    n = np.arange(1, 4 * MAX_DIST, dtype=np.float64)
    large = max_exact + (np.log(n / max_exact) / math.log(MAX_DIST / max_exact)
                         * (nb - max_exact)).astype(np.int64)
    large = np.minimum(large, nb - 1)
    thr = [int(n[np.argmax(large >= b)]) for b in range(max_exact + 1, nb)]
    return max_exact, thr


def _bias_kernel(rb_ref, qg_ref, kg_ref, o_ref, range_ref, *, tk, lo_diag, q_scale):
    h = pl.program_id(0)
    d = pl.program_id(1) + lo_diag
    gq = jnp.abs(qg_ref[0])
    gk = jnp.abs(kg_ref[0])
    for c in range(1, DH):
        gq = jnp.maximum(gq, jnp.abs(qg_ref[c]))
        gk = jnp.maximum(gk, jnp.abs(kg_ref[c]))
    qk_bound = gq * gk * (DH * q_scale * BF16_SLACK)
    bmax = rb_ref[0, h]
    bmin = rb_ref[0, h]
    for b in range(1, N_BUCKETS):
        bmax = jnp.maximum(bmax, rb_ref[b, h])
        bmin = jnp.minimum(bmin, rb_ref[b, h])
    shift = qk_bound + bmax * LOG2E
    range_ref[h] = 2.0 * qk_bound + (bmax - bmin) * LOG2E

    @pl.when(jnp.abs(d) < FAR)
    def _():
        row = lax.broadcasted_iota(jnp.int32, (tk, tk), 0)
        col = lax.broadcasted_iota(jnp.int32, (tk, tk), 1)
        rel = col - row + d * tk
        n = jnp.abs(rel)
        max_exact, thr = _bucket_thresholds()
        bucket = jnp.minimum(n, max_exact)
        for t in thr:
            bucket = bucket + jnp.where(n >= t, 1, 0)
        bucket = bucket + jnp.where(rel > 0, N_BUCKETS // 2, 0)
        val = jnp.zeros((tk, tk), F32)
        for b in range(N_BUCKETS):
            val = jnp.where(bucket == b, rb_ref[b, h], val)
        o_ref[0, 0] = val * LOG2E - shift

    @pl.when(d <= -FAR)
    def _():
        o_ref[0, 0] = jnp.full((tk, tk), rb_ref[N_BUCKETS // 2 - 1, h] * LOG2E - shift, F32)

    @pl.when(d >= FAR)
    def _():
        o_ref[0, 0] = jnp.full((tk, tk), rb_ref[N_BUCKETS - 1, h] * LOG2E - shift, F32)


EXP2_SAFE_RANGE = 100.0
BF16_SLACK = 1.01
N_DIAG = 5
FAR = N_DIAG // 2


def _bias_tiles(rel_bias, q_gain, k_gain, *, tk, q_scale):
    assert tk + 1 >= _bucket_thresholds()[1][-1]
    nd = N_DIAG
    smem = pl.BlockSpec(memory_space=pltpu.SMEM)
    kern = functools.partial(_bias_kernel, tk=tk, lo_diag=-FAR, q_scale=q_scale)
    return pl.pallas_call(
        kern,
        grid=(N_HEADS, nd),
        in_specs=[smem, smem, smem],
        out_specs=[pl.BlockSpec((1, 1, tk, tk), lambda h, d: (h, d, 0, 0)), smem],
        out_shape=[jax.ShapeDtypeStruct((N_HEADS, nd, tk, tk), F32),
                   jax.ShapeDtypeStruct((N_HEADS,), F32)],
        compiler_params=_cparams(("arbitrary", "arbitrary")),
        name="bias_tiles",
    )(rel_bias, q_gain, k_gain)


def _sub_head_norm(x, gain, scale):
    parts = []
    for t in range(x.shape[1] // DH):
        blk = x[:, t * DH:(t + 1) * DH]
        ms = jnp.mean(blk * blk, axis=-1, keepdims=True)
        parts.append(blk * lax.rsqrt(ms + EPS) * gain * scale)
    return jnp.concatenate(parts, axis=-1)


def _knorm_kernel(k_ref, g_ref, o_ref):
    o_ref[...] = _sub_head_norm(k_ref[...].astype(F32), g_ref[...], 1.0).astype(BF16)


def _knorm(p, k_gain, *, tr):
    n = p.shape[0]
    return pl.pallas_call(
        _knorm_kernel,
        grid=(n // tr,),
        in_specs=[pl.BlockSpec((tr, W_HALF), lambda i: (i, EVEN_K)),
                  pl.BlockSpec((1, DH), lambda i: (0, 0))],
        out_specs=pl.BlockSpec((tr, W_HALF), lambda i: (i, 0)),
        out_shape=jax.ShapeDtypeStruct((n, W_HALF), BF16),
        compiler_params=_cparams(("parallel",)),
        name="knorm",
    )(p, k_gain)


def _attn_kernel(range_ref, q_ref, k_ref, v_ref, bias_ref, gate_ref, qg_ref, lq1_ref, lk1_ref,
                 lq2_ref, lk2_ref, sg_ref, *rest, tq, tk, nkv, lam_init, q_scale, unroll,
                 conv_tiles_per_seq):
    conv_in = rest[:11]
    o_ref, conv_out = rest[11:13]
    qx_ref, acc_ref, ls_ref, l_ref = rest[13:17]
    cr = ConvRefs(*conv_in, conv_out, *rest[17:20])
    h = pl.program_id(1)
    i = pl.program_id(2)
    contract_last = (((1,), (1,)), ((), ()))

    step = (pl.program_id(0) * pl.num_programs(1) + h) * pl.num_programs(2) + i
    seq_pos = step % conv_tiles_per_seq
    _conv_prepare(cr, seq_pos == 0, seq_pos == conv_tiles_per_seq - 1)
    conv_chunks = CONV_TS // CONV_RC

    qx_ref[...] = _sub_head_norm(q_ref[...].astype(F32), qg_ref[...], q_scale).astype(BF16)
    acc_ref[...] = jnp.zeros_like(acc_ref)

    nsub = tq // tk

    def shifted_scores(j, t):
        rows = pl.ds(pl.multiple_of(j * tk, tk), tk)
        s = lax.dot_general(qx_ref[:, t * DH:(t + 1) * DH], k_ref[rows, t * DH:(t + 1) * DH],
                            contract_last, preferred_element_type=F32)
        parts = []
        for a in range(nsub):
            diag = jnp.clip(j - (i * nsub + a), -FAR, FAR) + FAR
            parts.append(s[a * tk:(a + 1) * tk] + bias_ref[0, diag])
        return jnp.concatenate(parts, axis=0)

    def bounded():
        ls_ref[...] = jnp.zeros_like(ls_ref)
        trips = nkv // unroll
        chunks_per_trip = conv_chunks // trips

        pieces_per_tile = chunks_per_trip * CONV_LANE_PARTS // unroll

        def body(jj, carry):
            for u in range(unroll):
                for piece in range(u * pieces_per_tile, (u + 1) * pieces_per_tile):
                    cc, part = divmod(piece, CONV_LANE_PARTS)
                    _conv_taps(cr, jj * chunks_per_trip + cc, [part])
                j = unroll * jj + u
                vv = v_ref[pl.ds(pl.multiple_of(j * tk, tk), tk), :]
                for t in range(2):
                    p = jnp.exp2(shifted_scores(j, t))
                    lane_sums = p[:, :DH]
                    for c in range(1, tk // DH):
                        lane_sums = lane_sums + p[:, c * DH:(c + 1) * DH]
                    ls_ref[t] = ls_ref[t] + lane_sums
                    acc_ref[t] = acc_ref[t] + jnp.dot(p.astype(BF16), vv,
                                                      preferred_element_type=F32)
            return carry

        lax.fori_loop(0, trips, body, 0)
        for t in range(2):
            l_ref[t] = jnp.sum(ls_ref[t], axis=-1, keepdims=True)

    def running_max():
        def taps(q, carry):
            _conv_taps(cr, q)
            return carry
        lax.fori_loop(0, conv_chunks, taps, 0)

        def body(j, ml):
            vv = v_ref[pl.ds(pl.multiple_of(j * tk, tk), tk), :]
            out = []
            for t in range(2):
                m, l = ml[2 * t], ml[2 * t + 1]
                s = shifted_scores(j, t)
                mn = jnp.maximum(m, jnp.max(s, axis=-1, keepdims=True))
                p = jnp.exp2(s - mn)
                a = jnp.exp2(m - mn)
                l = a * l + jnp.sum(p, axis=-1, keepdims=True)
                acc_ref[t] = a * acc_ref[t] + jnp.dot(p.astype(BF16), vv,
                                                      preferred_element_type=F32)
                out += [mn, l]
            return tuple(out)

        neg = jnp.full((tq, 1), -jnp.inf, F32)
        zero = jnp.zeros((tq, 1), F32)
        _, l1, _, l2 = lax.fori_loop(0, nkv, body, (neg, zero, neg, zero))
        l_ref[0] = l1
        l_ref[1] = l2

    lax.cond(range_ref[h] <= EXP2_SAFE_RANGE, bounded, running_max)
    _conv_finish(cr)

    lam =(jnp.exp(jnp.sum(lq1_ref[...] * lk1_ref[...], axis=-1, keepdims=True))
           - jnp.exp(jnp.sum(lq2_ref[...] * lk2_ref[...], axis=-1, keepdims=True)) + lam_init)
    o = acc_ref[0] * (1.0 / l_ref[0]) - lam * (acc_ref[1] * (1.0 / l_ref[1]))
    ms = jnp.mean(o * o, axis=-1, keepdims=True)
    y = o * lax.rsqrt(ms + EPS) * sg_ref[...] * (1.0 - lam_init)
    o_ref[...] = (y * _silu(gate_ref[...].astype(F32))).astype(BF16)


def _attention_and_conv(exp_range, p, kx, bias_tiles, q_gain, lq1, lk1, lq2, lk2, subln_g, conv_w,
                        conv_b, conv_ln_g, conv_ln_b, *, batch, seq, tq, tk, lam_init, q_scale):
    n = p.shape[0]
    nq = seq // tq
    nd = bias_tiles.shape[1]
    nkv = seq // tk
    unroll = 4
    trips = nkv // unroll
    conv_chunks = CONV_TS // CONV_RC
    assert tq % tk == 0 and bias_tiles.shape[2:] == (tk, tk) and nkv % unroll == 0
    assert batch * N_HEADS * nq * CONV_TS == n and conv_chunks % trips == 0
    assert seq % CONV_TS == 0 and (CONV_TS + 2 * HALO) % CONV_RC == 0
    per_head = W_HALF // DV

    def q_rows(seg):
        return pl.BlockSpec((tq, DV), lambda b, h, i: (b * nq + i, seg * per_head + h))

    vec = pl.BlockSpec((1, DH), lambda b, h, i: (0, 0))
    conv_in, conv_out, conv_scratch = _conv_specs(lambda b, h, i: (b * N_HEADS + h) * nq + i, n)
    kern = functools.partial(_attn_kernel, tq=tq, tk=tk, nkv=nkv, lam_init=lam_init,
                             q_scale=q_scale, unroll=unroll, conv_tiles_per_seq=seq // CONV_TS)
    return pl.pallas_call(
        kern,
        grid=(batch, N_HEADS, nq),
        in_specs=[
            pl.BlockSpec(memory_space=pltpu.SMEM),
            q_rows(EVEN_Q),
            pl.BlockSpec((seq, DV), lambda b, h, i: (b, h)),
            pl.BlockSpec((seq, DV), lambda b, h, i: (b, EVEN_V * per_head + h)),
            pl.BlockSpec((1, nd, tk, tk), lambda b, h, i: (h, 0, 0, 0),
                         pipeline_mode=pl.Buffered(1)),
            q_rows(EVEN_B_GATE),
            vec, vec, vec, vec, vec,
            pl.BlockSpec((1, DV), lambda b, h, i: (0, 0)),
        ] + conv_in,
        out_specs=[pl.BlockSpec((tq, DV), lambda b, h, i: (b * nq + i, h)), conv_out],
        out_shape=[jax.ShapeDtypeStruct((n, N_HEADS * DV), BF16),
                   jax.ShapeDtypeStruct((n, W_HALF), BF16)],
        scratch_shapes=[
            pltpu.VMEM((tq, DV), BF16),
            pltpu.VMEM((2, tq, DV), F32),
            pltpu.VMEM((2, tq, DH), F32),
            pltpu.VMEM((2, tq, 1), F32),
        ] + conv_scratch,
        compiler_params=_cparams(("parallel", "parallel", "parallel")),
        name="diff_attention",
    )(exp_range, p, kx, p, bias_tiles, p, q_gain, lq1, lk1, lq2, lk2, subln_g,
      p, p, p, p, p, p, p, conv_w, conv_b, conv_ln_g, conv_ln_b)


ROWS_BF16 = 16


def _fnet_constants(seq, ch):
    r = seq // FFT_INNER
    two_pi = 2.0 * np.pi
    ang_r = two_pi * np.outer(np.arange(r), np.arange(r)) / r
    base = np.stack([np.cos(ang_r), -np.sin(ang_r)], axis=1).reshape(2 * r, r)
    w1 = np.kron(base, np.eye(ROWS_BF16))
    ang_tw = two_pi * np.outer(np.arange(FFT_INNER), np.arange(r)) / seq
    tw = np.stack([np.cos(ang_tw), np.sin(ang_tw)])
    tw = tw.reshape(2, FFT_INNER, r // ROWS_BF16, ROWS_BF16).transpose(2, 0, 1, 3)
    ang_i = two_pi * np.outer(np.arange(FFT_INNER), np.arange(FFT_INNER)) / FFT_INNER
    c, s = np.cos(ang_i), np.sin(ang_i)
    w2 = np.block([[c, s], [-s, c]])
    scale = 1.0 / math.sqrt(seq * GROUP_C)
    eye_g = np.eye(ch // GROUP_C)
    bdc = np.kron(eye_g, c * scale)
    bds = np.kron(eye_g, s * scale)
    perm = np.zeros((ROWS_BF16 * ROWS_BF16,) * 2)
    for b in range(ROWS_BF16):
        for f in range(ROWS_BF16):
            perm[b * ROWS_BF16 + f, f * ROWS_BF16 + b] = 1.0
    as_bf16 = lambda a: jnp.asarray(a, dtype=BF16)
    return (as_bf16(w1), jnp.asarray(tw, dtype=F32), as_bf16(w2), as_bf16(bdc), as_bf16(bds),
            as_bf16(perm))


def _fft1_kernel(x_ref, w_ref, o_ref):
    _, r, rows, c = x_ref.shape
    x = x_ref[0].reshape(r * rows, c)
    y = jnp.dot(w_ref[...], x, preferred_element_type=F32).astype(BF16)
    o_ref[0] = y.reshape(r, 2, rows, c)


def _fft2_kernel(a_ref, tw_ref, w2_ref, bdc_ref, bds_ref, perm_ref, gate_ref, o_ref, u_ref, y_ref):
    nf = a_ref.shape[1]
    ch = a_ref.shape[-1]
    for kk in range(nf):
        br = a_ref[0, kk, 0].astype(F32)
        bi = a_ref[0, kk, 1].astype(F32)
        cw = tw_ref[0, 0][:, kk:kk + 1]
        sw = tw_ref[0, 1][:, kk:kk + 1]
        x = jnp.concatenate([br * cw + bi * sw, bi * cw - br * sw], axis=0).astype(BF16)
        u = jnp.dot(w2_ref[...], x, preferred_element_type=F32).astype(BF16)
        u_ref[0, kk * FFT_INNER:(kk + 1) * FFT_INNER, :] = u[:FFT_INNER]
        u_ref[1, kk * FFT_INNER:(kk + 1) * FFT_INNER, :] = u[FFT_INNER:]
    y_ref[...] = (jnp.dot(u_ref[0], bdc_ref[...], preferred_element_type=F32)
                  + jnp.dot(u_ref[1], bds_ref[...], preferred_element_type=F32)).astype(BF16)
    for a in range(FFT_INNER // ROWS_BF16):
        lo = a * ROWS_BF16
        piece = jnp.concatenate(
            [y_ref[k * FFT_INNER + lo:k * FFT_INNER + lo + ROWS_BF16, :] for k in range(nf)], axis=0)
        z = jnp.dot(perm_ref[...], piece, preferred_element_type=F32)
        g = gate_ref[0, lo:lo + ROWS_BF16].reshape(ROWS_BF16 * nf, ch).astype(F32)
        o_ref[0, lo:lo + ROWS_BF16] = (z * _silu(g)).astype(BF16).reshape(ROWS_BF16, nf, ch)


def _fnet(p, *, batch, seq, ch=512):
    c = W_HALF
    r = seq // FFT_INNER
    nf = ROWS_BF16
    assert r % nf == 0 and c % ch == 0
    w1, tw, w2, bdc, bds, perm = _fnet_constants(seq, ch)
    const = pl.Buffered(1)
    stage1 = pl.pallas_call(
        _fft1_kernel,
        grid=(batch, FFT_INNER // nf),
        in_specs=[
            pl.BlockSpec((1, r, nf, c), lambda b, t: (b, 0, t, ODD_C_IN)),
            pl.BlockSpec((2 * r * nf, r * nf), lambda b, t: (0, 0), pipeline_mode=const),
        ],
        out_specs=pl.BlockSpec((1, r, 2, nf, c), lambda b, t: (b, 0, 0, t, 0)),
        out_shape=jax.ShapeDtypeStruct((batch, r, 2, FFT_INNER, c), BF16),
        compiler_params=_cparams(("parallel", "parallel")),
        name="fft_stage1",
    )(p.reshape(batch, r, FFT_INNER, p.shape[1]), w1)
    gate_cols = ODD_C_GATE * (c // ch)
    cmat = lambda shape: pl.BlockSpec(shape, lambda b, f, j: (0,) * len(shape), pipeline_mode=const)
    out = pl.pallas_call(
        _fft2_kernel,
        grid=(batch, r // nf, c // ch),
        in_specs=[
            pl.BlockSpec((1, nf, 2, FFT_INNER, ch), lambda b, f, j: (b, f, 0, 0, j)),
            pl.BlockSpec((1, 2, FFT_INNER, nf), lambda b, f, j: (f, 0, 0, 0)),
            cmat((2 * FFT_INNER, 2 * FFT_INNER)), cmat((ch, ch)), cmat((ch, ch)),
            cmat((nf * nf, nf * nf)),
            pl.BlockSpec((1, FFT_INNER, nf, ch), lambda b, f, j: (b, 0, f, gate_cols + j)),
        ],
        out_specs=pl.BlockSpec((1, FFT_INNER, nf, ch), lambda b, f, j: (b, 0, f, j)),
        out_shape=jax.ShapeDtypeStruct((batch, FFT_INNER, r, c), BF16),
        scratch_shapes=[pltpu.VMEM((2, nf * FFT_INNER, ch), BF16),
                        pltpu.VMEM((nf * FFT_INNER, ch), BF16)],
        compiler_params=_cparams(("parallel", "parallel", "parallel")),
        name="fft_stage2",
    )(stage1, tw, w2, bdc, bds, perm, p.reshape(batch, FFT_INNER, r, p.shape[1]))
    return out.reshape(batch * seq, c)


def _sgu_kernel(u_ref, v_ref, gate_ref, lg_ref, lb_ref, ws_ref, bt_ref, o_ref, *, tr):
    v = v_ref[...].astype(F32)
    mu = jnp.mean(v, axis=-1, keepdims=True)
    vc = v - mu
    var = jnp.mean(vc * vc, axis=-1, keepdims=True)
    vn = (vc * lax.rsqrt(var + EPS) * lg_ref[...] + lb_ref[...]).astype(BF16)
    for g in range(SGU_GROUPS):
        cols = slice(g * SGU_DG, (g + 1) * SGU_DG)
        bcol = jnp.broadcast_to(bt_ref[:, g:g + 1], (SGU_CHUNK, SGU_DG))
        for n in range(tr // SGU_CHUNK):
            rows = slice(n * SGU_CHUNK, (n + 1) * SGU_CHUNK)
            sv = jnp.dot(ws_ref[g], vn[rows, cols], preferred_element_type=F32) + bcol
            out = u_ref[rows, cols].astype(F32) * sv * _silu(gate_ref[rows, cols].astype(F32))
            o_ref[rows, cols] = out.astype(BF16)


def _sgu(p, ln_g, ln_b, ws_bf16, b_t, *, tr):
    n = p.shape[0]
    c = W_HALF

    def seg(s):
        return pl.BlockSpec((tr, c), lambda i: (i, s))

    vec = pl.BlockSpec((1, c), lambda i: (0, 0))
    return pl.pallas_call(
        functools.partial(_sgu_kernel, tr=tr),
        grid=(n // tr,),
        in_specs=[
            seg(ODD_U), seg(ODD_V), seg(ODD_D_GATE), vec, vec,
            pl.BlockSpec((SGU_GROUPS, SGU_CHUNK, SGU_CHUNK), lambda i: (0, 0, 0)),
            pl.BlockSpec((SGU_CHUNK, SGU_GROUPS), lambda i: (0, 0)),
        ],
        out_specs=pl.BlockSpec((tr, c), lambda i: (i, 0)),
        out_shape=jax.ShapeDtypeStruct((n, c), BF16),
        compiler_params=_cparams(("parallel",)),
        name="sgu",
    )(p, p, p, ln_g, ln_b, ws_bf16, b_t)


QK_SCALE_LOG2 = LOG2E / math.sqrt(DH)


def _attention_bias(p, *, tk):
    return _bias_tiles(p["rel_bias"], p["q_norm_g"], p["k_norm_g"], tk=tk, q_scale=QK_SCALE_LOG2)


def _trunk(x, p, bias_tiles, exp_range, *, tq, tk):
    batch, seq, d = x.shape
    x2d = x.reshape(batch * seq, d)
    row = lambda a: a.reshape(1, -1)

    lam_init = 0.8 - 0.6 * math.exp(-0.3 * 0)
    pe = _inproj(x2d, row(p["norm_g"][0]), p["w_in_even"], tm=1024, tn=1024)
    kx = _knorm(pe, row(p["k_norm_g"]), tr=1024)
    mix_b, mix_a = _attention_and_conv(
        exp_range, pe, kx, bias_tiles, row(p["q_norm_g"]), row(p["lam_q1"]), row(p["lam_k1"]),
        row(p["lam_q2"]), row(p["lam_k2"]), row(p["subln_g"]), p["conv_w"], row(p["conv_b"]),
        row(p["conv_ln_g"]), row(p["conv_ln_b"]), batch=batch, seq=seq, tq=tq, tk=tk,
        lam_init=lam_init, q_scale=QK_SCALE_LOG2)
    x1 = _outproj(mix_a, mix_b, p["w_out_even"], x2d, tm=512)

    po = _inproj(x1, row(p["norm_g"][1]), p["w_in_odd"], tm=1024, tn=1024)
    mix_c = _fnet(po, batch=batch, seq=seq)
    mix_d = _sgu(po, row(p["sgu_ln_g"]), row(p["sgu_ln_b"]), p["sgu_w"], p["sgu_b"].T, tr=512)
    y = _outproj(mix_c, mix_d, p["w_out_odd"], x1, tm=512)
    return y.reshape(batch, seq, d)


def kernel(x_prompt, x_sample, norm_g, w_in_even, conv_w, conv_b, conv_ln_g, conv_ln_b,
           q_norm_g, k_norm_g, lam_q1, lam_k1, lam_q2, lam_k2, subln_g, rel_bias, w_out_even,
           w_in_odd, sgu_ln_g, sgu_ln_b, sgu_w, sgu_b, w_out_odd):
    p = dict(
        norm_g=norm_g, w_in_even=w_in_even[0].astype(BF16), conv_w=conv_w[0], conv_b=conv_b[0],
        conv_ln_g=conv_ln_g[0], conv_ln_b=conv_ln_b[0], q_norm_g=q_norm_g[0],
        k_norm_g=k_norm_g[0], lam_q1=lam_q1[0], lam_k1=lam_k1[0], lam_q2=lam_q2[0],
        lam_k2=lam_k2[0], subln_g=subln_g[0], w_out_even=w_out_even[0].astype(BF16),
        w_in_odd=w_in_odd[0].astype(BF16), sgu_ln_g=sgu_ln_g[0], sgu_ln_b=sgu_ln_b[0],
        sgu_w=sgu_w[0].astype(BF16), sgu_b=sgu_b[0], w_out_odd=w_out_odd[0].astype(BF16),
        rel_bias=rel_bias)
    tq, tk = 1024, 512
    bias_tiles, exp_range = _attention_bias(p, tk=tk)
    y_prompt = _trunk(x_prompt, p, bias_tiles, exp_range, tq=tq, tk=tk)
    y_sample = _trunk(x_sample, p, bias_tiles, exp_range, tq=tq, tk=tk)
    return (y_prompt, y_sample)
```

```python
import functools
import math

import numpy as np
import jax
import jax.numpy as jnp
from jax import lax
from jax.experimental import pallas as pl
from jax.experimental.pallas import tpu as pltpu

F32 = jnp.float32
BF16 = jnp.bfloat16

EPS = 1e-6
LOG2E = math.log2(math.e)

D_MODEL = 2048
W_HALF = D_MODEL // 2
DH = 128
N_HEADS = 4
DV = 2 * DH
CONV_W = 31
CONV_PAD = CONV_W // 2
N_BUCKETS = 32
MAX_DIST = 128
FFT_INNER = 128
GROUP_C = 128
SGU_GROUPS = 4
SGU_CHUNK = 128
SGU_DG = W_HALF // SGU_GROUPS

EVEN_A_LO, EVEN_A_HI, EVEN_A_GATE, EVEN_Q, EVEN_K, EVEN_V, EVEN_B_GATE = range(7)
ODD_C_IN, ODD_C_GATE, ODD_U, ODD_V, ODD_D_GATE = range(5)

VMEM_LIMIT_V7X = 56 * 1024 * 1024
HALO = 16
SUBLANES = 8


def _cparams(sem):
    return pltpu.CompilerParams(dimension_semantics=sem, vmem_limit_bytes=VMEM_LIMIT_V7X)


def _sigmoid(x):
    return 0.5 * jnp.tanh(0.5 * x) + 0.5


def _silu(x):
    return x * _sigmoid(x)


def _inproj_kernel(x_ref, g_ref, w_ref, o_ref, h_ref):
    @pl.when(pl.program_id(1) == 0)
    def _():
        x = x_ref[...]
        ms = jnp.mean(x * x, axis=-1, keepdims=True)
        h_ref[...] = (x * lax.rsqrt(ms + EPS) * g_ref[...]).astype(BF16)

    o_ref[...] = jnp.dot(h_ref[...], w_ref[...], preferred_element_type=F32).astype(BF16)


def _inproj(x2d, g, w_bf16, *, tm, tn):
    n, d = x2d.shape
    p = w_bf16.shape[1]
    return pl.pallas_call(
        _inproj_kernel,
        grid=(n // tm, p // tn),
        in_specs=[
            pl.BlockSpec((tm, d), lambda i, j: (i, 0)),
            pl.BlockSpec((1, d), lambda i, j: (0, 0)),
            pl.BlockSpec((d, tn), lambda i, j: (0, j)),
        ],
        out_specs=pl.BlockSpec((tm, tn), lambda i, j: (i, j)),
        out_shape=jax.ShapeDtypeStruct((n, p), BF16),
        scratch_shapes=[pltpu.VMEM((tm, d), BF16)],
        compiler_params=_cparams(("parallel", "arbitrary")),
        name="inproj",
    )(x2d, g, w_bf16)


def _outproj_kernel(ma_ref, *rest, n_sgu_inputs):
    if n_sgu_inputs:
        sgu_in = rest[:n_sgu_inputs]
        wa_ref, wb_ref, x_ref, o_ref, mb_ref = rest[n_sgu_inputs:]
    else:
        mb_ref, wa_ref, wb_ref, x_ref, o_ref = rest
    acc = jnp.dot(ma_ref[...], wa_ref[...], preferred_element_type=F32)
    if n_sgu_inputs:
        _sgu_kernel(*sgu_in, mb_ref, tr=mb_ref.shape[0])
    acc = acc + jnp.dot(mb_ref[...], wb_ref[...], preferred_element_type=F32)
    o_ref[...] = x_ref[...] + acc


def _outproj(mix_a, mix_b, w_bf16, x2d, *, tm, sgu=None):
    n, d = x2d.shape
    half = mix_a.shape[1]
    const = pl.Buffered(1)
    rows = pl.BlockSpec((tm, half), lambda i: (i, 0))
    if sgu is None:
        second, second_specs, scratch = [mix_b], [rows], []
    else:
        second, second_specs = sgu
        scratch = [pltpu.VMEM((tm, half), BF16)]
    return pl.pallas_call(
        functools.partial(_outproj_kernel, n_sgu_inputs=0 if sgu is None else len(second)),
        grid=(n // tm,),
        in_specs=[rows] + second_specs + [
            pl.BlockSpec((half, d), lambda i: (0, 0), pipeline_mode=const),
            pl.BlockSpec((half, d), lambda i: (1, 0), pipeline_mode=const),
            pl.BlockSpec((tm, d), lambda i: (i, 0)),
        ],
        out_specs=pl.BlockSpec((tm, d), lambda i: (i, 0)),
        out_shape=jax.ShapeDtypeStruct((n, d), F32),
        scratch_shapes=scratch,
        compiler_params=_cparams(("parallel",)),
        name="outproj",
    )(mix_a, *second, w_bf16, w_bf16, x2d)


CONV_TS = 256
CONV_RC = 32
CONV_NR = 32


class ConvRefs:
    def __init__(self, lo, hi, plo, phi, nlo, nhi, gate, w, b, lg, lb, out, xs, wb, y):
        self.lo, self.hi, self.plo, self.phi, self.nlo, self.nhi = lo, hi, plo, phi, nlo, nhi
        self.gate, self.w, self.b, self.lg, self.lb, self.out = gate, w, b, lg, lb, out
        self.xs, self.wb, self.y = xs, wb, y


def _conv_prepare(cr, first, last):
    ts, c = cr.y.shape
    lw = c // 2
    padded = ts + 2 * HALO

    def glu(a, b):
        return a.astype(F32) * _sigmoid(b.astype(F32))

    cr.xs[0, HALO:HALO + ts, :] = glu(cr.lo[...], cr.hi[...])
    cr.xs[0, 0:HALO, :] = jnp.where(first, 0.0, glu(cr.plo[...], cr.phi[...]))
    cr.xs[0, HALO + ts:padded, :] = jnp.where(last, 0.0, glu(cr.nlo[...], cr.nhi[...]))
    cr.xs[0, padded:padded + SUBLANES, :] = jnp.zeros((SUBLANES, c), F32)
    for k in range(CONV_W):
        cr.wb[k] = jnp.broadcast_to(cr.w[k:k + 1, :], (SUBLANES, c))

    def shift_chunk(q, carry):
        row = pl.multiple_of(q * CONV_RC, CONV_RC)
        for col in range(0, c, lw):
            win = cr.xs[0, pl.ds(row, CONV_RC + SUBLANES), col:col + lw]
            for r in range(1, SUBLANES):
                cr.xs[r, pl.ds(row, CONV_RC), col:col + lw] = pltpu.roll(
                    win, CONV_RC + SUBLANES - r, axis=0)[:CONV_RC]
        return carry

    lax.fori_loop(0, padded // CONV_RC, shift_chunk, 0)


CONV_LANE_PARTS = 2


def _conv_taps(cr, q, parts=range(CONV_LANE_PARTS)):
    c = cr.y.shape[1]
    lw = c // CONV_LANE_PARTS
    base0 = HALO - CONV_PAD
    groups = CONV_RC // SUBLANES
    row = pl.multiple_of(q * CONV_RC, CONV_RC)
    for col in [part * lw for part in parts]:
        acc = [jnp.zeros((SUBLANES, lw), F32) for _ in range(groups)]
        for k in range(CONV_W):
            a, r = divmod(base0 + k, SUBLANES)
            wk = cr.wb[k, :, col:col + lw]
            for g in range(groups):
                x = cr.xs[r, pl.ds(row + SUBLANES * (a + g), SUBLANES), col:col + lw]
                acc[g] = acc[g] + x * wk
        bias = cr.b[:, col:col + lw]
        for g in range(groups):
            cr.y[pl.ds(row + SUBLANES * g, SUBLANES), col:col + lw] = acc[g] + bias


def _conv_finish(cr):
    ts = cr.y.shape[0]

    def chunk(r, carry):
        row = pl.multiple_of(r * CONV_NR, CONV_NR)
        y = cr.y[pl.ds(row, CONV_NR), :]
        mu = jnp.mean(y, axis=-1, keepdims=True)
        yc = y - mu
        var = jnp.mean(yc * yc, axis=-1, keepdims=True)
        z = yc * lax.rsqrt(var + EPS) * cr.lg[...] + cr.lb[...]
        out = _silu(z) * _silu(cr.gate[pl.ds(row, CONV_NR), :].astype(F32))
        cr.out[pl.ds(row, CONV_NR), :] = out.astype(BF16)
        return carry

    lax.fori_loop(0, ts // CONV_NR, chunk, 0, unroll=2)


def _conv_specs(index_of_step, n_rows):
    c = W_HALF
    hb = CONV_TS // HALO
    nhb = n_rows // HALO

    def rows(col):
        return pl.BlockSpec((CONV_TS, c), lambda *g: (index_of_step(*g), col))

    def prev(col):
        return pl.BlockSpec((HALO, c), lambda *g: (jnp.maximum(index_of_step(*g) * hb - 1, 0), col))

    def nxt(col):
        return pl.BlockSpec(
            (HALO, c), lambda *g: (jnp.minimum((index_of_step(*g) + 1) * hb, nhb - 1), col))

    vec = pl.BlockSpec((1, c), lambda *g: (0, 0))
    in_specs = [rows(EVEN_A_LO), rows(EVEN_A_HI), prev(EVEN_A_LO), prev(EVEN_A_HI),
                nxt(EVEN_A_LO), nxt(EVEN_A_HI), rows(EVEN_A_GATE),
                pl.BlockSpec((CONV_W, c), lambda *g: (0, 0)), vec, vec, vec]
    out_spec = pl.BlockSpec((CONV_TS, c), lambda *g: (index_of_step(*g), 0))
    scratch = [pltpu.VMEM((SUBLANES, CONV_TS + 2 * HALO + SUBLANES, c), F32),
               pltpu.VMEM((CONV_W, SUBLANES, c), F32),
               pltpu.VMEM((CONV_TS, c), F32)]
    return in_specs, out_spec, scratch


def _bucket_thresholds():
    nb = N_BUCKETS // 2
    max_exact = nb // 2
    n = np.arange(1, 4 * MAX_DIST, dtype=np.float64)
    large = max_exact + (np.log(n / max_exact) / math.log(MAX_DIST / max_exact)
                         * (nb - max_exact)).astype(np.int64)
    large = np.minimum(large, nb - 1)
    thr = [int(n[np.argmax(large >= b)]) for b in range(max_exact + 1, nb)]
    return max_exact, thr


def _bias_kernel(rb_ref, qg_ref, kg_ref, o_ref, range_ref, *, tk, lo_diag, q_scale):
    h = pl.program_id(0)
    d = pl.program_id(1) + lo_diag
    gq = jnp.abs(qg_ref[0])
    gk = jnp.abs(kg_ref[0])
    for c in range(1, DH):
        gq = jnp.maximum(gq, jnp.abs(qg_ref[c]))
        gk = jnp.maximum(gk, jnp.abs(kg_ref[c]))
    qk_bound = gq * gk * (DH * q_scale * BF16_SLACK)
    bmax = rb_ref[0, h]
    bmin = rb_ref[0, h]
    for b in range(1, N_BUCKETS):
        bmax = jnp.maximum(bmax, rb_ref[b, h])
        bmin = jnp.minimum(bmin, rb_ref[b, h])
    shift = qk_bound + bmax * LOG2E
    range_ref[h] = 2.0 * qk_bound + (bmax - bmin) * LOG2E

    @pl.when(jnp.abs(d) < FAR)
    def _():
        row = lax.broadcasted_iota(jnp.int32, (tk, tk), 0)
        col = lax.broadcasted_iota(jnp.int32, (tk, tk), 1)
        rel = col - row + d * tk
        n = jnp.abs(rel)
        max_exact, thr = _bucket_thresholds()
        bucket = jnp.minimum(n, max_exact)
        for t in thr:
            bucket = bucket + jnp.where(n >= t, 1, 0)
        bucket = bucket + jnp.where(rel > 0, N_BUCKETS // 2, 0)
        val = jnp.zeros((tk, tk), F32)
        for b in range(N_BUCKETS):
            val = jnp.where(bucket == b, rb_ref[b, h], val)
        o_ref[0, 0] = val * LOG2E - shift

    @pl.when(d <= -FAR)
    def _():
        o_ref[0, 0] = jnp.full((tk, tk), rb_ref[N_BUCKETS // 2 - 1, h] * LOG2E - shift, F32)

    @pl.when(d >= FAR)
    def _():
        o_ref[0, 0] = jnp.full((tk, tk), rb_ref[N_BUCKETS - 1, h] * LOG2E - shift, F32)


EXP2_SAFE_RANGE = 100.0
BF16_SLACK = 1.01
N_DIAG = 5
FAR = N_DIAG // 2


def _bias_tiles(rel_bias, q_gain, k_gain, *, tk, q_scale):
    assert tk + 1 >= _bucket_thresholds()[1][-1]
    nd = N_DIAG
    smem = pl.BlockSpec(memory_space=pltpu.SMEM)
    kern = functools.partial(_bias_kernel, tk=tk, lo_diag=-FAR, q_scale=q_scale)
    return pl.pallas_call(
        kern,
        grid=(N_HEADS, nd),
        in_specs=[smem, smem, smem],
        out_specs=[pl.BlockSpec((1, 1, tk, tk), lambda h, d: (h, d, 0, 0)), smem],
        out_shape=[jax.ShapeDtypeStruct((N_HEADS, nd, tk, tk), F32),
                   jax.ShapeDtypeStruct((N_HEADS,), F32)],
        compiler_params=_cparams(("arbitrary", "arbitrary")),
        name="bias_tiles",
    )(rel_bias, q_gain, k_gain)


def _sub_head_norm(x, gain, scale):
    parts = []
    for t in range(x.shape[1] // DH):
        blk = x[:, t * DH:(t + 1) * DH]
        ms = jnp.mean(blk * blk, axis=-1, keepdims=True)
        parts.append(blk * lax.rsqrt(ms + EPS) * gain * scale)
    return jnp.concatenate(parts, axis=-1)


def _knorm_kernel(k_ref, g_ref, o_ref):
    o_ref[...] = _sub_head_norm(k_ref[...].astype(F32), g_ref[...], 1.0).astype(BF16)


def _knorm(p, k_gain, *, tr):
    n = p.shape[0]
    return pl.pallas_call(
        _knorm_kernel,
        grid=(n // tr,),
        in_specs=[pl.BlockSpec((tr, W_HALF), lambda i: (i, EVEN_K)),
                  pl.BlockSpec((1, DH), lambda i: (0, 0))],
        out_specs=pl.BlockSpec((tr, W_HALF), lambda i: (i, 0)),
        out_shape=jax.ShapeDtypeStruct((n, W_HALF), BF16),
        compiler_params=_cparams(("parallel",)),
        name="knorm",
    )(p, k_gain)


def _attn_kernel(range_ref, q_ref, k_ref, v_ref, bias_ref, gate_ref, qg_ref, lq1_ref, lk1_ref,
                 lq2_ref, lk2_ref, sg_ref, *rest, tq, tk, nkv, lam_init, q_scale, unroll,
                 conv_tiles_per_seq):
    conv_in = rest[:11]
    o_ref, conv_out = rest[11:13]
    qx_ref, acc_ref, ls_ref, l_ref = rest[13:17]
    cr = ConvRefs(*conv_in, conv_out, *rest[17:20])
    h = pl.program_id(1)
    i = pl.program_id(2)
    contract_last = (((1,), (1,)), ((), ()))

    step = (pl.program_id(0) * pl.num_programs(1) + h) * pl.num_programs(2) + i
    seq_pos = step % conv_tiles_per_seq
    _conv_prepare(cr, seq_pos == 0, seq_pos == conv_tiles_per_seq - 1)
    conv_chunks = CONV_TS // CONV_RC

    qx_ref[...] = _sub_head_norm(q_ref[...].astype(F32), qg_ref[...], q_scale).astype(BF16)
    acc_ref[...] = jnp.zeros_like(acc_ref)

    nsub = tq // tk

    def shifted_scores(j, t):
        rows = pl.ds(pl.multiple_of(j * tk, tk), tk)
        s = lax.dot_general(qx_ref[:, t * DH:(t + 1) * DH], k_ref[rows, t * DH:(t + 1) * DH],
                            contract_last, preferred_element_type=F32)
        parts = []
        for a in range(nsub):
            diag = jnp.clip(j - (i * nsub + a), -FAR, FAR) + FAR
            parts.append(s[a * tk:(a + 1) * tk] + bias_ref[0, diag])
        return jnp.concatenate(parts, axis=0)

    def bounded():
        ls_ref[...] = jnp.zeros_like(ls_ref)
        trips = nkv // unroll
        chunks_per_trip = conv_chunks // trips

        pieces_per_tile = chunks_per_trip * CONV_LANE_PARTS // unroll

        def body(jj, carry):
            for u in range(unroll):
                for piece in range(u * pieces_per_tile, (u + 1) * pieces_per_tile):
                    cc, part = divmod(piece, CONV_LANE_PARTS)
                    _conv_taps(cr, jj * chunks_per_trip + cc, [part])
                j = unroll * jj + u
                vv = v_ref[pl.ds(pl.multiple_of(j * tk, tk), tk), :]
                for t in range(2):
                    p = jnp.exp2(shifted_scores(j, t))
                    lane_sums = p[:, :DH]
                    for c in range(1, tk // DH):
                        lane_sums = lane_sums + p[:, c * DH:(c + 1) * DH]
                    ls_ref[t] = ls_ref[t] + lane_sums
                    acc_ref[t] = acc_ref[t] + jnp.dot(p.astype(BF16), vv,
                                                      preferred_element_type=F32)
            return carry

        lax.fori_loop(0, trips, body, 0)
        for t in range(2):
            l_ref[t] = jnp.sum(ls_ref[t], axis=-1, keepdims=True)

    def running_max():
        def taps(q, carry):
            _conv_taps(cr, q)
            return carry
        lax.fori_loop(0, conv_chunks, taps, 0)

        def body(j, ml):
            vv = v_ref[pl.ds(pl.multiple_of(j * tk, tk), tk), :]
            out = []
            for t in range(2):
                m, l = ml[2 * t], ml[2 * t + 1]
                s = shifted_scores(j, t)
                mn = jnp.maximum(m, jnp.max(s, axis=-1, keepdims=True))
                p = jnp.exp2(s - mn)
                a = jnp.exp2(m - mn)
                l = a * l + jnp.sum(p, axis=-1, keepdims=True)
                acc_ref[t] = a * acc_ref[t] + jnp.dot(p.astype(BF16), vv,
                                                      preferred_element_type=F32)
                out += [mn, l]
            return tuple(out)

        neg = jnp.full((tq, 1), -jnp.inf, F32)
        zero = jnp.zeros((tq, 1), F32)
        _, l1, _, l2 = lax.fori_loop(0, nkv, body, (neg, zero, neg, zero))
        l_ref[0] = l1
        l_ref[1] = l2

    lax.cond(range_ref[h] <= EXP2_SAFE_RANGE, bounded, running_max)
    _conv_finish(cr)

    lam =(jnp.exp(jnp.sum(lq1_ref[...] * lk1_ref[...], axis=-1, keepdims=True))
           - jnp.exp(jnp.sum(lq2_ref[...] * lk2_ref[...], axis=-1, keepdims=True)) + lam_init)
    o = acc_ref[0] * (1.0 / l_ref[0]) - lam * (acc_ref[1] * (1.0 / l_ref[1]))
    ms = jnp.mean(o * o, axis=-1, keepdims=True)
    y = o * lax.rsqrt(ms + EPS) * sg_ref[...] * (1.0 - lam_init)
    o_ref[...] = (y * _silu(gate_ref[...].astype(F32))).astype(BF16)


def _attention_and_conv(exp_range, p, kx, bias_tiles, q_gain, lq1, lk1, lq2, lk2, subln_g, conv_w,
                        conv_b, conv_ln_g, conv_ln_b, *, batch, seq, tq, tk, lam_init, q_scale):
    n = p.shape[0]
    nq = seq // tq
    nd = bias_tiles.shape[1]
    nkv = seq // tk
    unroll = 4
    trips = nkv // unroll
    conv_chunks = CONV_TS // CONV_RC
    assert tq % tk == 0 and bias_tiles.shape[2:] == (tk, tk) and nkv % unroll == 0
    assert batch * N_HEADS * nq * CONV_TS == n and conv_chunks % trips == 0
    assert seq % CONV_TS == 0 and (CONV_TS + 2 * HALO) % CONV_RC == 0
    per_head = W_HALF // DV

    def q_rows(seg):
        return pl.BlockSpec((tq, DV), lambda b, h, i: (b * nq + i, seg * per_head + h))

    vec = pl.BlockSpec((1, DH), lambda b, h, i: (0, 0))
    conv_in, conv_out, conv_scratch = _conv_specs(lambda b, h, i: (b * N_HEADS + h) * nq + i, n)
    kern = functools.partial(_attn_kernel, tq=tq, tk=tk, nkv=nkv, lam_init=lam_init,
                             q_scale=q_scale, unroll=unroll, conv_tiles_per_seq=seq // CONV_TS)
    return pl.pallas_call(
        kern,
        grid=(batch, N_HEADS, nq),
        in_specs=[
            pl.BlockSpec(memory_space=pltpu.SMEM),
            q_rows(EVEN_Q),
            pl.BlockSpec((seq, DV), lambda b, h, i: (b, h)),
            pl.BlockSpec((seq, DV), lambda b, h, i: (b, EVEN_V * per_head + h)),
            pl.BlockSpec((1, nd, tk, tk), lambda b, h, i: (h, 0, 0, 0),
                         pipeline_mode=pl.Buffered(1)),
            q_rows(EVEN_B_GATE),
            vec, vec, vec, vec, vec,
            pl.BlockSpec((1, DV), lambda b, h, i: (0, 0)),
        ] + conv_in,
        out_specs=[pl.BlockSpec((tq, DV), lambda b, h, i: (b * nq + i, h)), conv_out],
        out_shape=[jax.ShapeDtypeStruct((n, N_HEADS * DV), BF16),
                   jax.ShapeDtypeStruct((n, W_HALF), BF16)],
        scratch_shapes=[
            pltpu.VMEM((tq, DV), BF16),
            pltpu.VMEM((2, tq, DV), F32),
            pltpu.VMEM((2, tq, DH), F32),
            pltpu.VMEM((2, tq, 1), F32),
        ] + conv_scratch,
        compiler_params=_cparams(("parallel", "parallel", "parallel")),
        name="diff_attention",
    )(exp_range, p, kx, p, bias_tiles, p, q_gain, lq1, lk1, lq2, lk2, subln_g,
      p, p, p, p, p, p, p, conv_w, conv_b, conv_ln_g, conv_ln_b)


ROWS_BF16 = 16


def _fnet_constants(seq, ch):
    r = seq // FFT_INNER
    two_pi = 2.0 * np.pi
    ang_r = two_pi * np.outer(np.arange(r), np.arange(r)) / r
    base = np.stack([np.cos(ang_r), -np.sin(ang_r)], axis=1).reshape(2 * r, r)
    w1 = np.kron(base, np.eye(ROWS_BF16))
    ang_tw = two_pi * np.outer(np.arange(FFT_INNER), np.arange(r)) / seq
    tw = np.stack([np.cos(ang_tw), np.sin(ang_tw)])
    tw = tw.reshape(2, FFT_INNER, r // ROWS_BF16, ROWS_BF16).transpose(2, 0, 1, 3)
    ang_i = two_pi * np.outer(np.arange(FFT_INNER), np.arange(FFT_INNER)) / FFT_INNER
    c, s = np.cos(ang_i), np.sin(ang_i)
    w2 = np.block([[c, s], [-s, c]])
    scale = 1.0 / math.sqrt(seq * GROUP_C)
    eye_g = np.eye(ch // GROUP_C)
    bdc = np.kron(eye_g, c * scale)
    bds = np.kron(eye_g, s * scale)
    perm = np.zeros((ROWS_BF16 * ROWS_BF16,) * 2)
    for b in range(ROWS_BF16):
        for f in range(ROWS_BF16):
            perm[b * ROWS_BF16 + f, f * ROWS_BF16 + b] = 1.0
    as_bf16 = lambda a: jnp.asarray(a, dtype=BF16)
    return (as_bf16(w1), jnp.asarray(tw, dtype=F32), as_bf16(w2), as_bf16(bdc), as_bf16(bds),
            as_bf16(perm))


def _fft1_kernel(x_ref, w_ref, o_ref):
    _, r, rows, c = x_ref.shape
    x = x_ref[0].reshape(r * rows, c)
    y = jnp.dot(w_ref[...], x, preferred_element_type=F32).astype(BF16)
    o_ref[0] = y.reshape(r, 2, rows, c)


def _fft2_kernel(a_ref, tw_ref, w2_ref, bdc_ref, bds_ref, perm_ref, gate_ref, o_ref, u_ref, y_ref):
    nf = a_ref.shape[1]
    ch = a_ref.shape[-1]
    for kk in range(nf):
        br = a_ref[0, kk, 0].astype(F32)
        bi = a_ref[0, kk, 1].astype(F32)
        cw = tw_ref[0, 0][:, kk:kk + 1]
        sw = tw_ref[0, 1][:, kk:kk + 1]
        x = jnp.concatenate([br * cw + bi * sw, bi * cw - br * sw], axis=0).astype(BF16)
        u = jnp.dot(w2_ref[...], x, preferred_element_type=F32).astype(BF16)
        u_ref[0, kk * FFT_INNER:(kk + 1) * FFT_INNER, :] = u[:FFT_INNER]
        u_ref[1, kk * FFT_INNER:(kk + 1) * FFT_INNER, :] = u[FFT_INNER:]
    y_ref[...] = (jnp.dot(u_ref[0], bdc_ref[...], preferred_element_type=F32)
                  + jnp.dot(u_ref[1], bds_ref[...], preferred_element_type=F32)).astype(BF16)
    for a in range(FFT_INNER // ROWS_BF16):
        lo = a * ROWS_BF16
        piece = jnp.concatenate(
            [y_ref[k * FFT_INNER + lo:k * FFT_INNER + lo + ROWS_BF16, :] for k in range(nf)], axis=0)
        z = jnp.dot(perm_ref[...], piece, preferred_element_type=F32)
        g = gate_ref[0, lo:lo + ROWS_BF16].reshape(ROWS_BF16 * nf, ch).astype(F32)
        o_ref[0, lo:lo + ROWS_BF16] = (z * _silu(g)).astype(BF16).reshape(ROWS_BF16, nf, ch)


def _fnet(p, *, batch, seq, ch=512):
    c = W_HALF
    r = seq // FFT_INNER
    nf = ROWS_BF16
    assert r % nf == 0 and c % ch == 0
    w1, tw, w2, bdc, bds, perm = _fnet_constants(seq, ch)
    const = pl.Buffered(1)
    stage1 = pl.pallas_call(
        _fft1_kernel,
        grid=(batch, FFT_INNER // nf),
        in_specs=[
            pl.BlockSpec((1, r, nf, c), lambda b, t: (b, 0, t, ODD_C_IN)),
            pl.BlockSpec((2 * r * nf, r * nf), lambda b, t: (0, 0), pipeline_mode=const),
        ],
        out_specs=pl.BlockSpec((1, r, 2, nf, c), lambda b, t: (b, 0, 0, t, 0)),
        out_shape=jax.ShapeDtypeStruct((batch, r, 2, FFT_INNER, c), BF16),
        compiler_params=_cparams(("parallel", "parallel")),
        name="fft_stage1",
    )(p.reshape(batch, r, FFT_INNER, p.shape[1]), w1)
    gate_cols = ODD_C_GATE * (c // ch)
    cmat = lambda shape: pl.BlockSpec(shape, lambda b, f, j: (0,) * len(shape), pipeline_mode=const)
    out = pl.pallas_call(
        _fft2_kernel,
        grid=(batch, r // nf, c // ch),
        in_specs=[
            pl.BlockSpec((1, nf, 2, FFT_INNER, ch), lambda b, f, j: (b, f, 0, 0, j)),
            pl.BlockSpec((1, 2, FFT_INNER, nf), lambda b, f, j: (f, 0, 0, 0)),
            cmat((2 * FFT_INNER, 2 * FFT_INNER)), cmat((ch, ch)), cmat((ch, ch)),
            cmat((nf * nf, nf * nf)),
            pl.BlockSpec((1, FFT_INNER, nf, ch), lambda b, f, j: (b, 0, f, gate_cols + j)),
        ],
        out_specs=pl.BlockSpec((1, FFT_INNER, nf, ch), lambda b, f, j: (b, 0, f, j)),
        out_shape=jax.ShapeDtypeStruct((batch, FFT_INNER, r, c), BF16),
        scratch_shapes=[pltpu.VMEM((2, nf * FFT_INNER, ch), BF16),
                        pltpu.VMEM((nf * FFT_INNER, ch), BF16)],
        compiler_params=_cparams(("parallel", "parallel", "parallel")),
        name="fft_stage2",
    )(stage1, tw, w2, bdc, bds, perm, p.reshape(batch, FFT_INNER, r, p.shape[1]))
    return out.reshape(batch * seq, c)


def _sgu_kernel(u_ref, v_ref, gate_ref, lg_ref, lb_ref, ws_ref, bt_ref, o_ref, *, tr):
    v = v_ref[...].astype(F32)
    mu = jnp.mean(v, axis=-1, keepdims=True)
    vc = v - mu
    var = jnp.mean(vc * vc, axis=-1, keepdims=True)
    vn = (vc * lax.rsqrt(var + EPS) * lg_ref[...] + lb_ref[...]).astype(BF16)
    for g in range(SGU_GROUPS):
        cols = slice(g * SGU_DG, (g + 1) * SGU_DG)
        bcol = jnp.broadcast_to(bt_ref[:, g:g + 1], (SGU_CHUNK, SGU_DG))
        for n in range(tr // SGU_CHUNK):
            rows = slice(n * SGU_CHUNK, (n + 1) * SGU_CHUNK)
            sv = jnp.dot(ws_ref[g], vn[rows, cols], preferred_element_type=F32) + bcol
            out = u_ref[rows, cols].astype(F32) * sv * _silu(gate_ref[rows, cols].astype(F32))
            o_ref[rows, cols] = out.astype(BF16)


def _sgu_inputs(p, ln_g, ln_b, ws_bf16, b_t, *, tr):
    c = W_HALF

    def seg(s):
        return pl.BlockSpec((tr, c), lambda i: (i, s))

    vec = pl.BlockSpec((1, c), lambda i: (0, 0))
    specs = [seg(ODD_U), seg(ODD_V), seg(ODD_D_GATE), vec, vec,
             pl.BlockSpec((SGU_GROUPS, SGU_CHUNK, SGU_CHUNK), lambda i: (0, 0, 0)),
             pl.BlockSpec((SGU_CHUNK, SGU_GROUPS), lambda i: (0, 0))]
    return [p, p, p, ln_g, ln_b, ws_bf16, b_t], specs


QK_SCALE_LOG2 = LOG2E / math.sqrt(DH)


def _attention_bias(p, *, tk):
    return _bias_tiles(p["rel_bias"], p["q_norm_g"], p["k_norm_g"], tk=tk, q_scale=QK_SCALE_LOG2)


def _trunk(x, p, bias_tiles, exp_range, *, tq, tk):
    batch, seq, d = x.shape
    x2d = x.reshape(batch * seq, d)
    row = lambda a: a.reshape(1, -1)

    lam_init = 0.8 - 0.6 * math.exp(-0.3 * 0)
    pe = _inproj(x2d, row(p["norm_g"][0]), p["w_in_even"], tm=1024, tn=1792)
    kx = _knorm(pe, row(p["k_norm_g"]), tr=1024)
    mix_b, mix_a = _attention_and_conv(
        exp_range, pe, kx, bias_tiles, row(p["q_norm_g"]), row(p["lam_q1"]), row(p["lam_k1"]),
        row(p["lam_q2"]), row(p["lam_k2"]), row(p["subln_g"]), p["conv_w"], row(p["conv_b"]),
        row(p["conv_ln_g"]), row(p["conv_ln_b"]), batch=batch, seq=seq, tq=tq, tk=tk,
        lam_init=lam_init, q_scale=QK_SCALE_LOG2)
    x1 = _outproj(mix_a, mix_b, p["w_out_even"], x2d, tm=512)

    po = _inproj(x1, row(p["norm_g"][1]), p["w_in_odd"], tm=1024, tn=1280)
    mix_c = _fnet(po, batch=batch, seq=seq)
    sgu = _sgu_inputs(po, row(p["sgu_ln_g"]), row(p["sgu_ln_b"]), p["sgu_w"], p["sgu_b"].T, tr=512)
    y = _outproj(mix_c, None, p["w_out_odd"], x1, tm=512, sgu=sgu)
    return y.reshape(batch, seq, d)


def kernel(x_prompt, x_sample, norm_g, w_in_even, conv_w, conv_b, conv_ln_g, conv_ln_b,
           q_norm_g, k_norm_g, lam_q1, lam_k1, lam_q2, lam_k2, subln_g, rel_bias, w_out_even,
           w_in_odd, sgu_ln_g, sgu_ln_b, sgu_w, sgu_b, w_out_odd):
    p = dict(
        norm_g=norm_g, w_in_even=w_in_even[0].astype(BF16), conv_w=conv_w[0], conv_b=conv_b[0],
        conv_ln_g=conv_ln_g[0], conv_ln_b=conv_ln_b[0], q_norm_g=q_norm_g[0],
        k_norm_g=k_norm_g[0], lam_q1=lam_q1[0], lam_k1=lam_k1[0], lam_q2=lam_q2[0],
        lam_k2=lam_k2[0], subln_g=subln_g[0], w_out_even=w_out_even[0].astype(BF16),
        w_in_odd=w_in_odd[0].astype(BF16), sgu_ln_g=sgu_ln_g[0], sgu_ln_b=sgu_ln_b[0],
        sgu_w=sgu_w[0].astype(BF16), sgu_b=sgu_b[0], w_out_odd=w_out_odd[0].astype(BF16),
        rel_bias=rel_bias)
    tq, tk = 1024, 512
    bias_tiles, exp_range = _attention_bias(p, tk=tk)
    y_prompt = _trunk(x_prompt, p, bias_tiles, exp_range, tq=tq, tk=tk)
    y_sample = _trunk(x_sample, p, bias_tiles, exp_range, tq=tq, tk=tk)
    return (y_prompt, y_sample)
```

```python
import functools
import math

import numpy as np
import jax
import jax.numpy as jnp
from jax import lax
from jax.experimental import pallas as pl
from jax.experimental.pallas import tpu as pltpu

F32 = jnp.float32
BF16 = jnp.bfloat16

EPS = 1e-6
LOG2E = math.log2(math.e)

D_MODEL = 2048
W_HALF = D_MODEL // 2
DH = 128
N_HEADS = 4
DV = 2 * DH
CONV_W = 31
CONV_PAD = CONV_W // 2
N_BUCKETS = 32
MAX_DIST = 128
FFT_INNER = 128
GROUP_C = 128
SGU_GROUPS = 4
SGU_CHUNK = 128
SGU_DG = W_HALF // SGU_GROUPS

EVEN_A_LO, EVEN_A_HI, EVEN_A_GATE, EVEN_Q, EVEN_K, EVEN_V, EVEN_B_GATE = range(7)
ODD_C_IN, ODD_C_GATE, ODD_U, ODD_V, ODD_D_GATE = range(5)

VMEM_LIMIT_V7X = 56 * 1024 * 1024
HALO = 16
SUBLANES = 8


def _cparams(sem):
    return pltpu.CompilerParams(dimension_semantics=sem, vmem_limit_bytes=VMEM_LIMIT_V7X)


def _sigmoid(x):
    return 0.5 * jnp.tanh(0.5 * x) + 0.5


def _silu(x):
    return x * _sigmoid(x)


def _inproj_kernel(x_ref, g_ref, w_ref, o_ref, h_ref):
    @pl.when(pl.program_id(1) == 0)
    def _():
        x = x_ref[...]
        ms = jnp.mean(x * x, axis=-1, keepdims=True)
        h_ref[...] = (x * lax.rsqrt(ms + EPS) * g_ref[...]).astype(BF16)

    o_ref[...] = jnp.dot(h_ref[...], w_ref[...], preferred_element_type=F32).astype(BF16)


def _inproj(x2d, g, w_bf16, *, tm, tn):
    n, d = x2d.shape
    p = w_bf16.shape[1]
    return pl.pallas_call(
        _inproj_kernel,
        grid=(n // tm, p // tn),
        in_specs=[
            pl.BlockSpec((tm, d), lambda i, j: (i, 0)),
            pl.BlockSpec((1, d), lambda i, j: (0, 0)),
            pl.BlockSpec((d, tn), lambda i, j: (0, j)),
        ],
        out_specs=pl.BlockSpec((tm, tn), lambda i, j: (i, j)),
        out_shape=jax.ShapeDtypeStruct((n, p), BF16),
        scratch_shapes=[pltpu.VMEM((tm, d), BF16)],
        compiler_params=_cparams(("parallel", "arbitrary")),
        name="inproj",
    )(x2d, g, w_bf16)


def _outproj_kernel(ma_ref, *rest, n_sgu_inputs):
    if n_sgu_inputs:
        sgu_in = rest[:n_sgu_inputs]
        wa_ref, wb_ref, x_ref, o_ref, mb_ref = rest[n_sgu_inputs:]
    else:
        mb_ref, wa_ref, wb_ref, x_ref, o_ref = rest
    acc = jnp.dot(ma_ref[...], wa_ref[...], preferred_element_type=F32)
    if n_sgu_inputs:
        _sgu_kernel(*sgu_in, mb_ref, tr=mb_ref.shape[0])
    acc = acc + jnp.dot(mb_ref[...], wb_ref[...], preferred_element_type=F32)
    o_ref[...] = x_ref[...] + acc


def _outproj(mix_a, mix_b, w_bf16, x2d, *, tm, sgu=None):
    n, d = x2d.shape
    half = mix_a.shape[1]
    const = pl.Buffered(1)
    rows = pl.BlockSpec((tm, half), lambda i: (i, 0))
    if sgu is None:
        second, second_specs, scratch = [mix_b], [rows], []
    else:
        second, second_specs = sgu
        scratch = [pltpu.VMEM((tm, half), BF16)]
    return pl.pallas_call(
        functools.partial(_outproj_kernel, n_sgu_inputs=0 if sgu is None else len(second)),
        grid=(n // tm,),
        in_specs=[rows] + second_specs + [
            pl.BlockSpec((half, d), lambda i: (0, 0), pipeline_mode=const),
            pl.BlockSpec((half, d), lambda i: (1, 0), pipeline_mode=const),
            pl.BlockSpec((tm, d), lambda i: (i, 0)),
        ],
        out_specs=pl.BlockSpec((tm, d), lambda i: (i, 0)),
        out_shape=jax.ShapeDtypeStruct((n, d), F32),
        scratch_shapes=scratch,
        compiler_params=_cparams(("parallel",)),
        name="outproj",
    )(mix_a, *second, w_bf16, w_bf16, x2d)


CONV_TS = 256
CONV_RC = 32
CONV_LANE_PARTS = 2


class ConvRefs:
    def __init__(self, lo, hi, plo, phi, nlo, nhi, gate, w, b, lg, lb, out, xs, wb, y):
        self.lo, self.hi, self.plo, self.phi, self.nlo, self.nhi = lo, hi, plo, phi, nlo, nhi
        self.gate, self.w, self.b, self.lg, self.lb, self.out = gate, w, b, lg, lb, out
        self.xs, self.wb, self.y = xs, wb, y


def _conv_glu(cr, first, last):
    ts, c = cr.y.shape
    padded = ts + 2 * HALO

    def glu(a, b):
        return a.astype(F32) * _sigmoid(b.astype(F32))

    cr.xs[0, HALO:HALO + ts, :] = glu(cr.lo[...], cr.hi[...])
    cr.xs[0, 0:HALO, :] = jnp.where(first, 0.0, glu(cr.plo[...], cr.phi[...]))
    cr.xs[0, HALO + ts:padded, :] = jnp.where(last, 0.0, glu(cr.nlo[...], cr.nhi[...]))
    cr.xs[0, padded:padded + SUBLANES, :] = jnp.zeros((SUBLANES, c), F32)
    for k in range(CONV_W):
        cr.wb[k] = jnp.broadcast_to(cr.w[k:k + 1, :], (SUBLANES, c))


def _conv_shift(cr, q):
    c = cr.y.shape[1]
    lw = c // CONV_LANE_PARTS
    row = pl.multiple_of(q * CONV_RC, CONV_RC)
    for col in range(0, c, lw):
        win = cr.xs[0, pl.ds(row, CONV_RC + SUBLANES), col:col + lw]
        for r in range(1, SUBLANES):
            cr.xs[r, pl.ds(row, CONV_RC), col:col + lw] = pltpu.roll(
                win, CONV_RC + SUBLANES - r, axis=0)[:CONV_RC]


CONV_GROUPS = CONV_RC // SUBLANES
CONV_GROUP_PARTS = 2
CONV_PIECES = CONV_LANE_PARTS * CONV_GROUP_PARTS


def _zero_token(x):
    bits = pltpu.bitcast(x[:SUBLANES, :DH], jnp.uint32)
    return pltpu.bitcast(lax.shift_right_logical(bits, jnp.uint32(32)), F32)


def _conv_tap_piece(cr, q, piece, token=None):
    c = cr.y.shape[1]
    lw = c // CONV_LANE_PARTS
    base0 = HALO - CONV_PAD
    part, gpart = divmod(piece, CONV_GROUP_PARTS)
    per = CONV_GROUPS // CONV_GROUP_PARTS
    groups = range(gpart * per, (gpart + 1) * per)
    col = part * lw
    row = pl.multiple_of(q * CONV_RC, CONV_RC)
    start = jnp.zeros((SUBLANES, lw), F32) if token is None else jnp.concatenate(
        [token] * (lw // DH), axis=-1)
    acc = {g: start for g in groups}
    for r in range(SUBLANES):
        taps = [(a, SUBLANES * a + r - base0) for a in range((base0 + CONV_W - 1) // SUBLANES + 1)
                if 0 <= SUBLANES * a + r - base0 < CONV_W]
        x = {s: cr.xs[r, pl.ds(row + SUBLANES * s, SUBLANES), col:col + lw]
             for s in sorted({a + g for a, _ in taps for g in groups})}
        for a, k in taps:
            wk = cr.wb[k, :, col:col + lw]
            for g in groups:
                acc[g] = acc[g] + x[a + g] * wk
    bias = cr.b[:, col:col + lw]
    for g in groups:
        cr.y[pl.ds(row + SUBLANES * g, SUBLANES), col:col + lw] = acc[g] + bias


def _conv_taps(cr, q):
    for piece in range(CONV_PIECES):
        _conv_tap_piece(cr, q, piece)


def _conv_norm(cr, q):
    row = pl.multiple_of(q * CONV_RC, CONV_RC)
    y = cr.y[pl.ds(row, CONV_RC), :]
    mu = jnp.mean(y, axis=-1, keepdims=True)
    yc = y - mu
    var = jnp.mean(yc * yc, axis=-1, keepdims=True)
    z = yc * lax.rsqrt(var + EPS) * cr.lg[...] + cr.lb[...]
    out = _silu(z) * _silu(cr.gate[pl.ds(row, CONV_RC), :].astype(F32))
    cr.out[pl.ds(row, CONV_RC), :] = out.astype(BF16)


def _conv_phase(cr, chunk_fn, n_chunks, unroll=1):
    def body(q, carry):
        chunk_fn(cr, q)
        return carry
    lax.fori_loop(0, n_chunks, body, 0, unroll=unroll)


def _conv_specs(index_of_step, n_rows):
    c = W_HALF
    hb = CONV_TS // HALO
    nhb = n_rows // HALO

    def rows(col):
        return pl.BlockSpec((CONV_TS, c), lambda *g: (index_of_step(*g), col))

    def prev(col):
        return pl.BlockSpec((HALO, c), lambda *g: (jnp.maximum(index_of_step(*g) * hb - 1, 0), col))

    def nxt(col):
        return pl.BlockSpec(
            (HALO, c), lambda *g: (jnp.minimum((index_of_step(*g) + 1) * hb, nhb - 1), col))

    vec = pl.BlockSpec((1, c), lambda *g: (0, 0))
    in_specs = [rows(EVEN_A_LO), rows(EVEN_A_HI), prev(EVEN_A_LO), prev(EVEN_A_HI),
                nxt(EVEN_A_LO), nxt(EVEN_A_HI), rows(EVEN_A_GATE),
                pl.BlockSpec((CONV_W, c), lambda *g: (0, 0)), vec, vec, vec]
    out_spec = pl.BlockSpec((CONV_TS, c), lambda *g: (index_of_step(*g), 0))
    scratch = [pltpu.VMEM((SUBLANES, CONV_TS + 2 * HALO + SUBLANES, c), F32),
               pltpu.VMEM((CONV_W, SUBLANES, c), F32),
               pltpu.VMEM((CONV_TS, c), F32)]
    return in_specs, out_spec, scratch


def _bucket_thresholds():
    nb = N_BUCKETS // 2
    max_exact = nb // 2
    n = np.arange(1, 4 * MAX_DIST, dtype=np.float64)
    large = max_exact + (np.log(n / max_exact) / math.log(MAX_DIST / max_exact)
                         * (nb - max_exact)).astype(np.int64)
    large = np.minimum(large, nb - 1)
    thr = [int(n[np.argmax(large >= b)]) for b in range(max_exact + 1, nb)]
    return max_exact, thr


def _bias_kernel(rb_ref, qg_ref, kg_ref, o_ref, range_ref, *, tk, lo_diag, q_scale):
    h = pl.program_id(0)
    d = pl.program_id(1) + lo_diag
    gq = jnp.abs(qg_ref[0])
    gk = jnp.abs(kg_ref[0])
    for c in range(1, DH):
        gq = jnp.maximum(gq, jnp.abs(qg_ref[c]))
        gk = jnp.maximum(gk, jnp.abs(kg_ref[c]))
    qk_bound = gq * gk * (DH * q_scale * BF16_SLACK)
    bmax = rb_ref[0, h]
    bmin = rb_ref[0, h]
    for b in range(1, N_BUCKETS):
        bmax = jnp.maximum(bmax, rb_ref[b, h])
        bmin = jnp.minimum(bmin, rb_ref[b, h])
    shift = qk_bound + bmax * LOG2E
    range_ref[h] = 2.0 * qk_bound + (bmax - bmin) * LOG2E

    @pl.when(jnp.abs(d) < FAR)
    def _():
        row = lax.broadcasted_iota(jnp.int32, (tk, tk), 0)
        col = lax.broadcasted_iota(jnp.int32, (tk, tk), 1)
        rel = col - row + d * tk
        n = jnp.abs(rel)
        max_exact, thr = _bucket_thresholds()
        bucket = jnp.minimum(n, max_exact)
        for t in thr:
            bucket = bucket + jnp.where(n >= t, 1, 0)
        bucket = bucket + jnp.where(rel > 0, N_BUCKETS // 2, 0)
        val = jnp.zeros((tk, tk), F32)
        for b in range(N_BUCKETS):
            val = jnp.where(bucket == b, rb_ref[b, h], val)
        o_ref[0, 0] = val * LOG2E - shift

    @pl.when(d <= -FAR)
    def _():
        o_ref[0, 0] = jnp.full((tk, tk), rb_ref[N_BUCKETS // 2 - 1, h] * LOG2E - shift, F32)

    @pl.when(d >= FAR)
    def _():
        o_ref[0, 0] = jnp.full((tk, tk), rb_ref[N_BUCKETS - 1, h] * LOG2E - shift, F32)


EXP2_SAFE_RANGE = 100.0
BF16_SLACK = 1.01
N_DIAG = 5
FAR = N_DIAG // 2


def _bias_tiles(rel_bias, q_gain, k_gain, *, tk, q_scale):
    assert tk + 1 >= _bucket_thresholds()[1][-1]
    nd = N_DIAG
    smem = pl.BlockSpec(memory_space=pltpu.SMEM)
    kern = functools.partial(_bias_kernel, tk=tk, lo_diag=-FAR, q_scale=q_scale)
    return pl.pallas_call(
        kern,
        grid=(N_HEADS, nd),
        in_specs=[smem, smem, smem],
        out_specs=[pl.BlockSpec((1, 1, tk, tk), lambda h, d: (h, d, 0, 0)), smem],
        out_shape=[jax.ShapeDtypeStruct((N_HEADS, nd, tk, tk), F32),
                   jax.ShapeDtypeStruct((N_HEADS,), F32)],
        compiler_params=_cparams(("arbitrary", "arbitrary")),
        name="bias_tiles",
    )(rel_bias, q_gain, k_gain)


def _sub_head_norm(x, gain, scale):
    parts = []
    for t in range(x.shape[1] // DH):
        blk = x[:, t * DH:(t + 1) * DH]
        ms = jnp.mean(blk * blk, axis=-1, keepdims=True)
        parts.append(blk * lax.rsqrt(ms + EPS) * gain * scale)
    return jnp.concatenate(parts, axis=-1)


def _knorm_kernel(k_ref, g_ref, o_ref):
    o_ref[...] = _sub_head_norm(k_ref[...].astype(F32), g_ref[...], 1.0).astype(BF16)


def _knorm(p, k_gain, *, tr):
    n = p.shape[0]
    return pl.pallas_call(
        _knorm_kernel,
        grid=(n // tr,),
        in_specs=[pl.BlockSpec((tr, W_HALF), lambda i: (i, EVEN_K)),
                  pl.BlockSpec((1, DH), lambda i: (0, 0))],
        out_specs=pl.BlockSpec((tr, W_HALF), lambda i: (i, 0)),
        out_shape=jax.ShapeDtypeStruct((n, W_HALF), BF16),
        compiler_params=_cparams(("parallel",)),
        name="knorm",
    )(p, k_gain)


def _attn_kernel(range_ref, q_ref, k_ref, v_ref, bias_ref, gate_ref, qg_ref, lq1_ref, lk1_ref,
                 lq2_ref, lk2_ref, sg_ref, *rest, tq, tk, nkv, lam_init, q_scale, unroll,
                 conv_tiles_per_seq):
    conv_in = rest[:11]
    o_ref, conv_out = rest[11:13]
    qx_ref, acc_ref, ls_ref, l_ref = rest[13:17]
    cr = ConvRefs(*conv_in, conv_out, *rest[17:20])
    h = pl.program_id(1)
    i = pl.program_id(2)
    contract_last = (((1,), (1,)), ((), ()))

    step = (pl.program_id(0) * pl.num_programs(1) + h) * pl.num_programs(2) + i
    seq_pos = step % conv_tiles_per_seq
    _conv_glu(cr, seq_pos == 0, seq_pos == conv_tiles_per_seq - 1)
    conv_chunks = CONV_TS // CONV_RC

    qx_ref[...] = _sub_head_norm(q_ref[...].astype(F32), qg_ref[...], q_scale).astype(BF16)
    acc_ref[...] = jnp.zeros_like(acc_ref)

    nsub = tq // tk

    def shifted_scores(j, t):
        rows = pl.ds(pl.multiple_of(j * tk, tk), tk)
        s = lax.dot_general(qx_ref[:, t * DH:(t + 1) * DH], k_ref[rows, t * DH:(t + 1) * DH],
                            contract_last, preferred_element_type=F32)
        parts = []
        for a in range(nsub):
            diag = jnp.clip(j - (i * nsub + a), -FAR, FAR) + FAR
            parts.append(s[a * tk:(a + 1) * tk] + bias_ref[0, diag])
        return jnp.concatenate(parts, axis=0)

    def bounded():
        ls_ref[...] = jnp.zeros_like(ls_ref)
        trips = nkv // unroll
        chunks_per_trip = conv_chunks // trips
        _conv_phase(cr, _conv_shift, conv_chunks + 1)
        pieces = [(cc, piece) for cc in range(chunks_per_trip) for piece in range(CONV_PIECES)]
        per_softmax = len(pieces) // (2 * unroll)
        assert per_softmax * 2 * unroll == len(pieces)

        def body(jj, carry):
            for u in range(unroll):
                j = unroll * jj + u
                vv = v_ref[pl.ds(pl.multiple_of(j * tk, tk), tk), :]
                for t in range(2):
                    p = jnp.exp2(shifted_scores(j, t))
                    lane_sums = p[:, :DH]
                    for c in range(1, tk // DH):
                        lane_sums = lane_sums + p[:, c * DH:(c + 1) * DH]
                    at = (2 * u + t) * per_softmax
                    for n, (cc, piece) in enumerate(pieces[at:at + per_softmax]):
                        r0 = n * (tq // per_softmax)
                        _conv_tap_piece(cr, jj * chunks_per_trip + cc, piece,
                                        _zero_token(lane_sums[r0:r0 + SUBLANES]))
                    ls_ref[t] = ls_ref[t] + lane_sums
                    acc_ref[t] = acc_ref[t] + jnp.dot(p.astype(BF16), vv,
                                                      preferred_element_type=F32)
            return carry

        lax.fori_loop(0, trips, body, 0)
        _conv_phase(cr, _conv_norm, conv_chunks, unroll=2)
        for t in range(2):
            l_ref[t] = jnp.sum(ls_ref[t], axis=-1, keepdims=True)

    def running_max():
        _conv_phase(cr, _conv_shift, conv_chunks + 1)
        _conv_phase(cr, _conv_taps, conv_chunks)
        _conv_phase(cr, _conv_norm, conv_chunks, unroll=2)

        def body(j, ml):
            vv = v_ref[pl.ds(pl.multiple_of(j * tk, tk), tk), :]
            out = []
            for t in range(2):
                m, l = ml[2 * t], ml[2 * t + 1]
                s = shifted_scores(j, t)
                mn = jnp.maximum(m, jnp.max(s, axis=-1, keepdims=True))
                p = jnp.exp2(s - mn)
                a = jnp.exp2(m - mn)
                l = a * l + jnp.sum(p, axis=-1, keepdims=True)
                acc_ref[t] = a * acc_ref[t] + jnp.dot(p.astype(BF16), vv,
                                                      preferred_element_type=F32)
                out += [mn, l]
            return tuple(out)

        neg = jnp.full((tq, 1), -jnp.inf, F32)
        zero = jnp.zeros((tq, 1), F32)
        _, l1, _, l2 = lax.fori_loop(0, nkv, body, (neg, zero, neg, zero))
        l_ref[0] = l1
        l_ref[1] = l2

    lax.cond(range_ref[h] <= EXP2_SAFE_RANGE, bounded, running_max)

    lam =(jnp.exp(jnp.sum(lq1_ref[...] * lk1_ref[...], axis=-1, keepdims=True))
           - jnp.exp(jnp.sum(lq2_ref[...] * lk2_ref[...], axis=-1, keepdims=True)) + lam_init)
    o = acc_ref[0] * (1.0 / l_ref[0]) - lam * (acc_ref[1] * (1.0 / l_ref[1]))
    ms = jnp.mean(o * o, axis=-1, keepdims=True)
    y = o * lax.rsqrt(ms + EPS) * sg_ref[...] * (1.0 - lam_init)
    o_ref[...] = (y * _silu(gate_ref[...].astype(F32))).astype(BF16)


def _attention_and_conv(exp_range, p, kx, bias_tiles, q_gain, lq1, lk1, lq2, lk2, subln_g, conv_w,
                        conv_b, conv_ln_g, conv_ln_b, *, batch, seq, tq, tk, lam_init, q_scale):
    n = p.shape[0]
    nq = seq // tq
    nd = bias_tiles.shape[1]
    nkv = seq // tk
    unroll = 4
    trips = nkv // unroll
    conv_chunks = CONV_TS // CONV_RC
    assert tq % tk == 0 and bias_tiles.shape[2:] == (tk, tk) and nkv % unroll == 0
    assert batch * N_HEADS * nq * CONV_TS == n and conv_chunks % trips == 0
    assert seq % CONV_TS == 0 and (CONV_TS + 2 * HALO) % CONV_RC == 0
    per_head = W_HALF // DV

    def q_rows(seg):
        return pl.BlockSpec((tq, DV), lambda b, h, i: (b * nq + i, seg * per_head + h))

    vec = pl.BlockSpec((1, DH), lambda b, h, i: (0, 0))
    conv_in, conv_out, conv_scratch = _conv_specs(lambda b, h, i: (b * N_HEADS + h) * nq + i, n)
    kern = functools.partial(_attn_kernel, tq=tq, tk=tk, nkv=nkv, lam_init=lam_init,
                             q_scale=q_scale, unroll=unroll, conv_tiles_per_seq=seq // CONV_TS)
    return pl.pallas_call(
        kern,
        grid=(batch, N_HEADS, nq),
        in_specs=[
            pl.BlockSpec(memory_space=pltpu.SMEM),
            q_rows(EVEN_Q),
            pl.BlockSpec((seq, DV), lambda b, h, i: (b, h)),
            pl.BlockSpec((seq, DV), lambda b, h, i: (b, EVEN_V * per_head + h)),
            pl.BlockSpec((1, nd, tk, tk), lambda b, h, i: (h, 0, 0, 0),
                         pipeline_mode=pl.Buffered(1)),
            q_rows(EVEN_B_GATE),
            vec, vec, vec, vec, vec,
            pl.BlockSpec((1, DV), lambda b, h, i: (0, 0)),
        ] + conv_in,
        out_specs=[pl.BlockSpec((tq, DV), lambda b, h, i: (b * nq + i, h)), conv_out],
        out_shape=[jax.ShapeDtypeStruct((n, N_HEADS * DV), BF16),
                   jax.ShapeDtypeStruct((n, W_HALF), BF16)],
        scratch_shapes=[
            pltpu.VMEM((tq, DV), BF16),
            pltpu.VMEM((2, tq, DV), F32),
            pltpu.VMEM((2, tq, DH), F32),
            pltpu.VMEM((2, tq, 1), F32),
        ] + conv_scratch,
        compiler_params=_cparams(("parallel", "parallel", "parallel")),
        name="diff_attention",
    )(exp_range, p, kx, p, bias_tiles, p, q_gain, lq1, lk1, lq2, lk2, subln_g,
      p, p, p, p, p, p, p, conv_w, conv_b, conv_ln_g, conv_ln_b)


ROWS_BF16 = 16


def _fnet_constants(seq, ch):
    r = seq // FFT_INNER
    two_pi = 2.0 * np.pi
    ang_r = two_pi * np.outer(np.arange(r), np.arange(r)) / r
    base = np.stack([np.cos(ang_r), -np.sin(ang_r)], axis=1).reshape(2 * r, r)
    w1 = np.kron(base, np.eye(ROWS_BF16))
    ang_tw = two_pi * np.outer(np.arange(FFT_INNER), np.arange(r)) / seq
    tw = np.stack([np.cos(ang_tw), np.sin(ang_tw)])
    tw = tw.reshape(2, FFT_INNER, r // ROWS_BF16, ROWS_BF16).transpose(2, 0, 1, 3)
    ang_i = two_pi * np.outer(np.arange(FFT_INNER), np.arange(FFT_INNER)) / FFT_INNER
    c, s = np.cos(ang_i), np.sin(ang_i)
    w2 = np.block([[c, s], [-s, c]])
    scale = 1.0 / math.sqrt(seq * GROUP_C)
    eye_g = np.eye(ch // GROUP_C)
    bdc = np.kron(eye_g, c * scale)
    bds = np.kron(eye_g, s * scale)
    perm = np.zeros((ROWS_BF16 * ROWS_BF16,) * 2)
    for b in range(ROWS_BF16):
        for f in range(ROWS_BF16):
            perm[b * ROWS_BF16 + f, f * ROWS_BF16 + b] = 1.0
    as_bf16 = lambda a: jnp.asarray(a, dtype=BF16)
    return (as_bf16(w1), jnp.asarray(tw, dtype=F32), as_bf16(w2), as_bf16(bdc), as_bf16(bds),
            as_bf16(perm))


def _fft1_kernel(x_ref, w_ref, o_ref):
    _, r, rows, c = x_ref.shape
    x = x_ref[0].reshape(r * rows, c)
    y = jnp.dot(w_ref[...], x, preferred_element_type=F32).astype(BF16)
    o_ref[0] = y.reshape(r, 2, rows, c)


def _fft2_kernel(a_ref, tw_ref, w2_ref, bdc_ref, bds_ref, perm_ref, gate_ref, o_ref, u_ref, y_ref):
    nf = a_ref.shape[1]
    ch = a_ref.shape[-1]
    for kk in range(nf):
        br = a_ref[0, kk, 0].astype(F32)
        bi = a_ref[0, kk, 1].astype(F32)
        cw = tw_ref[0, 0][:, kk:kk + 1]
        sw = tw_ref[0, 1][:, kk:kk + 1]
        x = jnp.concatenate([br * cw + bi * sw, bi * cw - br * sw], axis=0).astype(BF16)
        u = jnp.dot(w2_ref[...], x, preferred_element_type=F32).astype(BF16)
        u_ref[0, kk * FFT_INNER:(kk + 1) * FFT_INNER, :] = u[:FFT_INNER]
        u_ref[1, kk * FFT_INNER:(kk + 1) * FFT_INNER, :] = u[FFT_INNER:]
    y_ref[...] = (jnp.dot(u_ref[0], bdc_ref[...], preferred_element_type=F32)
                  + jnp.dot(u_ref[1], bds_ref[...], preferred_element_type=F32)).astype(BF16)
    for a in range(FFT_INNER // ROWS_BF16):
        lo = a * ROWS_BF16
        piece = jnp.concatenate(
            [y_ref[k * FFT_INNER + lo:k * FFT_INNER + lo + ROWS_BF16, :] for k in range(nf)], axis=0)
        z = jnp.dot(perm_ref[...], piece, preferred_element_type=F32)
        g = gate_ref[0, lo:lo + ROWS_BF16].reshape(ROWS_BF16 * nf, ch).astype(F32)
        o_ref[0, lo:lo + ROWS_BF16] = (z * _silu(g)).astype(BF16).reshape(ROWS_BF16, nf, ch)


def _fnet(p, *, batch, seq, ch=512):
    c = W_HALF
    r = seq // FFT_INNER
    nf = ROWS_BF16
    assert r % nf == 0 and c % ch == 0
    w1, tw, w2, bdc, bds, perm = _fnet_constants(seq, ch)
    const = pl.Buffered(1)
    stage1 = pl.pallas_call(
        _fft1_kernel,
        grid=(batch, FFT_INNER // nf),
        in_specs=[
            pl.BlockSpec((1, r, nf, c), lambda b, t: (b, 0, t, ODD_C_IN)),
            pl.BlockSpec((2 * r * nf, r * nf), lambda b, t: (0, 0), pipeline_mode=const),
        ],
        out_specs=pl.BlockSpec((1, r, 2, nf, c), lambda b, t: (b, 0, 0, t, 0)),
        out_shape=jax.ShapeDtypeStruct((batch, r, 2, FFT_INNER, c), BF16),
        compiler_params=_cparams(("parallel", "parallel")),
        name="fft_stage1",
    )(p.reshape(batch, r, FFT_INNER, p.shape[1]), w1)
    gate_cols = ODD_C_GATE * (c // ch)
    cmat = lambda shape: pl.BlockSpec(shape, lambda b, f, j: (0,) * len(shape), pipeline_mode=const)
    out = pl.pallas_call(
        _fft2_kernel,
        grid=(batch, r // nf, c // ch),
        in_specs=[
            pl.BlockSpec((1, nf, 2, FFT_INNER, ch), lambda b, f, j: (b, f, 0, 0, j)),
            pl.BlockSpec((1, 2, FFT_INNER, nf), lambda b, f, j: (f, 0, 0, 0)),
            cmat((2 * FFT_INNER, 2 * FFT_INNER)), cmat((ch, ch)), cmat((ch, ch)),
            cmat((nf * nf, nf * nf)),
            pl.BlockSpec((1, FFT_INNER, nf, ch), lambda b, f, j: (b, 0, f, gate_cols + j)),
        ],
        out_specs=pl.BlockSpec((1, FFT_INNER, nf, ch), lambda b, f, j: (b, 0, f, j)),
        out_shape=jax.ShapeDtypeStruct((batch, FFT_INNER, r, c), BF16),
        scratch_shapes=[pltpu.VMEM((2, nf * FFT_INNER, ch), BF16),
                        pltpu.VMEM((nf * FFT_INNER, ch), BF16)],
        compiler_params=_cparams(("parallel", "parallel", "parallel")),
        name="fft_stage2",
    )(stage1, tw, w2, bdc, bds, perm, p.reshape(batch, FFT_INNER, r, p.shape[1]))
    return out.reshape(batch * seq, c)


def _sgu_kernel(u_ref, v_ref, gate_ref, lg_ref, lb_ref, ws_ref, bt_ref, o_ref, *, tr):
    v = v_ref[...].astype(F32)
    mu = jnp.mean(v, axis=-1, keepdims=True)
    vc = v - mu
    var = jnp.mean(vc * vc, axis=-1, keepdims=True)
    vn = (vc * lax.rsqrt(var + EPS) * lg_ref[...] + lb_ref[...]).astype(BF16)
    for g in range(SGU_GROUPS):
        cols = slice(g * SGU_DG, (g + 1) * SGU_DG)
        bcol = jnp.broadcast_to(bt_ref[:, g:g + 1], (SGU_CHUNK, SGU_DG))
        for n in range(tr // SGU_CHUNK):
            rows = slice(n * SGU_CHUNK, (n + 1) * SGU_CHUNK)
            sv = jnp.dot(ws_ref[g], vn[rows, cols], preferred_element_type=F32) + bcol
            out = u_ref[rows, cols].astype(F32) * sv * _silu(gate_ref[rows, cols].astype(F32))
            o_ref[rows, cols] = out.astype(BF16)


def _sgu_inputs(p, ln_g, ln_b, ws_bf16, b_t, *, tr):
    c = W_HALF

    def seg(s):
        return pl.BlockSpec((tr, c), lambda i: (i, s))

    vec = pl.BlockSpec((1, c), lambda i: (0, 0))
    specs = [seg(ODD_U), seg(ODD_V), seg(ODD_D_GATE), vec, vec,
             pl.BlockSpec((SGU_GROUPS, SGU_CHUNK, SGU_CHUNK), lambda i: (0, 0, 0)),
             pl.BlockSpec((SGU_CHUNK, SGU_GROUPS), lambda i: (0, 0))]
    return [p, p, p, ln_g, ln_b, ws_bf16, b_t], specs


QK_SCALE_LOG2 = LOG2E / math.sqrt(DH)


def _attention_bias(p, *, tk):
    return _bias_tiles(p["rel_bias"], p["q_norm_g"], p["k_norm_g"], tk=tk, q_scale=QK_SCALE_LOG2)


def _trunk(x, p, bias_tiles, exp_range, *, tq, tk):
    batch, seq, d = x.shape
    x2d = x.reshape(batch * seq, d)
    row = lambda a: a.reshape(1, -1)

    lam_init = 0.8 - 0.6 * math.exp(-0.3 * 0)
    pe = _inproj(x2d, row(p["norm_g"][0]), p["w_in_even"], tm=1024, tn=1792)
    kx = _knorm(pe, row(p["k_norm_g"]), tr=1024)
    mix_b, mix_a = _attention_and_conv(
        exp_range, pe, kx, bias_tiles, row(p["q_norm_g"]), row(p["lam_q1"]), row(p["lam_k1"]),
        row(p["lam_q2"]), row(p["lam_k2"]), row(p["subln_g"]), p["conv_w"], row(p["conv_b"]),
        row(p["conv_ln_g"]), row(p["conv_ln_b"]), batch=batch, seq=seq, tq=tq, tk=tk,
        lam_init=lam_init, q_scale=QK_SCALE_LOG2)
    x1 = _outproj(mix_a, mix_b, p["w_out_even"], x2d, tm=512)

    po = _inproj(x1, row(p["norm_g"][1]), p["w_in_odd"], tm=1024, tn=1280)
    mix_c = _fnet(po, batch=batch, seq=seq)
    sgu = _sgu_inputs(po, row(p["sgu_ln_g"]), row(p["sgu_ln_b"]), p["sgu_w"], p["sgu_b"].T, tr=512)
    y = _outproj(mix_c, None, p["w_out_odd"], x1, tm=512, sgu=sgu)
    return y.reshape(batch, seq, d)


def kernel(x_prompt, x_sample, norm_g, w_in_even, conv_w, conv_b, conv_ln_g, conv_ln_b,
           q_norm_g, k_norm_g, lam_q1, lam_k1, lam_q2, lam_k2, subln_g, rel_bias, w_out_even,
           w_in_odd, sgu_ln_g, sgu_ln_b, sgu_w, sgu_b, w_out_odd):
    p = dict(
        norm_g=norm_g, w_in_even=w_in_even[0].astype(BF16), conv_w=conv_w[0], conv_b=conv_b[0],
        conv_ln_g=conv_ln_g[0], conv_ln_b=conv_ln_b[0], q_norm_g=q_norm_g[0],
        k_norm_g=k_norm_g[0], lam_q1=lam_q1[0], lam_k1=lam_k1[0], lam_q2=lam_q2[0],
        lam_k2=lam_k2[0], subln_g=subln_g[0], w_out_even=w_out_even[0].astype(BF16),
        w_in_odd=w_in_odd[0].astype(BF16), sgu_ln_g=sgu_ln_g[0], sgu_ln_b=sgu_ln_b[0],
        sgu_w=sgu_w[0].astype(BF16), sgu_b=sgu_b[0], w_out_odd=w_out_odd[0].astype(BF16),
        rel_bias=rel_bias)
    tq, tk = 1024, 512
    bias_tiles, exp_range = _attention_bias(p, tk=tk)
    y_prompt = _trunk(x_prompt, p, bias_tiles, exp_range, tq=tq, tk=tk)
    y_sample = _trunk(x_sample, p, bias_tiles, exp_range, tq=tq, tk=tk)
    return (y_prompt, y_sample)
```

```python
import functools
import math

import numpy as np
import jax
import jax.numpy as jnp
from jax import lax
from jax.experimental import pallas as pl
from jax.experimental.pallas import tpu as pltpu

F32 = jnp.float32
BF16 = jnp.bfloat16

EPS = 1e-6
LOG2E = math.log2(math.e)

D_MODEL = 2048
W_HALF = D_MODEL // 2
DH = 128
N_HEADS = 4
DV = 2 * DH
CONV_W = 31
CONV_PAD = CONV_W // 2
N_BUCKETS = 32
MAX_DIST = 128
FFT_INNER = 128
GROUP_C = 128
SGU_GROUPS = 4
SGU_CHUNK = 128
SGU_DG = W_HALF // SGU_GROUPS

EVEN_A_LO, EVEN_A_HI, EVEN_A_GATE, EVEN_Q, EVEN_K, EVEN_V, EVEN_B_GATE = range(7)
ODD_C_IN, ODD_C_GATE, ODD_U, ODD_V, ODD_D_GATE = range(5)

VMEM_LIMIT_V7X = 56 * 1024 * 1024
HALO = 16
SUBLANES = 8


def _cparams(sem):
    return pltpu.CompilerParams(dimension_semantics=sem, vmem_limit_bytes=VMEM_LIMIT_V7X)


def _sigmoid(x):
    return 0.5 * jnp.tanh(0.5 * x) + 0.5


def _silu(x):
    return x * _sigmoid(x)


def _inproj_kernel(x_ref, g_ref, w_ref, o_ref, h_ref):
    @pl.when(pl.program_id(1) == 0)
    def _():
        x = x_ref[...]
        ms = jnp.mean(x * x, axis=-1, keepdims=True)
        h_ref[...] = (x * lax.rsqrt(ms + EPS) * g_ref[...]).astype(BF16)

    o_ref[...] = jnp.dot(h_ref[...], w_ref[...], preferred_element_type=F32).astype(BF16)


def _matmul_kernel(h_ref, w_ref, o_ref):
    o_ref[...] = jnp.dot(h_ref[...], w_ref[...], preferred_element_type=F32).astype(BF16)


def _inproj(x2d, g, w_bf16, *, tm, tn):
    n, d = x2d.shape
    p = w_bf16.shape[1]
    rows = pl.BlockSpec((tm, d), lambda i, j: (i, 0))
    cols = pl.BlockSpec((d, tn), lambda i, j: (0, j))
    if g is None:
        kern, operands, in_specs, scratch = _matmul_kernel, (x2d, w_bf16), [rows, cols], []
    else:
        kern, operands = _inproj_kernel, (x2d, g, w_bf16)
        in_specs = [rows, pl.BlockSpec((1, d), lambda i, j: (0, 0)), cols]
        scratch = [pltpu.VMEM((tm, d), BF16)]
    return pl.pallas_call(
        kern,
        grid=(n // tm, p // tn),
        in_specs=in_specs,
        out_specs=pl.BlockSpec((tm, tn), lambda i, j: (i, j)),
        out_shape=jax.ShapeDtypeStruct((n, p), BF16),
        scratch_shapes=scratch,
        compiler_params=_cparams(("parallel", "arbitrary")),
        name="inproj",
    )(*operands)


def _outproj_kernel(ma_ref, *rest, n_sgu_inputs, emit_norm):
    rest = list(rest)
    second = [rest.pop(0) for _ in range(n_sgu_inputs or 1)]
    wa_ref, wb_ref, x_ref = rest[:3]
    rest = rest[3:]
    gain_ref = rest.pop(0) if emit_norm else None
    o_ref = rest.pop(0)
    h_ref = rest.pop(0) if emit_norm else None
    mb_ref = rest.pop(0) if n_sgu_inputs else second[0]
    acc = jnp.dot(ma_ref[...], wa_ref[...], preferred_element_type=F32)
    if n_sgu_inputs:
        _sgu_kernel(*second, mb_ref, tr=mb_ref.shape[0])
    acc = acc + jnp.dot(mb_ref[...], wb_ref[...], preferred_element_type=F32)
    y = x_ref[...] + acc
    o_ref[...] = y
    if emit_norm:
        ms = jnp.mean(y * y, axis=-1, keepdims=True)
        h_ref[...] = (y * lax.rsqrt(ms + EPS) * gain_ref[...]).astype(BF16)


def _outproj(mix_a, mix_b, w_bf16, x2d, *, tm, sgu=None, next_norm_gain=None):
    n, d = x2d.shape
    half = mix_a.shape[1]
    const = pl.Buffered(1)
    rows = pl.BlockSpec((tm, half), lambda i: (i, 0))
    full = pl.BlockSpec((tm, d), lambda i: (i, 0))
    emit_norm = next_norm_gain is not None
    if sgu is None:
        second, second_specs, scratch = [mix_b], [rows], []
    else:
        second, second_specs = sgu
        scratch = [pltpu.VMEM((tm, half), BF16)]
    gain, gain_spec = ([next_norm_gain], [pl.BlockSpec((1, d), lambda i: (0, 0))]) if emit_norm \
        else ([], [])
    out = pl.pallas_call(
        functools.partial(_outproj_kernel, n_sgu_inputs=0 if sgu is None else len(second),
                          emit_norm=emit_norm),
        grid=(n // tm,),
        in_specs=[rows] + second_specs + [
            pl.BlockSpec((half, d), lambda i: (0, 0), pipeline_mode=const),
            pl.BlockSpec((half, d), lambda i: (1, 0), pipeline_mode=const),
            full,
        ] + gain_spec,
        out_specs=[full] + ([full] if emit_norm else []),
        out_shape=[jax.ShapeDtypeStruct((n, d), F32)]
        + ([jax.ShapeDtypeStruct((n, d), BF16)] if emit_norm else []),
        scratch_shapes=scratch,
        compiler_params=_cparams(("parallel",)),
        name="outproj",
    )(mix_a, *second, w_bf16, w_bf16, x2d, *gain)
    return out if emit_norm else out[0]


CONV_TS = 256
CONV_RC = 32
CONV_LANE_PARTS = 2


class ConvRefs:
    def __init__(self, lo, hi, plo, phi, nlo, nhi, gate, w, b, lg, lb, out, xs, wb, y):
        self.lo, self.hi, self.plo, self.phi, self.nlo, self.nhi = lo, hi, plo, phi, nlo, nhi
        self.gate, self.w, self.b, self.lg, self.lb, self.out = gate, w, b, lg, lb, out
        self.xs, self.wb, self.y = xs, wb, y


def _conv_glu(cr, first, last):
    ts, c = cr.y.shape
    padded = ts + 2 * HALO

    def glu(a, b):
        return a.astype(F32) * _sigmoid(b.astype(F32))

    cr.xs[0, HALO:HALO + ts, :] = glu(cr.lo[...], cr.hi[...])
    cr.xs[0, 0:HALO, :] = jnp.where(first, 0.0, glu(cr.plo[...], cr.phi[...]))
    cr.xs[0, HALO + ts:padded, :] = jnp.where(last, 0.0, glu(cr.nlo[...], cr.nhi[...]))
    cr.xs[0, padded:padded + SUBLANES, :] = jnp.zeros((SUBLANES, c), F32)
    for k in range(CONV_W):
        cr.wb[k] = jnp.broadcast_to(cr.w[k:k + 1, :], (SUBLANES, c))


def _conv_shift(cr, q):
    c = cr.y.shape[1]
    lw = c // CONV_LANE_PARTS
    row = pl.multiple_of(q * CONV_RC, CONV_RC)
    for col in range(0, c, lw):
        win = cr.xs[0, pl.ds(row, CONV_RC + SUBLANES), col:col + lw]
        for r in range(1, SUBLANES):
            cr.xs[r, pl.ds(row, CONV_RC), col:col + lw] = pltpu.roll(
                win, CONV_RC + SUBLANES - r, axis=0)[:CONV_RC]


CONV_GROUPS = CONV_RC // SUBLANES
CONV_GROUP_PARTS = 2
CONV_PIECES = CONV_LANE_PARTS * CONV_GROUP_PARTS


def _zero_token(x):
    bits = pltpu.bitcast(x[:SUBLANES, :DH], jnp.uint32)
    return pltpu.bitcast(lax.shift_right_logical(bits, jnp.uint32(32)), F32)


def _conv_tap_piece(cr, q, piece, token=None):
    c = cr.y.shape[1]
    lw = c // CONV_LANE_PARTS
    base0 = HALO - CONV_PAD
    part, gpart = divmod(piece, CONV_GROUP_PARTS)
    per = CONV_GROUPS // CONV_GROUP_PARTS
    groups = range(gpart * per, (gpart + 1) * per)
    col = part * lw
    row = pl.multiple_of(q * CONV_RC, CONV_RC)
    start = jnp.zeros((SUBLANES, lw), F32) if token is None else jnp.concatenate(
        [token] * (lw // DH), axis=-1)
    acc = {g: start for g in groups}
    for r in range(SUBLANES):
        taps = [(a, SUBLANES * a + r - base0) for a in range((base0 + CONV_W - 1) // SUBLANES + 1)
                if 0 <= SUBLANES * a + r - base0 < CONV_W]
        x = {s: cr.xs[r, pl.ds(row + SUBLANES * s, SUBLANES), col:col + lw]
             for s in sorted({a + g for a, _ in taps for g in groups})}
        for a, k in taps:
            wk = cr.wb[k, :, col:col + lw]
            for g in groups:
                acc[g] = acc[g] + x[a + g] * wk
    bias = cr.b[:, col:col + lw]
    for g in groups:
        cr.y[pl.ds(row + SUBLANES * g, SUBLANES), col:col + lw] = acc[g] + bias


def _conv_taps(cr, q):
    for piece in range(CONV_PIECES):
        _conv_tap_piece(cr, q, piece)


def _conv_norm(cr, q):
    row = pl.multiple_of(q * CONV_RC, CONV_RC)
    y = cr.y[pl.ds(row, CONV_RC), :]
    mu = jnp.mean(y, axis=-1, keepdims=True)
    yc = y - mu
    var = jnp.mean(yc * yc, axis=-1, keepdims=True)
    z = yc * lax.rsqrt(var + EPS) * cr.lg[...] + cr.lb[...]
    out = _silu(z) * _silu(cr.gate[pl.ds(row, CONV_RC), :].astype(F32))
    cr.out[pl.ds(row, CONV_RC), :] = out.astype(BF16)


def _conv_phase(cr, chunk_fn, n_chunks, unroll=1):
    def body(q, carry):
        chunk_fn(cr, q)
        return carry
    lax.fori_loop(0, n_chunks, body, 0, unroll=unroll)


def _conv_specs(index_of_step, n_rows):
    c = W_HALF
    hb = CONV_TS // HALO
    nhb = n_rows // HALO

    def rows(col):
        return pl.BlockSpec((CONV_TS, c), lambda *g: (index_of_step(*g), col))

    def prev(col):
        return pl.BlockSpec((HALO, c), lambda *g: (jnp.maximum(index_of_step(*g) * hb - 1, 0), col))

    def nxt(col):
        return pl.BlockSpec(
            (HALO, c), lambda *g: (jnp.minimum((index_of_step(*g) + 1) * hb, nhb - 1), col))

    vec = pl.BlockSpec((1, c), lambda *g: (0, 0))
    in_specs = [rows(EVEN_A_LO), rows(EVEN_A_HI), prev(EVEN_A_LO), prev(EVEN_A_HI),
                nxt(EVEN_A_LO), nxt(EVEN_A_HI), rows(EVEN_A_GATE),
                pl.BlockSpec((CONV_W, c), lambda *g: (0, 0)), vec, vec, vec]
    out_spec = pl.BlockSpec((CONV_TS, c), lambda *g: (index_of_step(*g), 0))
    scratch = [pltpu.VMEM((SUBLANES, CONV_TS + 2 * HALO + SUBLANES, c), F32),
               pltpu.VMEM((CONV_W, SUBLANES, c), F32),
               pltpu.VMEM((CONV_TS, c), F32)]
    return in_specs, out_spec, scratch


def _bucket_thresholds():
    nb = N_BUCKETS // 2
    max_exact = nb // 2
    n = np.arange(1, 4 * MAX_DIST, dtype=np.float64)
    large = max_exact + (np.log(n / max_exact) / math.log(MAX_DIST / max_exact)
                         * (nb - max_exact)).astype(np.int64)
    large = np.minimum(large, nb - 1)
    thr = [int(n[np.argmax(large >= b)]) for b in range(max_exact + 1, nb)]
    return max_exact, thr


def _bias_kernel(rb_ref, qg_ref, kg_ref, o_ref, range_ref, *, tk, lo_diag, q_scale):
    h = pl.program_id(0)
    d = pl.program_id(1) + lo_diag
    gq = jnp.abs(qg_ref[0])
    gk = jnp.abs(kg_ref[0])
    for c in range(1, DH):
        gq = jnp.maximum(gq, jnp.abs(qg_ref[c]))
        gk = jnp.maximum(gk, jnp.abs(kg_ref[c]))
    qk_bound = gq * gk * (DH * q_scale * BF16_SLACK)
    bmax = rb_ref[0, h]
    bmin = rb_ref[0, h]
    for b in range(1, N_BUCKETS):
        bmax = jnp.maximum(bmax, rb_ref[b, h])
        bmin = jnp.minimum(bmin, rb_ref[b, h])
    shift = qk_bound + bmax * LOG2E
    range_ref[h] = 2.0 * qk_bound + (bmax - bmin) * LOG2E

    @pl.when(jnp.abs(d) < FAR)
    def _():
        row = lax.broadcasted_iota(jnp.int32, (tk, tk), 0)
        col = lax.broadcasted_iota(jnp.int32, (tk, tk), 1)
        rel = col - row + d * tk
        n = jnp.abs(rel)
        max_exact, thr = _bucket_thresholds()
        bucket = jnp.minimum(n, max_exact)
        for t in thr:
            bucket = bucket + jnp.where(n >= t, 1, 0)
        bucket = bucket + jnp.where(rel > 0, N_BUCKETS // 2, 0)
        level = [rb_ref[b, h] for b in range(N_BUCKETS)]
        bit = 1
        while len(level) > 1:
            odd = (bucket & bit) != 0
            level = [jnp.where(odd, level[i + 1], level[i]) for i in range(0, len(level), 2)]
            bit *= 2
        o_ref[0, 0] = level[0] * LOG2E - shift

    @pl.when(d <= -FAR)
    def _():
        o_ref[0, 0] = jnp.full((tk, tk), rb_ref[N_BUCKETS // 2 - 1, h] * LOG2E - shift, F32)

    @pl.when(d >= FAR)
    def _():
        o_ref[0, 0] = jnp.full((tk, tk), rb_ref[N_BUCKETS - 1, h] * LOG2E - shift, F32)


EXP2_SAFE_RANGE = 100.0
BF16_SLACK = 1.01
N_DIAG = 5
FAR = N_DIAG // 2


def _bias_tiles(rel_bias, q_gain, k_gain, *, tk, q_scale):
    assert tk + 1 >= _bucket_thresholds()[1][-1]
    nd = N_DIAG
    smem = pl.BlockSpec(memory_space=pltpu.SMEM)
    kern = functools.partial(_bias_kernel, tk=tk, lo_diag=-FAR, q_scale=q_scale)
    return pl.pallas_call(
        kern,
        grid=(N_HEADS, nd),
        in_specs=[smem, smem, smem],
        out_specs=[pl.BlockSpec((1, 1, tk, tk), lambda h, d: (h, d, 0, 0)), smem],
        out_shape=[jax.ShapeDtypeStruct((N_HEADS, nd, tk, tk), F32),
                   jax.ShapeDtypeStruct((N_HEADS,), F32)],
        compiler_params=_cparams(("arbitrary", "arbitrary")),
        name="bias_tiles",
    )(rel_bias, q_gain, k_gain)


def _sub_head_norm(x, gain, scale):
    parts = []
    for t in range(x.shape[1] // DH):
        blk = x[:, t * DH:(t + 1) * DH]
        ms = jnp.mean(blk * blk, axis=-1, keepdims=True)
        parts.append(blk * lax.rsqrt(ms + EPS) * gain * scale)
    return jnp.concatenate(parts, axis=-1)


def _knorm_kernel(k_ref, g_ref, o_ref):
    o_ref[...] = _sub_head_norm(k_ref[...].astype(F32), g_ref[...], 1.0).astype(BF16)


def _knorm(p, k_gain, *, tr):
    n = p.shape[0]
    return pl.pallas_call(
        _knorm_kernel,
        grid=(n // tr,),
        in_specs=[pl.BlockSpec((tr, W_HALF), lambda i: (i, EVEN_K)),
                  pl.BlockSpec((1, DH), lambda i: (0, 0))],
        out_specs=pl.BlockSpec((tr, W_HALF), lambda i: (i, 0)),
        out_shape=jax.ShapeDtypeStruct((n, W_HALF), BF16),
        compiler_params=_cparams(("parallel",)),
        name="knorm",
    )(p, k_gain)


def _attn_kernel(range_ref, q_ref, k_ref, v_ref, bias_ref, gate_ref, qg_ref, lq1_ref, lk1_ref,
                 lq2_ref, lk2_ref, sg_ref, *rest, tq, tk, nkv, lam_init, q_scale, unroll,
                 conv_tiles_per_seq):
    conv_in = rest[:11]
    o_ref, conv_out = rest[11:13]
    qx_ref, acc_ref, ls_ref, l_ref = rest[13:17]
    cr = ConvRefs(*conv_in, conv_out, *rest[17:20])
    h = pl.program_id(1)
    i = pl.program_id(2)
    contract_last = (((1,), (1,)), ((), ()))

    step = (pl.program_id(0) * pl.num_programs(1) + h) * pl.num_programs(2) + i
    seq_pos = step % conv_tiles_per_seq
    _conv_glu(cr, seq_pos == 0, seq_pos == conv_tiles_per_seq - 1)
    conv_chunks = CONV_TS // CONV_RC

    qx_ref[...] = _sub_head_norm(q_ref[...].astype(F32), qg_ref[...], q_scale).astype(BF16)
    acc_ref[...] = jnp.zeros_like(acc_ref)

    nsub = tq // tk

    def shifted_scores(j, t):
        rows = pl.ds(pl.multiple_of(j * tk, tk), tk)
        s = lax.dot_general(qx_ref[:, t * DH:(t + 1) * DH], k_ref[rows, t * DH:(t + 1) * DH],
                            contract_last, preferred_element_type=F32)
        parts = []
        for a in range(nsub):
            diag = jnp.clip(j - (i * nsub + a), -FAR, FAR) + FAR
            parts.append(s[a * tk:(a + 1) * tk] + bias_ref[0, diag])
        return jnp.concatenate(parts, axis=0)

    def bounded():
        ls_ref[...] = jnp.zeros_like(ls_ref)
        trips = nkv // unroll
        chunks_per_trip = conv_chunks // trips
        _conv_phase(cr, _conv_shift, conv_chunks + 1)
        pieces = [(cc, piece) for cc in range(chunks_per_trip) for piece in range(CONV_PIECES)]
        per_softmax = len(pieces) // (2 * unroll)
        assert per_softmax * 2 * unroll == len(pieces)

        def body(jj, carry):
            for u in range(unroll):
                j = unroll * jj + u
                vv = v_ref[pl.ds(pl.multiple_of(j * tk, tk), tk), :]
                for t in range(2):
                    p = jnp.exp2(shifted_scores(j, t))
                    lane_sums = p[:, :DH]
                    for c in range(1, tk // DH):
                        lane_sums = lane_sums + p[:, c * DH:(c + 1) * DH]
                    at = (2 * u + t) * per_softmax
                    for n, (cc, piece) in enumerate(pieces[at:at + per_softmax]):
                        r0 = n * (tq // per_softmax)
                        _conv_tap_piece(cr, jj * chunks_per_trip + cc, piece,
                                        _zero_token(lane_sums[r0:r0 + SUBLANES]))
                    ls_ref[t] = ls_ref[t] + lane_sums
                    acc_ref[t] = acc_ref[t] + jnp.dot(p.astype(BF16), vv,
                                                      preferred_element_type=F32)
            return carry

        lax.fori_loop(0, trips, body, 0)
        _conv_phase(cr, _conv_norm, conv_chunks, unroll=2)
        for t in range(2):
            l_ref[t] = jnp.sum(ls_ref[t], axis=-1, keepdims=True)

    def running_max():
        _conv_phase(cr, _conv_shift, conv_chunks + 1)
        _conv_phase(cr, _conv_taps, conv_chunks)
        _conv_phase(cr, _conv_norm, conv_chunks, unroll=2)

        def body(j, ml):
            vv = v_ref[pl.ds(pl.multiple_of(j * tk, tk), tk), :]
            out = []
            for t in range(2):
                m, l = ml[2 * t], ml[2 * t + 1]
                s = shifted_scores(j, t)
                mn = jnp.maximum(m, jnp.max(s, axis=-1, keepdims=True))
                p = jnp.exp2(s - mn)
                a = jnp.exp2(m - mn)
                l = a * l + jnp.sum(p, axis=-1, keepdims=True)
                acc_ref[t] = a * acc_ref[t] + jnp.dot(p.astype(BF16), vv,
                                                      preferred_element_type=F32)
                out += [mn, l]
            return tuple(out)

        neg = jnp.full((tq, 1), -jnp.inf, F32)
        zero = jnp.zeros((tq, 1), F32)
        _, l1, _, l2 = lax.fori_loop(0, nkv, body, (neg, zero, neg, zero))
        l_ref[0] = l1
        l_ref[1] = l2

    lax.cond(range_ref[h] <= EXP2_SAFE_RANGE, bounded, running_max)

    lam = (jnp.exp(jnp.sum(lq1_ref[...] * lk1_ref[...], axis=-1, keepdims=True))
           - jnp.exp(jnp.sum(lq2_ref[...] * lk2_ref[...], axis=-1, keepdims=True)) + lam_init)
    o = acc_ref[0] * (1.0 / l_ref[0]) - lam * (acc_ref[1] * (1.0 / l_ref[1]))
    ms = jnp.mean(o * o, axis=-1, keepdims=True)
    y = o * lax.rsqrt(ms + EPS) * sg_ref[...] * (1.0 - lam_init)
    o_ref[...] = (y * _silu(gate_ref[...].astype(F32))).astype(BF16)


def _attention_and_conv(exp_range, p, kx, bias_tiles, q_gain, lq1, lk1, lq2, lk2, subln_g, conv_w,
                        conv_b, conv_ln_g, conv_ln_b, *, batch, seq, tq, tk, lam_init, q_scale):
    n = p.shape[0]
    nq = seq // tq
    nd = bias_tiles.shape[1]
    nkv = seq // tk
    unroll = 4
    trips = nkv // unroll
    conv_chunks = CONV_TS // CONV_RC
    assert tq % tk == 0 and bias_tiles.shape[2:] == (tk, tk) and nkv % unroll == 0
    assert batch * N_HEADS * nq * CONV_TS == n and conv_chunks % trips == 0
    assert seq % CONV_TS == 0 and (CONV_TS + 2 * HALO) % CONV_RC == 0
    per_head = W_HALF // DV

    def q_rows(seg):
        return pl.BlockSpec((tq, DV), lambda b, h, i: (b * nq + i, seg * per_head + h))

    vec = pl.BlockSpec((1, DH), lambda b, h, i: (0, 0))
    conv_in, conv_out, conv_scratch = _conv_specs(lambda b, h, i: (b * N_HEADS + h) * nq + i, n)
    kern = functools.partial(_attn_kernel, tq=tq, tk=tk, nkv=nkv, lam_init=lam_init,
                             q_scale=q_scale, unroll=unroll, conv_tiles_per_seq=seq // CONV_TS)
    return pl.pallas_call(
        kern,
        grid=(batch, N_HEADS, nq),
        in_specs=[
            pl.BlockSpec(memory_space=pltpu.SMEM),
            q_rows(EVEN_Q),
            pl.BlockSpec((seq, DV), lambda b, h, i: (b, h)),
            pl.BlockSpec((seq, DV), lambda b, h, i: (b, EVEN_V * per_head + h)),
            pl.BlockSpec((1, nd, tk, tk), lambda b, h, i: (h, 0, 0, 0),
                         pipeline_mode=pl.Buffered(1)),
            q_rows(EVEN_B_GATE),
            vec, vec, vec, vec, vec,
            pl.BlockSpec((1, DV), lambda b, h, i: (0, 0)),
        ] + conv_in,
        out_specs=[pl.BlockSpec((tq, DV), lambda b, h, i: (b * nq + i, h)), conv_out],
        out_shape=[jax.ShapeDtypeStruct((n, N_HEADS * DV), BF16),
                   jax.ShapeDtypeStruct((n, W_HALF), BF16)],
        scratch_shapes=[
            pltpu.VMEM((tq, DV), BF16),
            pltpu.VMEM((2, tq, DV), F32),
            pltpu.VMEM((2, tq, DH), F32),
            pltpu.VMEM((2, tq, 1), F32),
        ] + conv_scratch,
        compiler_params=_cparams(("parallel", "parallel", "parallel")),
        name="diff_attention",
    )(exp_range, p, kx, p, bias_tiles, p, q_gain, lq1, lk1, lq2, lk2, subln_g,
      p, p, p, p, p, p, p, conv_w, conv_b, conv_ln_g, conv_ln_b)


ROWS_BF16 = 16


def _fnet_constants(seq, ch):
    r = seq // FFT_INNER
    two_pi = 2.0 * np.pi
    ang_r = two_pi * np.outer(np.arange(r), np.arange(r)) / r
    base = np.stack([np.cos(ang_r), -np.sin(ang_r)], axis=1).reshape(2 * r, r)
    w1 = np.kron(base, np.eye(ROWS_BF16))
    ang_tw = two_pi * np.outer(np.arange(FFT_INNER), np.arange(r)) / seq
    tw = np.stack([np.cos(ang_tw), np.sin(ang_tw)])
    tw = tw.reshape(2, FFT_INNER, r // ROWS_BF16, ROWS_BF16).transpose(2, 0, 1, 3)
    ang_i = two_pi * np.outer(np.arange(FFT_INNER), np.arange(FFT_INNER)) / FFT_INNER
    c, s = np.cos(ang_i), np.sin(ang_i)
    w2 = np.block([[c, s], [-s, c]])
    scale = 1.0 / math.sqrt(seq * GROUP_C)
    eye_g = np.eye(ch // GROUP_C)
    bdc = np.kron(eye_g, c * scale)
    bds = np.kron(eye_g, s * scale)
    perm = np.zeros((ROWS_BF16 * ROWS_BF16,) * 2)
    for b in range(ROWS_BF16):
        for f in range(ROWS_BF16):
            perm[b * ROWS_BF16 + f, f * ROWS_BF16 + b] = 1.0
    as_bf16 = lambda a: jnp.asarray(a, dtype=BF16)
    return (as_bf16(w1), jnp.asarray(tw, dtype=F32), as_bf16(w2), as_bf16(bdc), as_bf16(bds),
            as_bf16(perm))


def _fft1_kernel(x_ref, w_ref, o_ref):
    _, r, rows, c = x_ref.shape
    x = x_ref[0].reshape(r * rows, c)
    y = jnp.dot(w_ref[...], x, preferred_element_type=F32).astype(BF16)
    o_ref[0] = y.reshape(r, 2, rows, c)


def _fft2_kernel(a_ref, tw_ref, w2_ref, bdc_ref, bds_ref, perm_ref, gate_ref, o_ref, u_ref, y_ref):
    nf = a_ref.shape[1]
    ch = a_ref.shape[-1]
    for kk in range(nf):
        br = a_ref[0, kk, 0].astype(F32)
        bi = a_ref[0, kk, 1].astype(F32)
        cw = tw_ref[0, 0][:, kk:kk + 1]
        sw = tw_ref[0, 1][:, kk:kk + 1]
        x = jnp.concatenate([br * cw + bi * sw, bi * cw - br * sw], axis=0).astype(BF16)
        u = jnp.dot(w2_ref[...], x, preferred_element_type=F32).astype(BF16)
        u_ref[0, kk * FFT_INNER:(kk + 1) * FFT_INNER, :] = u[:FFT_INNER]
        u_ref[1, kk * FFT_INNER:(kk + 1) * FFT_INNER, :] = u[FFT_INNER:]
    y_ref[...] = (jnp.dot(u_ref[0], bdc_ref[...], preferred_element_type=F32)
                  + jnp.dot(u_ref[1], bds_ref[...], preferred_element_type=F32)).astype(BF16)
    for a in range(FFT_INNER // ROWS_BF16):
        lo = a * ROWS_BF16
        piece = jnp.concatenate(
            [y_ref[k * FFT_INNER + lo:k * FFT_INNER + lo + ROWS_BF16, :] for k in range(nf)], axis=0)
        z = jnp.dot(perm_ref[...], piece, preferred_element_type=F32)
        g = gate_ref[0, lo:lo + ROWS_BF16].reshape(ROWS_BF16 * nf, ch).astype(F32)
        o_ref[0, lo:lo + ROWS_BF16] = (z * _silu(g)).astype(BF16).reshape(ROWS_BF16, nf, ch)


def _fnet(p, *, batch, seq, ch=512):
    c = W_HALF
    r = seq // FFT_INNER
    nf = ROWS_BF16
    assert r % nf == 0 and c % ch == 0
    w1, tw, w2, bdc, bds, perm = _fnet_constants(seq, ch)
    const = pl.Buffered(1)
    stage1 = pl.pallas_call(
        _fft1_kernel,
        grid=(batch, FFT_INNER // nf),
        in_specs=[
            pl.BlockSpec((1, r, nf, c), lambda b, t: (b, 0, t, ODD_C_IN)),
            pl.BlockSpec((2 * r * nf, r * nf), lambda b, t: (0, 0), pipeline_mode=const),
        ],
        out_specs=pl.BlockSpec((1, r, 2, nf, c), lambda b, t: (b, 0, 0, t, 0)),
        out_shape=jax.ShapeDtypeStruct((batch, r, 2, FFT_INNER, c), BF16),
        compiler_params=_cparams(("parallel", "parallel")),
        name="fft_stage1",
    )(p.reshape(batch, r, FFT_INNER, p.shape[1]), w1)
    gate_cols = ODD_C_GATE * (c // ch)
    cmat = lambda shape: pl.BlockSpec(shape, lambda b, f, j: (0,) * len(shape), pipeline_mode=const)
    out = pl.pallas_call(
        _fft2_kernel,
        grid=(batch, r // nf, c // ch),
        in_specs=[
            pl.BlockSpec((1, nf, 2, FFT_INNER, ch), lambda b, f, j: (b, f, 0, 0, j)),
            pl.BlockSpec((1, 2, FFT_INNER, nf), lambda b, f, j: (f, 0, 0, 0)),
            cmat((2 * FFT_INNER, 2 * FFT_INNER)), cmat((ch, ch)), cmat((ch, ch)),
            cmat((nf * nf, nf * nf)),
            pl.BlockSpec((1, FFT_INNER, nf, ch), lambda b, f, j: (b, 0, f, gate_cols + j)),
        ],
        out_specs=pl.BlockSpec((1, FFT_INNER, nf, ch), lambda b, f, j: (b, 0, f, j)),
        out_shape=jax.ShapeDtypeStruct((batch, FFT_INNER, r, c), BF16),
        scratch_shapes=[pltpu.VMEM((2, nf * FFT_INNER, ch), BF16),
                        pltpu.VMEM((nf * FFT_INNER, ch), BF16)],
        compiler_params=_cparams(("parallel", "parallel", "parallel")),
        name="fft_stage2",
    )(stage1, tw, w2, bdc, bds, perm, p.reshape(batch, FFT_INNER, r, p.shape[1]))
    return out.reshape(batch * seq, c)


def _sgu_kernel(u_ref, v_ref, gate_ref, lg_ref, lb_ref, ws_ref, bt_ref, o_ref, *, tr):
    v = v_ref[...].astype(F32)
    mu = jnp.mean(v, axis=-1, keepdims=True)
    vc = v - mu
    var = jnp.mean(vc * vc, axis=-1, keepdims=True)
    vn = (vc * lax.rsqrt(var + EPS) * lg_ref[...] + lb_ref[...]).astype(BF16)
    for g in range(SGU_GROUPS):
        cols = slice(g * SGU_DG, (g + 1) * SGU_DG)
        bcol = jnp.broadcast_to(bt_ref[:, g:g + 1], (SGU_CHUNK, SGU_DG))
        for n in range(tr // SGU_CHUNK):
            rows = slice(n * SGU_CHUNK, (n + 1) * SGU_CHUNK)
            sv = jnp.dot(ws_ref[g], vn[rows, cols], preferred_element_type=F32) + bcol
            out = u_ref[rows, cols].astype(F32) * sv * _silu(gate_ref[rows, cols].astype(F32))
            o_ref[rows, cols] = out.astype(BF16)


def _sgu_inputs(p, ln_g, ln_b, ws_bf16, b_t, *, tr):
    c = W_HALF

    def seg(s):
        return pl.BlockSpec((tr, c), lambda i: (i, s))

    vec = pl.BlockSpec((1, c), lambda i: (0, 0))
    specs = [seg(ODD_U), seg(ODD_V), seg(ODD_D_GATE), vec, vec,
             pl.BlockSpec((SGU_GROUPS, SGU_CHUNK, SGU_CHUNK), lambda i: (0, 0, 0)),
             pl.BlockSpec((SGU_CHUNK, SGU_GROUPS), lambda i: (0, 0))]
    return [p, p, p, ln_g, ln_b, ws_bf16, b_t], specs


QK_SCALE_LOG2 = LOG2E / math.sqrt(DH)


def _attention_bias(p, *, tk):
    return _bias_tiles(p["rel_bias"], p["q_norm_g"], p["k_norm_g"], tk=tk, q_scale=QK_SCALE_LOG2)


def _trunk(x, p, bias_tiles, exp_range, *, tq, tk):
    batch, seq, d = x.shape
    x2d = x.reshape(batch * seq, d)
    row = lambda a: a.reshape(1, -1)

    lam_init = 0.8 - 0.6 * math.exp(-0.3 * 0)
    pe = _inproj(x2d, row(p["norm_g"][0]), p["w_in_even"], tm=1024, tn=1792)
    kx = _knorm(pe, row(p["k_norm_g"]), tr=1024)
    mix_b, mix_a = _attention_and_conv(
        exp_range, pe, kx, bias_tiles, row(p["q_norm_g"]), row(p["lam_q1"]), row(p["lam_k1"]),
        row(p["lam_q2"]), row(p["lam_k2"]), row(p["subln_g"]), p["conv_w"], row(p["conv_b"]),
        row(p["conv_ln_g"]), row(p["conv_ln_b"]), batch=batch, seq=seq, tq=tq, tk=tk,
        lam_init=lam_init, q_scale=QK_SCALE_LOG2)
    x1, h1 = _outproj(mix_a, mix_b, p["w_out_even"], x2d, tm=512,
                      next_norm_gain=row(p["norm_g"][1]))

    po = _inproj(h1, None, p["w_in_odd"], tm=1024, tn=1280)
    mix_c = _fnet(po, batch=batch, seq=seq)
    sgu = _sgu_inputs(po, row(p["sgu_ln_g"]), row(p["sgu_ln_b"]), p["sgu_w"], p["sgu_b"].T, tr=512)
    y = _outproj(mix_c, None, p["w_out_odd"], x1, tm=512, sgu=sgu)
    return y.reshape(batch, seq, d)


def kernel(x_prompt, x_sample, norm_g, w_in_even, conv_w, conv_b, conv_ln_g, conv_ln_b,
           q_norm_g, k_norm_g, lam_q1, lam_k1, lam_q2, lam_k2, subln_g, rel_bias, w_out_even,
           w_in_odd, sgu_ln_g, sgu_ln_b, sgu_w, sgu_b, w_out_odd):
    p = dict(
        norm_g=norm_g, w_in_even=w_in_even[0].astype(BF16), conv_w=conv_w[0], conv_b=conv_b[0],
        conv_ln_g=conv_ln_g[0], conv_ln_b=conv_ln_b[0], q_norm_g=q_norm_g[0],
        k_norm_g=k_norm_g[0], lam_q1=lam_q1[0], lam_k1=lam_k1[0], lam_q2=lam_q2[0],
        lam_k2=lam_k2[0], subln_g=subln_g[0], w_out_even=w_out_even[0].astype(BF16),
        w_in_odd=w_in_odd[0].astype(BF16), sgu_ln_g=sgu_ln_g[0], sgu_ln_b=sgu_ln_b[0],
        sgu_w=sgu_w[0].astype(BF16), sgu_b=sgu_b[0], w_out_odd=w_out_odd[0].astype(BF16),
        rel_bias=rel_bias)
    tq, tk = 1024, 512
    bias_tiles, exp_range = _attention_bias(p, tk=tk)
    y_prompt = _trunk(x_prompt, p, bias_tiles, exp_range, tq=tq, tk=tk)
    y_sample = _trunk(x_sample, p, bias_tiles, exp_range, tq=tq, tk=tk)
    return (y_prompt, y_sample)
```

```python
import functools
import math

import numpy as np
import jax
import jax.numpy as jnp
from jax import lax
from jax.experimental import pallas as pl
from jax.experimental.pallas import tpu as pltpu

F32 = jnp.float32
BF16 = jnp.bfloat16

EPS = 1e-6
LOG2E = math.log2(math.e)

D_MODEL = 2048
W_HALF = D_MODEL // 2
DH = 128
N_HEADS = 4
DV = 2 * DH
CONV_W = 31
CONV_PAD = CONV_W // 2
N_BUCKETS = 32
MAX_DIST = 128
FFT_INNER = 128
GROUP_C = 128
SGU_GROUPS = 4
SGU_CHUNK = 128
SGU_DG = W_HALF // SGU_GROUPS

EVEN_A_LO, EVEN_A_HI, EVEN_A_GATE, EVEN_Q, EVEN_K, EVEN_V, EVEN_B_GATE = range(7)
ODD_C_IN, ODD_C_GATE, ODD_U, ODD_V, ODD_D_GATE = range(5)

VMEM_LIMIT_V7X = 56 * 1024 * 1024
HALO = 16
SUBLANES = 8


def _cparams(sem):
    return pltpu.CompilerParams(dimension_semantics=sem, vmem_limit_bytes=VMEM_LIMIT_V7X)


def _sigmoid(x):
    return 0.5 * jnp.tanh(0.5 * x) + 0.5


def _silu(x):
    return x * _sigmoid(x)


def _inproj_kernel(x_ref, g_ref, w_ref, o_ref, h_ref):
    @pl.when(pl.program_id(1) == 0)
    def _():
        x = x_ref[...]
        ms = jnp.mean(x * x, axis=-1, keepdims=True)
        h_ref[...] = (x * lax.rsqrt(ms + EPS) * g_ref[...]).astype(BF16)

    o_ref[...] = jnp.dot(h_ref[...], w_ref[...], preferred_element_type=F32).astype(BF16)


def _inproj(x2d, g, w_bf16, *, tm, tn):
    n, d = x2d.shape
    p = w_bf16.shape[1]
    return pl.pallas_call(
        _inproj_kernel,
        grid=(n // tm, p // tn),
        in_specs=[
            pl.BlockSpec((tm, d), lambda i, j: (i, 0)),
            pl.BlockSpec((1, d), lambda i, j: (0, 0)),
            pl.BlockSpec((d, tn), lambda i, j: (0, j)),
        ],
        out_specs=pl.BlockSpec((tm, tn), lambda i, j: (i, j)),
        out_shape=jax.ShapeDtypeStruct((n, p), BF16),
        scratch_shapes=[pltpu.VMEM((tm, d), BF16)],
        compiler_params=_cparams(("parallel", "arbitrary")),
        name="inproj",
    )(x2d, g, w_bf16)


def _matmul_kernel(h_ref, w_ref, o_ref, wb_ref):
    @pl.when(pl.program_id(1) == 0)
    def _():
        wb_ref[...] = w_ref[...].astype(BF16)

    o_ref[...] = jnp.dot(h_ref[...], wb_ref[...], preferred_element_type=F32).astype(BF16)


def _inproj_normalised(h, w_f32, *, tm, tn):
    n, d = h.shape
    p = w_f32.shape[1]
    return pl.pallas_call(
        _matmul_kernel,
        grid=(p // tn, n // tm),
        in_specs=[
            pl.BlockSpec((tm, d), lambda j, i: (i, 0)),
            pl.BlockSpec((d, tn), lambda j, i: (0, j)),
        ],
        out_specs=pl.BlockSpec((tm, tn), lambda j, i: (i, j)),
        out_shape=jax.ShapeDtypeStruct((n, p), BF16),
        scratch_shapes=[pltpu.VMEM((d, tn), BF16)],
        compiler_params=_cparams(("arbitrary", "arbitrary")),
        name="inproj",
    )(h, w_f32)


def _outproj_kernel(ma_ref, *rest, n_sgu_inputs, emit_norm):
    rest = list(rest)
    second = [rest.pop(0) for _ in range(n_sgu_inputs or 1)]
    wa_ref, wb_ref, x_ref = rest[:3]
    rest = rest[3:]
    gain_ref = rest.pop(0) if emit_norm else None
    o_ref = rest.pop(0)
    h_ref = rest.pop(0) if emit_norm else None
    w_ref = rest.pop(0)
    mb_ref = rest.pop(0) if n_sgu_inputs else second[0]

    @pl.when(pl.program_id(0) == 0)
    def _():
        w_ref[0] = wa_ref[...].astype(BF16)
        w_ref[1] = wb_ref[...].astype(BF16)

    acc = jnp.dot(ma_ref[...], w_ref[0], preferred_element_type=F32)
    if n_sgu_inputs:
        _sgu_kernel(*second, mb_ref, tr=mb_ref.shape[0])
    acc = acc + jnp.dot(mb_ref[...], w_ref[1], preferred_element_type=F32)
    y = x_ref[...] + acc
    o_ref[...] = y
    if emit_norm:
        ms = jnp.mean(y * y, axis=-1, keepdims=True)
        h_ref[...] = (y * lax.rsqrt(ms + EPS) * gain_ref[...]).astype(BF16)


def _outproj(mix_a, mix_b, w_f32, x2d, *, tm, sgu=None, next_norm_gain=None):
    n, d = x2d.shape
    half = mix_a.shape[1]
    const = pl.Buffered(1)
    rows = pl.BlockSpec((tm, half), lambda i: (i, 0))
    full = pl.BlockSpec((tm, d), lambda i: (i, 0))
    emit_norm = next_norm_gain is not None
    scratch = [pltpu.VMEM((2, half, d), BF16)]
    if sgu is None:
        second, second_specs = [mix_b], [rows]
    else:
        second, second_specs = sgu
        scratch.append(pltpu.VMEM((tm, half), BF16))
    gain, gain_spec = ([next_norm_gain], [pl.BlockSpec((1, d), lambda i: (0, 0))]) if emit_norm \
        else ([], [])
    out = pl.pallas_call(
        functools.partial(_outproj_kernel, n_sgu_inputs=0 if sgu is None else len(second),
                          emit_norm=emit_norm),
        grid=(n // tm,),
        in_specs=[rows] + second_specs + [
            pl.BlockSpec((half, d), lambda i: (0, 0), pipeline_mode=const),
            pl.BlockSpec((half, d), lambda i: (1, 0), pipeline_mode=const),
            full,
        ] + gain_spec,
        out_specs=[full] + ([full] if emit_norm else []),
        out_shape=[jax.ShapeDtypeStruct((n, d), F32)]
        + ([jax.ShapeDtypeStruct((n, d), BF16)] if emit_norm else []),
        scratch_shapes=scratch,
        compiler_params=_cparams(("arbitrary",)),
        name="outproj",
    )(mix_a, *second, w_f32, w_f32, x2d, *gain)
    return out if emit_norm else out[0]


CONV_TS = 256
CONV_RC = 32
CONV_LANE_PARTS = 2


class ConvRefs:
    def __init__(self, lo, hi, plo, phi, nlo, nhi, gate, w, b, lg, lb, out, xs, wb, y):
        self.lo, self.hi, self.plo, self.phi, self.nlo, self.nhi = lo, hi, plo, phi, nlo, nhi
        self.gate, self.w, self.b, self.lg, self.lb, self.out = gate, w, b, lg, lb, out
        self.xs, self.wb, self.y = xs, wb, y


def _conv_glu(cr, first, last):
    ts, c = cr.y.shape
    padded = ts + 2 * HALO

    def glu(a, b):
        return a.astype(F32) * _sigmoid(b.astype(F32))

    cr.xs[0, HALO:HALO + ts, :] = glu(cr.lo[...], cr.hi[...])
    cr.xs[0, 0:HALO, :] = jnp.where(first, 0.0, glu(cr.plo[...], cr.phi[...]))
    cr.xs[0, HALO + ts:padded, :] = jnp.where(last, 0.0, glu(cr.nlo[...], cr.nhi[...]))
    cr.xs[0, padded:padded + SUBLANES, :] = jnp.zeros((SUBLANES, c), F32)
    for k in range(CONV_W):
        cr.wb[k] = jnp.broadcast_to(cr.w[k:k + 1, :], (SUBLANES, c))


def _conv_shift(cr, q):
    c = cr.y.shape[1]
    lw = c // CONV_LANE_PARTS
    row = pl.multiple_of(q * CONV_RC, CONV_RC)
    for col in range(0, c, lw):
        win = cr.xs[0, pl.ds(row, CONV_RC + SUBLANES), col:col + lw]
        for r in range(1, SUBLANES):
            cr.xs[r, pl.ds(row, CONV_RC), col:col + lw] = pltpu.roll(
                win, CONV_RC + SUBLANES - r, axis=0)[:CONV_RC]


CONV_GROUPS = CONV_RC // SUBLANES
CONV_GROUP_PARTS = 2
CONV_PIECES = CONV_LANE_PARTS * CONV_GROUP_PARTS


def _zero_token(x):
    bits = pltpu.bitcast(x[:SUBLANES, :DH], jnp.uint32)
    return pltpu.bitcast(lax.shift_right_logical(bits, jnp.uint32(32)), F32)


def _conv_tap_piece(cr, q, piece, token=None):
    c = cr.y.shape[1]
    lw = c // CONV_LANE_PARTS
    base0 = HALO - CONV_PAD
    part, gpart = divmod(piece, CONV_GROUP_PARTS)
    per = CONV_GROUPS // CONV_GROUP_PARTS
    groups = range(gpart * per, (gpart + 1) * per)
    col = part * lw
    row = pl.multiple_of(q * CONV_RC, CONV_RC)
    start = jnp.zeros((SUBLANES, lw), F32) if token is None else jnp.concatenate(
        [token] * (lw // DH), axis=-1)
    acc = {g: start for g in groups}
    for r in range(SUBLANES):
        taps = [(a, SUBLANES * a + r - base0) for a in range((base0 + CONV_W - 1) // SUBLANES + 1)
                if 0 <= SUBLANES * a + r - base0 < CONV_W]
        x = {s: cr.xs[r, pl.ds(row + SUBLANES * s, SUBLANES), col:col + lw]
             for s in sorted({a + g for a, _ in taps for g in groups})}
        for a, k in taps:
            wk = cr.wb[k, :, col:col + lw]
            for g in groups:
                acc[g] = acc[g] + x[a + g] * wk
    bias = cr.b[:, col:col + lw]
    for g in groups:
        cr.y[pl.ds(row + SUBLANES * g, SUBLANES), col:col + lw] = acc[g] + bias


def _conv_taps(cr, q):
    for piece in range(CONV_PIECES):
        _conv_tap_piece(cr, q, piece)


def _conv_norm(cr, q):
    row = pl.multiple_of(q * CONV_RC, CONV_RC)
    y = cr.y[pl.ds(row, CONV_RC), :]
    mu = jnp.mean(y, axis=-1, keepdims=True)
    yc = y - mu
    var = jnp.mean(yc * yc, axis=-1, keepdims=True)
    z = yc * lax.rsqrt(var + EPS) * cr.lg[...] + cr.lb[...]
    out = _silu(z) * _silu(cr.gate[pl.ds(row, CONV_RC), :].astype(F32))
    cr.out[pl.ds(row, CONV_RC), :] = out.astype(BF16)


def _conv_phase(cr, chunk_fn, n_chunks, unroll=1):
    def body(q, carry):
        chunk_fn(cr, q)
        return carry
    lax.fori_loop(0, n_chunks, body, 0, unroll=unroll)


def _conv_specs(index_of_step, n_rows):
    c = W_HALF
    hb = CONV_TS // HALO
    nhb = n_rows // HALO

    def rows(col):
        return pl.BlockSpec((CONV_TS, c), lambda *g: (index_of_step(*g), col))

    def prev(col):
        return pl.BlockSpec((HALO, c), lambda *g: (jnp.maximum(index_of_step(*g) * hb - 1, 0), col))

    def nxt(col):
        return pl.BlockSpec(
            (HALO, c), lambda *g: (jnp.minimum((index_of_step(*g) + 1) * hb, nhb - 1), col))

    vec = pl.BlockSpec((1, c), lambda *g: (0, 0))
    in_specs = [rows(EVEN_A_LO), rows(EVEN_A_HI), prev(EVEN_A_LO), prev(EVEN_A_HI),
                nxt(EVEN_A_LO), nxt(EVEN_A_HI), rows(EVEN_A_GATE),
                pl.BlockSpec((CONV_W, c), lambda *g: (0, 0)), vec, vec, vec]
    out_spec = pl.BlockSpec((CONV_TS, c), lambda *g: (index_of_step(*g), 0))
    scratch = [pltpu.VMEM((SUBLANES, CONV_TS + 2 * HALO + SUBLANES, c), F32),
               pltpu.VMEM((CONV_W, SUBLANES, c), F32),
               pltpu.VMEM((CONV_TS, c), F32)]
    return in_specs, out_spec, scratch


def _bucket_thresholds():
    nb = N_BUCKETS // 2
    max_exact = nb // 2
    n = np.arange(1, 4 * MAX_DIST, dtype=np.float64)
    large = max_exact + (np.log(n / max_exact) / math.log(MAX_DIST / max_exact)
                         * (nb - max_exact)).astype(np.int64)
    large = np.minimum(large, nb - 1)
    thr = [int(n[np.argmax(large >= b)]) for b in range(max_exact + 1, nb)]
    return max_exact, thr


def _bias_kernel(rb_ref, qg_ref, kg_ref, o_ref, range_ref, *, tk, lo_diag, q_scale):
    h = pl.program_id(0)
    d = pl.program_id(1) + lo_diag
    gq = jnp.abs(qg_ref[0])
    gk = jnp.abs(kg_ref[0])
    for c in range(1, DH):
        gq = jnp.maximum(gq, jnp.abs(qg_ref[c]))
        gk = jnp.maximum(gk, jnp.abs(kg_ref[c]))
    qk_bound = gq * gk * (DH * q_scale * BF16_SLACK)
    bmax = rb_ref[0, h]
    bmin = rb_ref[0, h]
    for b in range(1, N_BUCKETS):
        bmax = jnp.maximum(bmax, rb_ref[b, h])
        bmin = jnp.minimum(bmin, rb_ref[b, h])
    shift = qk_bound + bmax * LOG2E
    range_ref[h] = 2.0 * qk_bound + (bmax - bmin) * LOG2E

    @pl.when(jnp.abs(d) < FAR)
    def _():
        row = lax.broadcasted_iota(jnp.int32, (tk, tk), 0)
        col = lax.broadcasted_iota(jnp.int32, (tk, tk), 1)
        rel = col - row + d * tk
        n = jnp.abs(rel)
        max_exact, thr = _bucket_thresholds()
        bucket = jnp.minimum(n, max_exact)
        for t in thr:
            bucket = bucket + jnp.where(n >= t, 1, 0)
        bucket = bucket + jnp.where(rel > 0, N_BUCKETS // 2, 0)
        level = [rb_ref[b, h] for b in range(N_BUCKETS)]
        bit = 1
        while len(level) > 1:
            odd = (bucket & bit) != 0
            level = [jnp.where(odd, level[i + 1], level[i]) for i in range(0, len(level), 2)]
            bit *= 2
        o_ref[0, 0] = level[0] * LOG2E - shift

    @pl.when(d <= -FAR)
    def _():
        o_ref[0, 0] = jnp.full((tk, tk), rb_ref[N_BUCKETS // 2 - 1, h] * LOG2E - shift, F32)

    @pl.when(d >= FAR)
    def _():
        o_ref[0, 0] = jnp.full((tk, tk), rb_ref[N_BUCKETS - 1, h] * LOG2E - shift, F32)


EXP2_SAFE_RANGE = 100.0
BF16_SLACK = 1.01
N_DIAG = 5
FAR = N_DIAG // 2


def _bias_tiles(rel_bias, q_gain, k_gain, *, tk, q_scale):
    assert tk + 1 >= _bucket_thresholds()[1][-1]
    nd = N_DIAG
    smem = pl.BlockSpec(memory_space=pltpu.SMEM)
    kern = functools.partial(_bias_kernel, tk=tk, lo_diag=-FAR, q_scale=q_scale)
    return pl.pallas_call(
        kern,
        grid=(N_HEADS, nd),
        in_specs=[smem, smem, smem],
        out_specs=[pl.BlockSpec((1, 1, tk, tk), lambda h, d: (h, d, 0, 0)), smem],
        out_shape=[jax.ShapeDtypeStruct((N_HEADS, nd, tk, tk), F32),
                   jax.ShapeDtypeStruct((N_HEADS,), F32)],
        compiler_params=_cparams(("arbitrary", "arbitrary")),
        name="bias_tiles",
    )(rel_bias, q_gain, k_gain)


def _sub_head_norm(x, gain, scale):
    parts = []
    for t in range(x.shape[1] // DH):
        blk = x[:, t * DH:(t + 1) * DH]
        ms = jnp.mean(blk * blk, axis=-1, keepdims=True)
        parts.append(blk * lax.rsqrt(ms + EPS) * gain * scale)
    return jnp.concatenate(parts, axis=-1)


def _knorm_kernel(k_ref, g_ref, o_ref):
    o_ref[...] = _sub_head_norm(k_ref[...].astype(F32), g_ref[...], 1.0).astype(BF16)


def _knorm(p, k_gain, *, tr):
    n = p.shape[0]
    return pl.pallas_call(
        _knorm_kernel,
        grid=(n // tr,),
        in_specs=[pl.BlockSpec((tr, W_HALF), lambda i: (i, EVEN_K)),
                  pl.BlockSpec((1, DH), lambda i: (0, 0))],
        out_specs=pl.BlockSpec((tr, W_HALF), lambda i: (i, 0)),
        out_shape=jax.ShapeDtypeStruct((n, W_HALF), BF16),
        compiler_params=_cparams(("parallel",)),
        name="knorm",
    )(p, k_gain)


def _attn_kernel(range_ref, q_ref, k_ref, v_ref, bias_ref, gate_ref, qg_ref, lq1_ref, lk1_ref,
                 lq2_ref, lk2_ref, sg_ref, *rest, tq, tk, nkv, lam_init, q_scale, unroll,
                 conv_tiles_per_seq):
    conv_in = rest[:11]
    o_ref, conv_out = rest[11:13]
    qx_ref, acc_ref, ls_ref, l_ref = rest[13:17]
    cr = ConvRefs(*conv_in, conv_out, *rest[17:20])
    h = pl.program_id(1)
    i = pl.program_id(2)
    contract_last = (((1,), (1,)), ((), ()))

    step = (pl.program_id(0) * pl.num_programs(1) + h) * pl.num_programs(2) + i
    seq_pos = step % conv_tiles_per_seq
    _conv_glu(cr, seq_pos == 0, seq_pos == conv_tiles_per_seq - 1)
    conv_chunks = CONV_TS // CONV_RC

    qx_ref[...] = _sub_head_norm(q_ref[...].astype(F32), qg_ref[...], q_scale).astype(BF16)
    acc_ref[...] = jnp.zeros_like(acc_ref)

    nsub = tq // tk

    def shifted_scores(j, t):
        rows = pl.ds(pl.multiple_of(j * tk, tk), tk)
        s = lax.dot_general(qx_ref[:, t * DH:(t + 1) * DH], k_ref[rows, t * DH:(t + 1) * DH],
                            contract_last, preferred_element_type=F32)
        parts = []
        for a in range(nsub):
            diag = jnp.clip(j - (i * nsub + a), -FAR, FAR) + FAR
            parts.append(s[a * tk:(a + 1) * tk] + bias_ref[0, diag])
        return jnp.concatenate(parts, axis=0)

    def bounded():
        ls_ref[...] = jnp.zeros_like(ls_ref)
        trips = nkv // unroll
        chunks_per_trip = conv_chunks // trips
        _conv_phase(cr, _conv_shift, conv_chunks + 1)
        pieces = [(cc, piece) for cc in range(chunks_per_trip) for piece in range(CONV_PIECES)]
        per_softmax = len(pieces) // (2 * unroll)
        assert per_softmax * 2 * unroll == len(pieces)

        def body(jj, carry):
            for u in range(unroll):
                j = unroll * jj + u
                vv = v_ref[pl.ds(pl.multiple_of(j * tk, tk), tk), :]
                for t in range(2):
                    p = jnp.exp2(shifted_scores(j, t))
                    lane_sums = p[:, :DH]
                    for c in range(1, tk // DH):
                        lane_sums = lane_sums + p[:, c * DH:(c + 1) * DH]
                    at = (2 * u + t) * per_softmax
                    for n, (cc, piece) in enumerate(pieces[at:at + per_softmax]):
                        r0 = n * (tq // per_softmax)
                        _conv_tap_piece(cr, jj * chunks_per_trip + cc, piece,
                                        _zero_token(lane_sums[r0:r0 + SUBLANES]))
                    ls_ref[t] = ls_ref[t] + lane_sums
                    acc_ref[t] = acc_ref[t] + jnp.dot(p.astype(BF16), vv,
                                                      preferred_element_type=F32)
            return carry

        lax.fori_loop(0, trips, body, 0)
        _conv_phase(cr, _conv_norm, conv_chunks, unroll=2)
        for t in range(2):
            l_ref[t] = jnp.sum(ls_ref[t], axis=-1, keepdims=True)

    def running_max():
        _conv_phase(cr, _conv_shift, conv_chunks + 1)
        _conv_phase(cr, _conv_taps, conv_chunks)
        _conv_phase(cr, _conv_norm, conv_chunks, unroll=2)

        def body(j, ml):
            vv = v_ref[pl.ds(pl.multiple_of(j * tk, tk), tk), :]
            out = []
            for t in range(2):
                m, l = ml[2 * t], ml[2 * t + 1]
                s = shifted_scores(j, t)
                mn = jnp.maximum(m, jnp.max(s, axis=-1, keepdims=True))
                p = jnp.exp2(s - mn)
                a = jnp.exp2(m - mn)
                l = a * l + jnp.sum(p, axis=-1, keepdims=True)
                acc_ref[t] = a * acc_ref[t] + jnp.dot(p.astype(BF16), vv,
                                                      preferred_element_type=F32)
                out += [mn, l]
            return tuple(out)

        neg = jnp.full((tq, 1), -jnp.inf, F32)
        zero = jnp.zeros((tq, 1), F32)
        _, l1, _, l2 = lax.fori_loop(0, nkv, body, (neg, zero, neg, zero))
        l_ref[0] = l1
        l_ref[1] = l2

    lax.cond(range_ref[h] <= EXP2_SAFE_RANGE, bounded, running_max)

    lam = (jnp.exp(jnp.sum(lq1_ref[...] * lk1_ref[...], axis=-1, keepdims=True))
           - jnp.exp(jnp.sum(lq2_ref[...] * lk2_ref[...], axis=-1, keepdims=True)) + lam_init)
    o = acc_ref[0] * (1.0 / l_ref[0]) - lam * (acc_ref[1] * (1.0 / l_ref[1]))
    ms = jnp.mean(o * o, axis=-1, keepdims=True)
    y = o * lax.rsqrt(ms + EPS) * sg_ref[...] * (1.0 - lam_init)
    o_ref[...] = (y * _silu(gate_ref[...].astype(F32))).astype(BF16)


def _attention_and_conv(exp_range, p, kx, bias_tiles, q_gain, lq1, lk1, lq2, lk2, subln_g, conv_w,
                        conv_b, conv_ln_g, conv_ln_b, *, batch, seq, tq, tk, lam_init, q_scale):
    n = p.shape[0]
    nq = seq // tq
    nd = bias_tiles.shape[1]
    nkv = seq // tk
    unroll = 4
    trips = nkv // unroll
    conv_chunks = CONV_TS // CONV_RC
    assert tq % tk == 0 and bias_tiles.shape[2:] == (tk, tk) and nkv % unroll == 0
    assert batch * N_HEADS * nq * CONV_TS == n and conv_chunks % trips == 0
    assert seq % CONV_TS == 0 and (CONV_TS + 2 * HALO) % CONV_RC == 0
    per_head = W_HALF // DV

    def q_rows(seg):
        return pl.BlockSpec((tq, DV), lambda b, h, i: (b * nq + i, seg * per_head + h))

    vec = pl.BlockSpec((1, DH), lambda b, h, i: (0, 0))
    conv_in, conv_out, conv_scratch = _conv_specs(lambda b, h, i: (b * N_HEADS + h) * nq + i, n)
    kern = functools.partial(_attn_kernel, tq=tq, tk=tk, nkv=nkv, lam_init=lam_init,
                             q_scale=q_scale, unroll=unroll, conv_tiles_per_seq=seq // CONV_TS)
    return pl.pallas_call(
        kern,
        grid=(batch, N_HEADS, nq),
        in_specs=[
            pl.BlockSpec(memory_space=pltpu.SMEM),
            q_rows(EVEN_Q),
            pl.BlockSpec((seq, DV), lambda b, h, i: (b, h)),
            pl.BlockSpec((seq, DV), lambda b, h, i: (b, EVEN_V * per_head + h)),
            pl.BlockSpec((1, nd, tk, tk), lambda b, h, i: (h, 0, 0, 0),
                         pipeline_mode=pl.Buffered(1)),
            q_rows(EVEN_B_GATE),
            vec, vec, vec, vec, vec,
            pl.BlockSpec((1, DV), lambda b, h, i: (0, 0)),
        ] + conv_in,
        out_specs=[pl.BlockSpec((tq, DV), lambda b, h, i: (b * nq + i, h)), conv_out],
        out_shape=[jax.ShapeDtypeStruct((n, N_HEADS * DV), BF16),
                   jax.ShapeDtypeStruct((n, W_HALF), BF16)],
        scratch_shapes=[
            pltpu.VMEM((tq, DV), BF16),
            pltpu.VMEM((2, tq, DV), F32),
            pltpu.VMEM((2, tq, DH), F32),
            pltpu.VMEM((2, tq, 1), F32),
        ] + conv_scratch,
        compiler_params=_cparams(("parallel", "parallel", "parallel")),
        name="diff_attention",
    )(exp_range, p, kx, p, bias_tiles, p, q_gain, lq1, lk1, lq2, lk2, subln_g,
      p, p, p, p, p, p, p, conv_w, conv_b, conv_ln_g, conv_ln_b)


ROWS_BF16 = 16


def _fnet_constants(seq, ch):
    r = seq // FFT_INNER
    two_pi = 2.0 * np.pi
    ang_r = two_pi * np.outer(np.arange(r), np.arange(r)) / r
    base = np.stack([np.cos(ang_r), -np.sin(ang_r)], axis=1).reshape(2 * r, r)
    w1 = np.kron(base, np.eye(ROWS_BF16))
    ang_tw = two_pi * np.outer(np.arange(FFT_INNER), np.arange(r)) / seq
    tw = np.stack([np.cos(ang_tw), np.sin(ang_tw)])
    tw = tw.reshape(2, FFT_INNER, r // ROWS_BF16, ROWS_BF16).transpose(2, 0, 1, 3)
    ang_i = two_pi * np.outer(np.arange(FFT_INNER), np.arange(FFT_INNER)) / FFT_INNER
    c, s = np.cos(ang_i), np.sin(ang_i)
    w2 = np.block([[c, s], [-s, c]])
    scale = 1.0 / math.sqrt(seq * GROUP_C)
    eye_g = np.eye(ch // GROUP_C)
    bdc = np.kron(eye_g, c * scale)
    bds = np.kron(eye_g, s * scale)
    perm = np.zeros((ROWS_BF16 * ROWS_BF16,) * 2)
    for b in range(ROWS_BF16):
        for f in range(ROWS_BF16):
            perm[b * ROWS_BF16 + f, f * ROWS_BF16 + b] = 1.0
    as_bf16 = lambda a: jnp.asarray(a, dtype=BF16)
    return (as_bf16(w1), jnp.asarray(tw, dtype=F32), as_bf16(w2), as_bf16(bdc), as_bf16(bds),
            as_bf16(perm))


def _fft1_kernel(x_ref, w_ref, o_ref):
    _, r, rows, c = x_ref.shape
    x = x_ref[0].reshape(r * rows, c)
    y = jnp.dot(w_ref[...], x, preferred_element_type=F32).astype(BF16)
    o_ref[0] = y.reshape(r, 2, rows, c)


def _fft2_kernel(a_ref, tw_ref, w2_ref, bdc_ref, bds_ref, perm_ref, gate_ref, o_ref, u_ref, y_ref):
    nf = a_ref.shape[1]
    ch = a_ref.shape[-1]
    for kk in range(nf):
        br = a_ref[0, kk, 0].astype(F32)
        bi = a_ref[0, kk, 1].astype(F32)
        cw = tw_ref[0, 0][:, kk:kk + 1]
        sw = tw_ref[0, 1][:, kk:kk + 1]
        x = jnp.concatenate([br * cw + bi * sw, bi * cw - br * sw], axis=0).astype(BF16)
        u = jnp.dot(w2_ref[...], x, preferred_element_type=F32).astype(BF16)
        u_ref[0, kk * FFT_INNER:(kk + 1) * FFT_INNER, :] = u[:FFT_INNER]
        u_ref[1, kk * FFT_INNER:(kk + 1) * FFT_INNER, :] = u[FFT_INNER:]
    y_ref[...] = (jnp.dot(u_ref[0], bdc_ref[...], preferred_element_type=F32)
                  + jnp.dot(u_ref[1], bds_ref[...], preferred_element_type=F32)).astype(BF16)
    for a in range(FFT_INNER // ROWS_BF16):
        lo = a * ROWS_BF16
        piece = jnp.concatenate(
            [y_ref[k * FFT_INNER + lo:k * FFT_INNER + lo + ROWS_BF16, :] for k in range(nf)], axis=0)
        z = jnp.dot(perm_ref[...], piece, preferred_element_type=F32)
        g = gate_ref[0, lo:lo + ROWS_BF16].reshape(ROWS_BF16 * nf, ch).astype(F32)
        o_ref[0, lo:lo + ROWS_BF16] = (z * _silu(g)).astype(BF16).reshape(ROWS_BF16, nf, ch)


def _fnet(p, *, batch, seq, ch=512):
    c = W_HALF
    r = seq // FFT_INNER
    nf = ROWS_BF16
    assert r % nf == 0 and c % ch == 0
    w1, tw, w2, bdc, bds, perm = _fnet_constants(seq, ch)
    const = pl.Buffered(1)
    stage1 = pl.pallas_call(
        _fft1_kernel,
        grid=(batch, FFT_INNER // nf),
        in_specs=[
            pl.BlockSpec((1, r, nf, c), lambda b, t: (b, 0, t, ODD_C_IN)),
            pl.BlockSpec((2 * r * nf, r * nf), lambda b, t: (0, 0), pipeline_mode=const),
        ],
        out_specs=pl.BlockSpec((1, r, 2, nf, c), lambda b, t: (b, 0, 0, t, 0)),
        out_shape=jax.ShapeDtypeStruct((batch, r, 2, FFT_INNER, c), BF16),
        compiler_params=_cparams(("parallel", "parallel")),
        name="fft_stage1",
    )(p.reshape(batch, r, FFT_INNER, p.shape[1]), w1)
    gate_cols = ODD_C_GATE * (c // ch)
    cmat = lambda shape: pl.BlockSpec(shape, lambda b, f, j: (0,) * len(shape), pipeline_mode=const)
    out = pl.pallas_call(
        _fft2_kernel,
        grid=(batch, r // nf, c // ch),
        in_specs=[
            pl.BlockSpec((1, nf, 2, FFT_INNER, ch), lambda b, f, j: (b, f, 0, 0, j)),
            pl.BlockSpec((1, 2, FFT_INNER, nf), lambda b, f, j: (f, 0, 0, 0)),
            cmat((2 * FFT_INNER, 2 * FFT_INNER)), cmat((ch, ch)), cmat((ch, ch)),
            cmat((nf * nf, nf * nf)),
            pl.BlockSpec((1, FFT_INNER, nf, ch), lambda b, f, j: (b, 0, f, gate_cols + j)),
        ],
        out_specs=pl.BlockSpec((1, FFT_INNER, nf, ch), lambda b, f, j: (b, 0, f, j)),
        out_shape=jax.ShapeDtypeStruct((batch, FFT_INNER, r, c), BF16),
        scratch_shapes=[pltpu.VMEM((2, nf * FFT_INNER, ch), BF16),
                        pltpu.VMEM((nf * FFT_INNER, ch), BF16)],
        compiler_params=_cparams(("parallel", "parallel", "parallel")),
        name="fft_stage2",
    )(stage1, tw, w2, bdc, bds, perm, p.reshape(batch, FFT_INNER, r, p.shape[1]))
    return out.reshape(batch * seq, c)


def _sgu_kernel(u_ref, v_ref, gate_ref, lg_ref, lb_ref, ws_ref, bt_ref, o_ref, *, tr):
    v = v_ref[...].astype(F32)
    mu = jnp.mean(v, axis=-1, keepdims=True)
    vc = v - mu
    var = jnp.mean(vc * vc, axis=-1, keepdims=True)
    vn = (vc * lax.rsqrt(var + EPS) * lg_ref[...] + lb_ref[...]).astype(BF16)
    for g in range(SGU_GROUPS):
        cols = slice(g * SGU_DG, (g + 1) * SGU_DG)
        bcol = jnp.broadcast_to(bt_ref[:, g:g + 1], (SGU_CHUNK, SGU_DG))
        for n in range(tr // SGU_CHUNK):
            rows = slice(n * SGU_CHUNK, (n + 1) * SGU_CHUNK)
            sv = jnp.dot(ws_ref[g], vn[rows, cols], preferred_element_type=F32) + bcol
            out = u_ref[rows, cols].astype(F32) * sv * _silu(gate_ref[rows, cols].astype(F32))
            o_ref[rows, cols] = out.astype(BF16)


def _sgu_inputs(p, ln_g, ln_b, ws_bf16, b_t, *, tr):
    c = W_HALF

    def seg(s):
        return pl.BlockSpec((tr, c), lambda i: (i, s))

    vec = pl.BlockSpec((1, c), lambda i: (0, 0))
    specs = [seg(ODD_U), seg(ODD_V), seg(ODD_D_GATE), vec, vec,
             pl.BlockSpec((SGU_GROUPS, SGU_CHUNK, SGU_CHUNK), lambda i: (0, 0, 0)),
             pl.BlockSpec((SGU_CHUNK, SGU_GROUPS), lambda i: (0, 0))]
    return [p, p, p, ln_g, ln_b, ws_bf16, b_t], specs


QK_SCALE_LOG2 = LOG2E / math.sqrt(DH)


def _attention_bias(p, *, tk):
    return _bias_tiles(p["rel_bias"], p["q_norm_g"], p["k_norm_g"], tk=tk, q_scale=QK_SCALE_LOG2)


def _trunk(x, p, bias_tiles, exp_range, *, tq, tk):
    batch, seq, d = x.shape
    x2d = x.reshape(batch * seq, d)
    row = lambda a: a.reshape(1, -1)

    lam_init = 0.8 - 0.6 * math.exp(-0.3 * 0)
    pe = _inproj(x2d, row(p["norm_g"][0]), p["w_in_even"], tm=1024, tn=1792)
    kx = _knorm(pe, row(p["k_norm_g"]), tr=1024)
    mix_b, mix_a = _attention_and_conv(
        exp_range, pe, kx, bias_tiles, row(p["q_norm_g"]), row(p["lam_q1"]), row(p["lam_k1"]),
        row(p["lam_q2"]), row(p["lam_k2"]), row(p["subln_g"]), p["conv_w"], row(p["conv_b"]),
        row(p["conv_ln_g"]), row(p["conv_ln_b"]), batch=batch, seq=seq, tq=tq, tk=tk,
        lam_init=lam_init, q_scale=QK_SCALE_LOG2)
    x1, h1 = _outproj(mix_a, mix_b, p["w_out_even"], x2d, tm=512,
                      next_norm_gain=row(p["norm_g"][1]))

    po = _inproj_normalised(h1, p["w_in_odd"], tm=1024, tn=1280)
    mix_c = _fnet(po, batch=batch, seq=seq)
    sgu = _sgu_inputs(po, row(p["sgu_ln_g"]), row(p["sgu_ln_b"]), p["sgu_w"], p["sgu_b"].T, tr=512)
    y = _outproj(mix_c, None, p["w_out_odd"], x1, tm=512, sgu=sgu)
    return y.reshape(batch, seq, d)


def kernel(x_prompt, x_sample, norm_g, w_in_even, conv_w, conv_b, conv_ln_g, conv_ln_b,
           q_norm_g, k_norm_g, lam_q1, lam_k1, lam_q2, lam_k2, subln_g, rel_bias, w_out_even,
           w_in_odd, sgu_ln_g, sgu_ln_b, sgu_w, sgu_b, w_out_odd):
    p = dict(
        norm_g=norm_g, w_in_even=w_in_even[0].astype(BF16), conv_w=conv_w[0], conv_b=conv_b[0],
        conv_ln_g=conv_ln_g[0], conv_ln_b=conv_ln_b[0], q_norm_g=q_norm_g[0],
        k_norm_g=k_norm_g[0], lam_q1=lam_q1[0], lam_k1=lam_k1[0], lam_q2=lam_q2[0],
        lam_k2=lam_k2[0], subln_g=subln_g[0], w_out_even=w_out_even[0],
        w_in_odd=w_in_odd[0], sgu_ln_g=sgu_ln_g[0], sgu_ln_b=sgu_ln_b[0],
        sgu_w=sgu_w[0].astype(BF16), sgu_b=sgu_b[0], w_out_odd=w_out_odd[0],
        rel_bias=rel_bias)
    tq, tk = 1024, 512
    bias_tiles, exp_range = _attention_bias(p, tk=tk)
    y_prompt = _trunk(x_prompt, p, bias_tiles, exp_range, tq=tq, tk=tk)
    y_sample = _trunk(x_sample, p, bias_tiles, exp_range, tq=tq, tk=tk)
    return (y_prompt, y_sample)
```

```python
import functools
import math

import numpy as np
import jax
import jax.numpy as jnp
from jax import lax
from jax.experimental import pallas as pl
from jax.experimental.pallas import tpu as pltpu

F32 = jnp.float32
BF16 = jnp.bfloat16

EPS = 1e-6
LOG2E = math.log2(math.e)

D_MODEL = 2048
W_HALF = D_MODEL // 2
DH = 128
N_HEADS = 4
DV = 2 * DH
CONV_W = 31
CONV_PAD = CONV_W // 2
N_BUCKETS = 32
MAX_DIST = 128
FFT_INNER = 128
GROUP_C = 128
SGU_GROUPS = 4
SGU_CHUNK = 128
SGU_DG = W_HALF // SGU_GROUPS

EVEN_A_LO, EVEN_A_HI, EVEN_A_GATE, EVEN_Q, EVEN_K, EVEN_V, EVEN_B_GATE = range(7)
ODD_C_IN, ODD_C_GATE, ODD_U, ODD_V, ODD_D_GATE = range(5)

VMEM_LIMIT_V7X = 56 * 1024 * 1024
HALO = 16
SUBLANES = 8

INPROJ_TM = 1024
INPROJ_TN_EVEN = 1792
INPROJ_TN_ODD = 1280
OUTPROJ_TM = 512
KNORM_TR = 1024
ATTN_TQ = 1024
ATTN_TK = 512


def _cparams(sem):
    return pltpu.CompilerParams(dimension_semantics=sem, vmem_limit_bytes=VMEM_LIMIT_V7X)


def _sigmoid(x):
    return 0.5 * jnp.tanh(0.5 * x) + 0.5


def _silu(x):
    return x * _sigmoid(x)


def _inproj_kernel(x_ref, g_ref, w_ref, o_ref, h_ref):
    @pl.when(pl.program_id(1) == 0)
    def _():
        x = x_ref[...]
        ms = jnp.mean(x * x, axis=-1, keepdims=True)
        h_ref[...] = (x * lax.rsqrt(ms + EPS) * g_ref[...]).astype(BF16)

    o_ref[...] = jnp.dot(h_ref[...], w_ref[...], preferred_element_type=F32).astype(BF16)


def _inproj(x2d, g, w_bf16, *, tm, tn):
    n, d = x2d.shape
    p = w_bf16.shape[1]
    return pl.pallas_call(
        _inproj_kernel,
        grid=(n // tm, p // tn),
        in_specs=[
            pl.BlockSpec((tm, d), lambda i, j: (i, 0)),
            pl.BlockSpec((1, d), lambda i, j: (0, 0)),
            pl.BlockSpec((d, tn), lambda i, j: (0, j)),
        ],
        out_specs=pl.BlockSpec((tm, tn), lambda i, j: (i, j)),
        out_shape=jax.ShapeDtypeStruct((n, p), BF16),
        scratch_shapes=[pltpu.VMEM((tm, d), BF16)],
        compiler_params=_cparams(("parallel", "arbitrary")),
        name="inproj",
    )(x2d, g, w_bf16)


def _matmul_kernel(h_ref, w_ref, o_ref, wb_ref):
    @pl.when(pl.program_id(1) == 0)
    def _():
        wb_ref[...] = w_ref[...].astype(BF16)

    o_ref[...] = jnp.dot(h_ref[...], wb_ref[...], preferred_element_type=F32).astype(BF16)


def _inproj_normalised(h, w_f32, *, tm, tn):
    n, d = h.shape
    p = w_f32.shape[1]
    return pl.pallas_call(
        _matmul_kernel,
        grid=(p // tn, n // tm),
        in_specs=[
            pl.BlockSpec((tm, d), lambda j, i: (i, 0)),
            pl.BlockSpec((d, tn), lambda j, i: (0, j)),
        ],
        out_specs=pl.BlockSpec((tm, tn), lambda j, i: (i, j)),
        out_shape=jax.ShapeDtypeStruct((n, p), BF16),
        scratch_shapes=[pltpu.VMEM((d, tn), BF16)],
        compiler_params=_cparams(("arbitrary", "arbitrary")),
        name="inproj",
    )(h, w_f32)


def _outproj_kernel(ma_ref, *rest, n_sgu_inputs, emit_norm):
    rest = list(rest)
    second = [rest.pop(0) for _ in range(n_sgu_inputs or 1)]
    wa_ref, wb_ref, x_ref = rest[:3]
    rest = rest[3:]
    gain_ref = rest.pop(0) if emit_norm else None
    o_ref = rest.pop(0)
    h_ref = rest.pop(0) if emit_norm else None
    w_ref = rest.pop(0)
    mb_ref = rest.pop(0) if n_sgu_inputs else second[0]

    @pl.when(pl.program_id(0) == 0)
    def _():
        w_ref[0] = wa_ref[...].astype(BF16)
        w_ref[1] = wb_ref[...].astype(BF16)

    acc = jnp.dot(ma_ref[...], w_ref[0], preferred_element_type=F32)
    if n_sgu_inputs:
        _sgu_kernel(*second, mb_ref, tr=mb_ref.shape[0])
    acc = acc + jnp.dot(mb_ref[...], w_ref[1], preferred_element_type=F32)
    y = x_ref[...] + acc
    o_ref[...] = y
    if emit_norm:
        ms = jnp.mean(y * y, axis=-1, keepdims=True)
        h_ref[...] = (y * lax.rsqrt(ms + EPS) * gain_ref[...]).astype(BF16)


def _outproj(mix_a, mix_b, w_f32, x2d, *, tm, sgu=None, next_norm_gain=None):
    n, d = x2d.shape
    half = mix_a.shape[1]
    const = pl.Buffered(1)
    rows = pl.BlockSpec((tm, half), lambda i: (i, 0))
    full = pl.BlockSpec((tm, d), lambda i: (i, 0))
    emit_norm = next_norm_gain is not None
    scratch = [pltpu.VMEM((2, half, d), BF16)]
    if sgu is None:
        second, second_specs = [mix_b], [rows]
    else:
        second, second_specs = sgu
        scratch.append(pltpu.VMEM((tm, half), BF16))
    gain, gain_spec = ([next_norm_gain], [pl.BlockSpec((1, d), lambda i: (0, 0))]) if emit_norm \
        else ([], [])
    out = pl.pallas_call(
        functools.partial(_outproj_kernel, n_sgu_inputs=0 if sgu is None else len(second),
                          emit_norm=emit_norm),
        grid=(n // tm,),
        in_specs=[rows] + second_specs + [
            pl.BlockSpec((half, d), lambda i: (0, 0), pipeline_mode=const),
            pl.BlockSpec((half, d), lambda i: (1, 0), pipeline_mode=const),
            full,
        ] + gain_spec,
        out_specs=[full] + ([full] if emit_norm else []),
        out_shape=[jax.ShapeDtypeStruct((n, d), F32)]
        + ([jax.ShapeDtypeStruct((n, d), BF16)] if emit_norm else []),
        scratch_shapes=scratch,
        compiler_params=_cparams(("arbitrary",)),
        name="outproj",
    )(mix_a, *second, w_f32, w_f32, x2d, *gain)
    return out if emit_norm else out[0]


CONV_TS = 256
CONV_RC = 32
CONV_LANE_PARTS = 2


class ConvRefs:
    def __init__(self, lo, hi, plo, phi, nlo, nhi, gate, w, b, lg, lb, out, xs, wb, y):
        self.lo, self.hi, self.plo, self.phi, self.nlo, self.nhi = lo, hi, plo, phi, nlo, nhi
        self.gate, self.w, self.b, self.lg, self.lb, self.out = gate, w, b, lg, lb, out
        self.xs, self.wb, self.y = xs, wb, y


def _conv_glu(cr, first, last):
    ts, c = cr.y.shape
    padded = ts + 2 * HALO

    def glu(a, b):
        return a.astype(F32) * _sigmoid(b.astype(F32))

    cr.xs[0, HALO:HALO + ts, :] = glu(cr.lo[...], cr.hi[...])
    cr.xs[0, 0:HALO, :] = jnp.where(first, 0.0, glu(cr.plo[...], cr.phi[...]))
    cr.xs[0, HALO + ts:padded, :] = jnp.where(last, 0.0, glu(cr.nlo[...], cr.nhi[...]))
    cr.xs[0, padded:padded + SUBLANES, :] = jnp.zeros((SUBLANES, c), F32)
    for k in range(CONV_W):
        cr.wb[k] = jnp.broadcast_to(cr.w[k:k + 1, :], (SUBLANES, c))


def _conv_shift(cr, q):
    c = cr.y.shape[1]
    lw = c // CONV_LANE_PARTS
    row = pl.multiple_of(q * CONV_RC, CONV_RC)
    for col in range(0, c, lw):
        win = cr.xs[0, pl.ds(row, CONV_RC + SUBLANES), col:col + lw]
        for r in range(1, SUBLANES):
            cr.xs[r, pl.ds(row, CONV_RC), col:col + lw] = pltpu.roll(
                win, CONV_RC + SUBLANES - r, axis=0)[:CONV_RC]


CONV_GROUPS = CONV_RC // SUBLANES
CONV_GROUP_PARTS = 2
CONV_PIECES = CONV_LANE_PARTS * CONV_GROUP_PARTS


def _zero_token(x):
    bits = pltpu.bitcast(x[:SUBLANES, :DH], jnp.uint32)
    return pltpu.bitcast(lax.shift_right_logical(bits, jnp.uint32(32)), F32)


def _conv_tap_piece(cr, q, piece, token=None):
    c = cr.y.shape[1]
    lw = c // CONV_LANE_PARTS
    base0 = HALO - CONV_PAD
    part, gpart = divmod(piece, CONV_GROUP_PARTS)
    per = CONV_GROUPS // CONV_GROUP_PARTS
    groups = range(gpart * per, (gpart + 1) * per)
    col = part * lw
    row = pl.multiple_of(q * CONV_RC, CONV_RC)
    start = jnp.zeros((SUBLANES, lw), F32) if token is None else jnp.concatenate(
        [token] * (lw // DH), axis=-1)
    acc = {g: start for g in groups}
    for r in range(SUBLANES):
        taps = [(a, SUBLANES * a + r - base0) for a in range((base0 + CONV_W - 1) // SUBLANES + 1)
                if 0 <= SUBLANES * a + r - base0 < CONV_W]
        x = {s: cr.xs[r, pl.ds(row + SUBLANES * s, SUBLANES), col:col + lw]
             for s in sorted({a + g for a, _ in taps for g in groups})}
        for a, k in taps:
            wk = cr.wb[k, :, col:col + lw]
            for g in groups:
                acc[g] = acc[g] + x[a + g] * wk
    bias = cr.b[:, col:col + lw]
    for g in groups:
        cr.y[pl.ds(row + SUBLANES * g, SUBLANES), col:col + lw] = acc[g] + bias


def _conv_taps(cr, q):
    for piece in range(CONV_PIECES):
        _conv_tap_piece(cr, q, piece)


def _conv_norm(cr, q):
    row = pl.multiple_of(q * CONV_RC, CONV_RC)
    y = cr.y[pl.ds(row, CONV_RC), :]
    mu = jnp.mean(y, axis=-1, keepdims=True)
    yc = y - mu
    var = jnp.mean(yc * yc, axis=-1, keepdims=True)
    z = yc * lax.rsqrt(var + EPS) * cr.lg[...] + cr.lb[...]
    out = _silu(z) * _silu(cr.gate[pl.ds(row, CONV_RC), :].astype(F32))
    cr.out[pl.ds(row, CONV_RC), :] = out.astype(BF16)


def _conv_phase(cr, chunk_fn, n_chunks, unroll=1):
    def body(q, carry):
        chunk_fn(cr, q)
        return carry
    lax.fori_loop(0, n_chunks, body, 0, unroll=unroll)


def _conv_specs(index_of_step, n_rows):
    c = W_HALF
    hb = CONV_TS // HALO
    nhb = n_rows // HALO

    def rows(col):
        return pl.BlockSpec((CONV_TS, c), lambda *g: (index_of_step(*g), col))

    def prev(col):
        return pl.BlockSpec((HALO, c), lambda *g: (jnp.maximum(index_of_step(*g) * hb - 1, 0), col))

    def nxt(col):
        return pl.BlockSpec(
            (HALO, c), lambda *g: (jnp.minimum((index_of_step(*g) + 1) * hb, nhb - 1), col))

    vec = pl.BlockSpec((1, c), lambda *g: (0, 0))
    in_specs = [rows(EVEN_A_LO), rows(EVEN_A_HI), prev(EVEN_A_LO), prev(EVEN_A_HI),
                nxt(EVEN_A_LO), nxt(EVEN_A_HI), rows(EVEN_A_GATE),
                pl.BlockSpec((CONV_W, c), lambda *g: (0, 0)), vec, vec, vec]
    out_spec = pl.BlockSpec((CONV_TS, c), lambda *g: (index_of_step(*g), 0))
    scratch = [pltpu.VMEM((SUBLANES, CONV_TS + 2 * HALO + SUBLANES, c), F32),
               pltpu.VMEM((CONV_W, SUBLANES, c), F32),
               pltpu.VMEM((CONV_TS, c), F32)]
    return in_specs, out_spec, scratch


def _bucket_thresholds():
    nb = N_BUCKETS // 2
    max_exact = nb // 2
    n = np.arange(1, 4 * MAX_DIST, dtype=np.float64)
    large = max_exact + (np.log(n / max_exact) / math.log(MAX_DIST / max_exact)
                         * (nb - max_exact)).astype(np.int64)
    large = np.minimum(large, nb - 1)
    thr = [int(n[np.argmax(large >= b)]) for b in range(max_exact + 1, nb)]
    return max_exact, thr


def _bias_kernel(rb_ref, qg_ref, kg_ref, o_ref, range_ref, *, tk, lo_diag, q_scale):
    h = pl.program_id(0)
    d = pl.program_id(1) + lo_diag
    gq = jnp.abs(qg_ref[0])
    gk = jnp.abs(kg_ref[0])
    for c in range(1, DH):
        gq = jnp.maximum(gq, jnp.abs(qg_ref[c]))
        gk = jnp.maximum(gk, jnp.abs(kg_ref[c]))
    qk_bound = gq * gk * (DH * q_scale * BF16_SLACK)
    bmax = rb_ref[0, h]
    bmin = rb_ref[0, h]
    for b in range(1, N_BUCKETS):
        bmax = jnp.maximum(bmax, rb_ref[b, h])
        bmin = jnp.minimum(bmin, rb_ref[b, h])
    shift = qk_bound + bmax * LOG2E
    range_ref[h] = 2.0 * qk_bound + (bmax - bmin) * LOG2E

    @pl.when(jnp.abs(d) < FAR)
    def _():
        row = lax.broadcasted_iota(jnp.int32, (tk, tk), 0)
        col = lax.broadcasted_iota(jnp.int32, (tk, tk), 1)
        rel = col - row + d * tk
        n = jnp.abs(rel)
        max_exact, thr = _bucket_thresholds()
        bucket = jnp.minimum(n, max_exact)
        for t in thr:
            bucket = bucket + jnp.where(n >= t, 1, 0)
        bucket = bucket + jnp.where(rel > 0, N_BUCKETS // 2, 0)
        level = [rb_ref[b, h] for b in range(N_BUCKETS)]
        bit = 1
        while len(level) > 1:
            odd = (bucket & bit) != 0
            level = [jnp.where(odd, level[i + 1], level[i]) for i in range(0, len(level), 2)]
            bit *= 2
        o_ref[0, 0] = level[0] * LOG2E - shift

    @pl.when(d <= -FAR)
    def _():
        o_ref[0, 0] = jnp.full((tk, tk), rb_ref[N_BUCKETS // 2 - 1, h] * LOG2E - shift, F32)

    @pl.when(d >= FAR)
    def _():
        o_ref[0, 0] = jnp.full((tk, tk), rb_ref[N_BUCKETS - 1, h] * LOG2E - shift, F32)


EXP2_SAFE_RANGE = 100.0
BF16_SLACK = 1.01
N_DIAG = 5
FAR = N_DIAG // 2


def _bias_tiles(rel_bias, q_gain, k_gain, *, tk, q_scale):
    assert tk + 1 >= _bucket_thresholds()[1][-1]
    nd = N_DIAG
    smem = pl.BlockSpec(memory_space=pltpu.SMEM)
    kern = functools.partial(_bias_kernel, tk=tk, lo_diag=-FAR, q_scale=q_scale)
    return pl.pallas_call(
        kern,
        grid=(N_HEADS, nd),
        in_specs=[smem, smem, smem],
        out_specs=[pl.BlockSpec((1, 1, tk, tk), lambda h, d: (h, d, 0, 0)), smem],
        out_shape=[jax.ShapeDtypeStruct((N_HEADS, nd, tk, tk), F32),
                   jax.ShapeDtypeStruct((N_HEADS,), F32)],
        compiler_params=_cparams(("arbitrary", "arbitrary")),
        name="bias_tiles",
    )(rel_bias, q_gain, k_gain)


def _sub_head_norm(x, gain, scale):
    parts = []
    for t in range(x.shape[1] // DH):
        blk = x[:, t * DH:(t + 1) * DH]
        ms = jnp.mean(blk * blk, axis=-1, keepdims=True)
        parts.append(blk * lax.rsqrt(ms + EPS) * gain * scale)
    return jnp.concatenate(parts, axis=-1)


def _knorm_kernel(k_ref, g_ref, o_ref):
    o_ref[...] = _sub_head_norm(k_ref[...].astype(F32), g_ref[...], 1.0).astype(BF16)


def _knorm(p, k_gain, *, tr):
    n = p.shape[0]
    return pl.pallas_call(
        _knorm_kernel,
        grid=(n // tr,),
        in_specs=[pl.BlockSpec((tr, W_HALF), lambda i: (i, EVEN_K)),
                  pl.BlockSpec((1, DH), lambda i: (0, 0))],
        out_specs=pl.BlockSpec((tr, W_HALF), lambda i: (i, 0)),
        out_shape=jax.ShapeDtypeStruct((n, W_HALF), BF16),
        compiler_params=_cparams(("parallel",)),
        name="knorm",
    )(p, k_gain)


def _attn_kernel(range_ref, q_ref, k_ref, v_ref, bias_ref, gate_ref, qg_ref, lq1_ref, lk1_ref,
                 lq2_ref, lk2_ref, sg_ref, *rest, tq, tk, nkv, lam_init, q_scale, unroll,
                 conv_tiles_per_seq):
    conv_in = rest[:11]
    o_ref, conv_out = rest[11:13]
    qx_ref, acc_ref, ls_ref, l_ref = rest[13:17]
    cr = ConvRefs(*conv_in, conv_out, *rest[17:20])
    h = pl.program_id(1)
    i = pl.program_id(2)
    contract_last = (((1,), (1,)), ((), ()))

    step = (pl.program_id(0) * pl.num_programs(1) + h) * pl.num_programs(2) + i
    seq_pos = step % conv_tiles_per_seq
    _conv_glu(cr, seq_pos == 0, seq_pos == conv_tiles_per_seq - 1)
    conv_chunks = CONV_TS // CONV_RC

    qx_ref[...] = _sub_head_norm(q_ref[...].astype(F32), qg_ref[...], q_scale).astype(BF16)
    acc_ref[...] = jnp.zeros_like(acc_ref)

    nsub = tq // tk

    def shifted_scores(j, t):
        rows = pl.ds(pl.multiple_of(j * tk, tk), tk)
        s = lax.dot_general(qx_ref[:, t * DH:(t + 1) * DH], k_ref[rows, t * DH:(t + 1) * DH],
                            contract_last, preferred_element_type=F32)
        parts = []
        for a in range(nsub):
            diag = jnp.clip(j - (i * nsub + a), -FAR, FAR) + FAR
            parts.append(s[a * tk:(a + 1) * tk] + bias_ref[0, diag])
        return jnp.concatenate(parts, axis=0)

    def bounded():
        ls_ref[...] = jnp.zeros_like(ls_ref)
        trips = nkv // unroll
        chunks_per_trip = conv_chunks // trips
        _conv_phase(cr, _conv_shift, conv_chunks + 1)
        pieces = [(cc, piece) for cc in range(chunks_per_trip) for piece in range(CONV_PIECES)]
        per_softmax = len(pieces) // (2 * unroll)
        assert per_softmax * 2 * unroll == len(pieces)

        def body(jj, carry):
            for u in range(unroll):
                j = unroll * jj + u
                vv = v_ref[pl.ds(pl.multiple_of(j * tk, tk), tk), :]
                for t in range(2):
                    p = jnp.exp2(shifted_scores(j, t))
                    lane_sums = p[:, :DH]
                    for c in range(1, tk // DH):
                        lane_sums = lane_sums + p[:, c * DH:(c + 1) * DH]
                    at = (2 * u + t) * per_softmax
                    for n, (cc, piece) in enumerate(pieces[at:at + per_softmax]):
                        r0 = n * (tq // per_softmax)
                        _conv_tap_piece(cr, jj * chunks_per_trip + cc, piece,
                                        _zero_token(lane_sums[r0:r0 + SUBLANES]))
                    ls_ref[t] = ls_ref[t] + lane_sums
                    acc_ref[t] = acc_ref[t] + jnp.dot(p.astype(BF16), vv,
                                                      preferred_element_type=F32)
            return carry

        lax.fori_loop(0, trips, body, 0)
        _conv_phase(cr, _conv_norm, conv_chunks, unroll=2)
        for t in range(2):
            l_ref[t] = jnp.sum(ls_ref[t], axis=-1, keepdims=True)

    def running_max():
        _conv_phase(cr, _conv_shift, conv_chunks + 1)
        _conv_phase(cr, _conv_taps, conv_chunks)
        _conv_phase(cr, _conv_norm, conv_chunks, unroll=2)

        def body(j, ml):
            vv = v_ref[pl.ds(pl.multiple_of(j * tk, tk), tk), :]
            out = []
            for t in range(2):
                m, l = ml[2 * t], ml[2 * t + 1]
                s = shifted_scores(j, t)
                mn = jnp.maximum(m, jnp.max(s, axis=-1, keepdims=True))
                p = jnp.exp2(s - mn)
                a = jnp.exp2(m - mn)
                l = a * l + jnp.sum(p, axis=-1, keepdims=True)
                acc_ref[t] = a * acc_ref[t] + jnp.dot(p.astype(BF16), vv,
                                                      preferred_element_type=F32)
                out += [mn, l]
            return tuple(out)

        neg = jnp.full((tq, 1), -jnp.inf, F32)
        zero = jnp.zeros((tq, 1), F32)
        _, l1, _, l2 = lax.fori_loop(0, nkv, body, (neg, zero, neg, zero))
        l_ref[0] = l1
        l_ref[1] = l2

    lax.cond(range_ref[h] <= EXP2_SAFE_RANGE, bounded, running_max)

    lam = (jnp.exp(jnp.sum(lq1_ref[...] * lk1_ref[...], axis=-1, keepdims=True))
           - jnp.exp(jnp.sum(lq2_ref[...] * lk2_ref[...], axis=-1, keepdims=True)) + lam_init)
    o = acc_ref[0] * (1.0 / l_ref[0]) - lam * (acc_ref[1] * (1.0 / l_ref[1]))
    ms = jnp.mean(o * o, axis=-1, keepdims=True)
    y = o * lax.rsqrt(ms + EPS) * sg_ref[...] * (1.0 - lam_init)
    o_ref[...] = (y * _silu(gate_ref[...].astype(F32))).astype(BF16)


def _attention_and_conv(exp_range, p, kx, bias_tiles, q_gain, lq1, lk1, lq2, lk2, subln_g, conv_w,
                        conv_b, conv_ln_g, conv_ln_b, *, batch, seq, tq, tk, lam_init, q_scale):
    n = p.shape[0]
    nq = seq // tq
    nd = bias_tiles.shape[1]
    nkv = seq // tk
    unroll = 4
    trips = nkv // unroll
    conv_chunks = CONV_TS // CONV_RC
    assert tq % tk == 0 and bias_tiles.shape[2:] == (tk, tk) and nkv % unroll == 0
    assert batch * N_HEADS * nq * CONV_TS == n and conv_chunks % trips == 0
    assert seq % CONV_TS == 0 and (CONV_TS + 2 * HALO) % CONV_RC == 0
    per_head = W_HALF // DV

    def q_rows(seg):
        return pl.BlockSpec((tq, DV), lambda b, h, i: (b * nq + i, seg * per_head + h))

    vec = pl.BlockSpec((1, DH), lambda b, h, i: (0, 0))
    conv_in, conv_out, conv_scratch = _conv_specs(lambda b, h, i: (b * N_HEADS + h) * nq + i, n)
    kern = functools.partial(_attn_kernel, tq=tq, tk=tk, nkv=nkv, lam_init=lam_init,
                             q_scale=q_scale, unroll=unroll, conv_tiles_per_seq=seq // CONV_TS)
    return pl.pallas_call(
        kern,
        grid=(batch, N_HEADS, nq),
        in_specs=[
            pl.BlockSpec(memory_space=pltpu.SMEM),
            q_rows(EVEN_Q),
            pl.BlockSpec((seq, DV), lambda b, h, i: (b, h)),
            pl.BlockSpec((seq, DV), lambda b, h, i: (b, EVEN_V * per_head + h)),
            pl.BlockSpec((1, nd, tk, tk), lambda b, h, i: (h, 0, 0, 0),
                         pipeline_mode=pl.Buffered(1)),
            q_rows(EVEN_B_GATE),
            vec, vec, vec, vec, vec,
            pl.BlockSpec((1, DV), lambda b, h, i: (0, 0)),
        ] + conv_in,
        out_specs=[pl.BlockSpec((tq, DV), lambda b, h, i: (b * nq + i, h)), conv_out],
        out_shape=[jax.ShapeDtypeStruct((n, N_HEADS * DV), BF16),
                   jax.ShapeDtypeStruct((n, W_HALF), BF16)],
        scratch_shapes=[
            pltpu.VMEM((tq, DV), BF16),
            pltpu.VMEM((2, tq, DV), F32),
            pltpu.VMEM((2, tq, DH), F32),
            pltpu.VMEM((2, tq, 1), F32),
        ] + conv_scratch,
        compiler_params=_cparams(("parallel", "parallel", "parallel")),
        name="diff_attention",
    )(exp_range, p, kx, p, bias_tiles, p, q_gain, lq1, lk1, lq2, lk2, subln_g,
      p, p, p, p, p, p, p, conv_w, conv_b, conv_ln_g, conv_ln_b)


ROWS_BF16 = 16


def _fnet_constants(seq, ch):
    r = seq // FFT_INNER
    two_pi = 2.0 * np.pi
    ang_r = two_pi * np.outer(np.arange(r), np.arange(r)) / r
    base = np.stack([np.cos(ang_r), -np.sin(ang_r)], axis=1).reshape(2 * r, r)
    w1 = np.kron(base, np.eye(ROWS_BF16))
    ang_tw = two_pi * np.outer(np.arange(FFT_INNER), np.arange(r)) / seq
    tw = np.stack([np.cos(ang_tw), np.sin(ang_tw)])
    tw = tw.reshape(2, FFT_INNER, r // ROWS_BF16, ROWS_BF16).transpose(2, 0, 1, 3)
    ang_i = two_pi * np.outer(np.arange(FFT_INNER), np.arange(FFT_INNER)) / FFT_INNER
    c, s = np.cos(ang_i), np.sin(ang_i)
    w2 = np.block([[c, s], [-s, c]])
    scale = 1.0 / math.sqrt(seq * GROUP_C)
    eye_g = np.eye(ch // GROUP_C)
    bdc = np.kron(eye_g, c * scale)
    bds = np.kron(eye_g, s * scale)
    perm = np.zeros((ROWS_BF16 * ROWS_BF16,) * 2)
    for b in range(ROWS_BF16):
        for f in range(ROWS_BF16):
            perm[b * ROWS_BF16 + f, f * ROWS_BF16 + b] = 1.0
    as_bf16 = lambda a: jnp.asarray(a, dtype=BF16)
    return (as_bf16(w1), jnp.asarray(tw, dtype=F32), as_bf16(w2), as_bf16(bdc), as_bf16(bds),
            as_bf16(perm))


def _fft1_kernel(x_ref, w_ref, o_ref):
    _, r, rows, c = x_ref.shape
    x = x_ref[0].reshape(r * rows, c)
    y = jnp.dot(w_ref[...], x, preferred_element_type=F32).astype(BF16)
    o_ref[0] = y.reshape(r, 2, rows, c)


def _fft2_kernel(a_ref, tw_ref, w2_ref, bdc_ref, bds_ref, perm_ref, gate_ref, o_ref, u_ref, y_ref):
    nf = a_ref.shape[1]
    ch = a_ref.shape[-1]
    for kk in range(nf):
        br = a_ref[0, kk, 0].astype(F32)
        bi = a_ref[0, kk, 1].astype(F32)
        cw = tw_ref[0, 0][:, kk:kk + 1]
        sw = tw_ref[0, 1][:, kk:kk + 1]
        x = jnp.concatenate([br * cw + bi * sw, bi * cw - br * sw], axis=0).astype(BF16)
        u = jnp.dot(w2_ref[...], x, preferred_element_type=F32).astype(BF16)
        u_ref[0, kk * FFT_INNER:(kk + 1) * FFT_INNER, :] = u[:FFT_INNER]
        u_ref[1, kk * FFT_INNER:(kk + 1) * FFT_INNER, :] = u[FFT_INNER:]
    y_ref[...] = (jnp.dot(u_ref[0], bdc_ref[...], preferred_element_type=F32)
                  + jnp.dot(u_ref[1], bds_ref[...], preferred_element_type=F32)).astype(BF16)
    for a in range(FFT_INNER // ROWS_BF16):
        lo = a * ROWS_BF16
        piece = jnp.concatenate(
            [y_ref[k * FFT_INNER + lo:k * FFT_INNER + lo + ROWS_BF16, :] for k in range(nf)], axis=0)
        z = jnp.dot(perm_ref[...], piece, preferred_element_type=F32)
        g = gate_ref[0, lo:lo + ROWS_BF16].reshape(ROWS_BF16 * nf, ch).astype(F32)
        o_ref[0, lo:lo + ROWS_BF16] = (z * _silu(g)).astype(BF16).reshape(ROWS_BF16, nf, ch)


def _fnet(p, *, batch, seq, ch=512):
    c = W_HALF
    r = seq // FFT_INNER
    nf = ROWS_BF16
    assert r % nf == 0 and c % ch == 0
    w1, tw, w2, bdc, bds, perm = _fnet_constants(seq, ch)
    const = pl.Buffered(1)
    stage1 = pl.pallas_call(
        _fft1_kernel,
        grid=(batch, FFT_INNER // nf),
        in_specs=[
            pl.BlockSpec((1, r, nf, c), lambda b, t: (b, 0, t, ODD_C_IN)),
            pl.BlockSpec((2 * r * nf, r * nf), lambda b, t: (0, 0), pipeline_mode=const),
        ],
        out_specs=pl.BlockSpec((1, r, 2, nf, c), lambda b, t: (b, 0, 0, t, 0)),
        out_shape=jax.ShapeDtypeStruct((batch, r, 2, FFT_INNER, c), BF16),
        compiler_params=_cparams(("parallel", "parallel")),
        name="fft_stage1",
    )(p.reshape(batch, r, FFT_INNER, p.shape[1]), w1)
    gate_cols = ODD_C_GATE * (c // ch)
    cmat = lambda shape: pl.BlockSpec(shape, lambda b, f, j: (0,) * len(shape), pipeline_mode=const)
    out = pl.pallas_call(
        _fft2_kernel,
        grid=(batch, r // nf, c // ch),
        in_specs=[
            pl.BlockSpec((1, nf, 2, FFT_INNER, ch), lambda b, f, j: (b, f, 0, 0, j)),
            pl.BlockSpec((1, 2, FFT_INNER, nf), lambda b, f, j: (f, 0, 0, 0)),
            cmat((2 * FFT_INNER, 2 * FFT_INNER)), cmat((ch, ch)), cmat((ch, ch)),
            cmat((nf * nf, nf * nf)),
            pl.BlockSpec((1, FFT_INNER, nf, ch), lambda b, f, j: (b, 0, f, gate_cols + j)),
        ],
        out_specs=pl.BlockSpec((1, FFT_INNER, nf, ch), lambda b, f, j: (b, 0, f, j)),
        out_shape=jax.ShapeDtypeStruct((batch, FFT_INNER, r, c), BF16),
        scratch_shapes=[pltpu.VMEM((2, nf * FFT_INNER, ch), BF16),
                        pltpu.VMEM((nf * FFT_INNER, ch), BF16)],
        compiler_params=_cparams(("parallel", "parallel", "parallel")),
        name="fft_stage2",
    )(stage1, tw, w2, bdc, bds, perm, p.reshape(batch, FFT_INNER, r, p.shape[1]))
    return out.reshape(batch * seq, c)


def _sgu_kernel(u_ref, v_ref, gate_ref, lg_ref, lb_ref, ws_ref, bt_ref, o_ref, *, tr):
    v = v_ref[...].astype(F32)
    mu = jnp.mean(v, axis=-1, keepdims=True)
    vc = v - mu
    var = jnp.mean(vc * vc, axis=-1, keepdims=True)
    vn = (vc * lax.rsqrt(var + EPS) * lg_ref[...] + lb_ref[...]).astype(BF16)
    for g in range(SGU_GROUPS):
        cols = slice(g * SGU_DG, (g + 1) * SGU_DG)
        bcol = jnp.broadcast_to(bt_ref[:, g:g + 1], (SGU_CHUNK, SGU_DG))
        for n in range(tr // SGU_CHUNK):
            rows = slice(n * SGU_CHUNK, (n + 1) * SGU_CHUNK)
            sv = jnp.dot(ws_ref[g], vn[rows, cols], preferred_element_type=F32) + bcol
            out = u_ref[rows, cols].astype(F32) * sv * _silu(gate_ref[rows, cols].astype(F32))
            o_ref[rows, cols] = out.astype(BF16)


def _sgu_inputs(p, ln_g, ln_b, ws_bf16, b_t, *, tr):
    c = W_HALF

    def seg(s):
        return pl.BlockSpec((tr, c), lambda i: (i, s))

    vec = pl.BlockSpec((1, c), lambda i: (0, 0))
    specs = [seg(ODD_U), seg(ODD_V), seg(ODD_D_GATE), vec, vec,
             pl.BlockSpec((SGU_GROUPS, SGU_CHUNK, SGU_CHUNK), lambda i: (0, 0, 0)),
             pl.BlockSpec((SGU_CHUNK, SGU_GROUPS), lambda i: (0, 0))]
    return [p, p, p, ln_g, ln_b, ws_bf16, b_t], specs


QK_SCALE_LOG2 = LOG2E / math.sqrt(DH)


def _attention_bias(p, *, tk):
    return _bias_tiles(p["rel_bias"], p["q_norm_g"], p["k_norm_g"], tk=tk, q_scale=QK_SCALE_LOG2)


def _trunk(x, p, bias_tiles, exp_range):
    batch, seq, d = x.shape
    x2d = x.reshape(batch * seq, d)
    row = lambda a: a.reshape(1, -1)

    lam_init = 0.8 - 0.6 * math.exp(-0.3 * 0)
    pe = _inproj(x2d, row(p["norm_g"][0]), p["w_in_even"], tm=INPROJ_TM, tn=INPROJ_TN_EVEN)
    kx = _knorm(pe, row(p["k_norm_g"]), tr=KNORM_TR)
    mix_b, mix_a = _attention_and_conv(
        exp_range, pe, kx, bias_tiles, row(p["q_norm_g"]), row(p["lam_q1"]), row(p["lam_k1"]),
        row(p["lam_q2"]), row(p["lam_k2"]), row(p["subln_g"]), p["conv_w"], row(p["conv_b"]),
        row(p["conv_ln_g"]), row(p["conv_ln_b"]), batch=batch, seq=seq, tq=ATTN_TQ, tk=ATTN_TK,
        lam_init=lam_init, q_scale=QK_SCALE_LOG2)
    x1, h1 = _outproj(mix_a, mix_b, p["w_out_even"], x2d, tm=OUTPROJ_TM,
                      next_norm_gain=row(p["norm_g"][1]))

    po = _inproj_normalised(h1, p["w_in_odd"], tm=INPROJ_TM, tn=INPROJ_TN_ODD)
    mix_c = _fnet(po, batch=batch, seq=seq)
    sgu = _sgu_inputs(po, row(p["sgu_ln_g"]), row(p["sgu_ln_b"]), p["sgu_w"], p["sgu_b"].T,
                      tr=OUTPROJ_TM)
    y = _outproj(mix_c, None, p["w_out_odd"], x1, tm=OUTPROJ_TM, sgu=sgu)
    return y.reshape(batch, seq, d)


def kernel(x_prompt, x_sample, norm_g, w_in_even, conv_w, conv_b, conv_ln_g, conv_ln_b,
           q_norm_g, k_norm_g, lam_q1, lam_k1, lam_q2, lam_k2, subln_g, rel_bias, w_out_even,
           w_in_odd, sgu_ln_g, sgu_ln_b, sgu_w, sgu_b, w_out_odd):
    p = dict(
        norm_g=norm_g, w_in_even=w_in_even[0].astype(BF16), conv_w=conv_w[0], conv_b=conv_b[0],
        conv_ln_g=conv_ln_g[0], conv_ln_b=conv_ln_b[0], q_norm_g=q_norm_g[0],
        k_norm_g=k_norm_g[0], lam_q1=lam_q1[0], lam_k1=lam_k1[0], lam_q2=lam_q2[0],
        lam_k2=lam_k2[0], subln_g=subln_g[0], w_out_even=w_out_even[0],
        w_in_odd=w_in_odd[0], sgu_ln_g=sgu_ln_g[0], sgu_ln_b=sgu_ln_b[0],
        sgu_w=sgu_w[0].astype(BF16), sgu_b=sgu_b[0], w_out_odd=w_out_odd[0],
        rel_bias=rel_bias)
    bias_tiles, exp_range = _attention_bias(p, tk=ATTN_TK)
    y_prompt = _trunk(x_prompt, p, bias_tiles, exp_range)
    y_sample = _trunk(x_sample, p, bias_tiles, exp_range)
    return (y_prompt, y_sample)
```

```python
import functools
import math

import numpy as np
import jax
import jax.numpy as jnp
from jax import lax
from jax.experimental import pallas as pl
from jax.experimental.pallas import tpu as pltpu

F32 = jnp.float32
BF16 = jnp.bfloat16

EPS = 1e-6
LOG2E = math.log2(math.e)

D_MODEL = 2048
W_HALF = D_MODEL // 2
DH = 128
N_HEADS = 4
DV = 2 * DH
CONV_W = 31
CONV_PAD = CONV_W // 2
N_BUCKETS = 32
MAX_DIST = 128
FFT_INNER = 128
GROUP_C = 128
SGU_GROUPS = 4
SGU_CHUNK = 128
SGU_DG = W_HALF // SGU_GROUPS

EVEN_A_LO, EVEN_A_HI, EVEN_A_GATE, EVEN_Q, EVEN_K, EVEN_V, EVEN_B_GATE = range(7)
ODD_C_IN, ODD_C_GATE, ODD_U, ODD_V, ODD_D_GATE = range(5)

VMEM_LIMIT_V7X = 56 * 1024 * 1024
HALO = 16
SUBLANES = 8

INPROJ_TM = 1024
INPROJ_TN_EVEN = 1792
INPROJ_TN_ODD = 1280
OUTPROJ_TM = 512
KNORM_TR = 1024
ATTN_TQ = 1024
ATTN_TK = 512


def _cparams(sem):
    return pltpu.CompilerParams(dimension_semantics=sem, vmem_limit_bytes=VMEM_LIMIT_V7X)


def _sigmoid(x):
    return 0.5 * jnp.tanh(0.5 * x) + 0.5


def _silu(x):
    return x * _sigmoid(x)


def _inproj_kernel(x_ref, g_ref, w_ref, o_ref, h_ref):
    @pl.when(pl.program_id(1) == 0)
    def _():
        x = x_ref[...]
        ms = jnp.mean(x * x, axis=-1, keepdims=True)
        h_ref[...] = (x * lax.rsqrt(ms + EPS) * g_ref[...]).astype(BF16)

    o_ref[...] = jnp.dot(h_ref[...], w_ref[...], preferred_element_type=F32).astype(BF16)


def _inproj(x2d, g, w_bf16, *, tm, tn):
    n, d = x2d.shape
    p = w_bf16.shape[1]
    return pl.pallas_call(
        _inproj_kernel,
        grid=(n // tm, p // tn),
        in_specs=[
            pl.BlockSpec((tm, d), lambda i, j: (i, 0)),
            pl.BlockSpec((1, d), lambda i, j: (0, 0)),
            pl.BlockSpec((d, tn), lambda i, j: (0, j)),
        ],
        out_specs=pl.BlockSpec((tm, tn), lambda i, j: (i, j)),
        out_shape=jax.ShapeDtypeStruct((n, p), BF16),
        scratch_shapes=[pltpu.VMEM((tm, d), BF16)],
        compiler_params=_cparams(("parallel", "arbitrary")),
        name="inproj",
    )(x2d, g, w_bf16)


def _matmul_kernel(h_ref, w_ref, o_ref, wb_ref):
    @pl.when(pl.program_id(1) == 0)
    def _():
        wb_ref[...] = w_ref[...].astype(BF16)

    o_ref[...] = jnp.dot(h_ref[...], wb_ref[...], preferred_element_type=F32).astype(BF16)


def _inproj_normalised(h, w_f32, *, tm, tn):
    n, d = h.shape
    p = w_f32.shape[1]
    return pl.pallas_call(
        _matmul_kernel,
        grid=(p // tn, n // tm),
        in_specs=[
            pl.BlockSpec((tm, d), lambda j, i: (i, 0)),
            pl.BlockSpec((d, tn), lambda j, i: (0, j)),
        ],
        out_specs=pl.BlockSpec((tm, tn), lambda j, i: (i, j)),
        out_shape=jax.ShapeDtypeStruct((n, p), BF16),
        scratch_shapes=[pltpu.VMEM((d, tn), BF16)],
        compiler_params=_cparams(("arbitrary", "arbitrary")),
        name="inproj",
    )(h, w_f32)


def _outproj_kernel(ma_ref, *rest, n_sgu_inputs, emit_norm):
    rest = list(rest)
    second = [rest.pop(0) for _ in range(n_sgu_inputs or 1)]
    wa_ref, wb_ref, x_ref = rest[:3]
    rest = rest[3:]
    gain_ref = rest.pop(0) if emit_norm else None
    o_ref = rest.pop(0)
    h_ref = rest.pop(0) if emit_norm else None
    w_ref = rest.pop(0)
    mb_ref = rest.pop(0) if n_sgu_inputs else second[0]

    @pl.when(pl.program_id(0) == 0)
    def _():
        w_ref[0] = wa_ref[...].astype(BF16)
        w_ref[1] = wb_ref[...].astype(BF16)

    acc = jnp.dot(ma_ref[...], w_ref[0], preferred_element_type=F32)
    if n_sgu_inputs:
        _sgu_kernel(*second, mb_ref, tr=mb_ref.shape[0])
    acc = acc + jnp.dot(mb_ref[...], w_ref[1], preferred_element_type=F32)
    y = x_ref[...] + acc
    o_ref[...] = y
    if emit_norm:
        ms = jnp.mean(y * y, axis=-1, keepdims=True)
        h_ref[...] = (y * lax.rsqrt(ms + EPS) * gain_ref[...]).astype(BF16)


def _outproj(mix_a, mix_b, w_f32, x2d, *, tm, sgu=None, next_norm_gain=None):
    n, d = x2d.shape
    half = mix_a.shape[1]
    const = pl.Buffered(1)
    rows = pl.BlockSpec((tm, half), lambda i: (i, 0))
    full = pl.BlockSpec((tm, d), lambda i: (i, 0))
    emit_norm = next_norm_gain is not None
    scratch = [pltpu.VMEM((2, half, d), BF16)]
    if sgu is None:
        second, second_specs = [mix_b], [rows]
    else:
        second, second_specs = sgu
        scratch.append(pltpu.VMEM((tm, half), BF16))
    gain, gain_spec = ([next_norm_gain], [pl.BlockSpec((1, d), lambda i: (0, 0))]) if emit_norm \
        else ([], [])
    out = pl.pallas_call(
        functools.partial(_outproj_kernel, n_sgu_inputs=0 if sgu is None else len(second),
                          emit_norm=emit_norm),
        grid=(n // tm,),
        in_specs=[rows] + second_specs + [
            pl.BlockSpec((half, d), lambda i: (0, 0), pipeline_mode=const),
            pl.BlockSpec((half, d), lambda i: (1, 0), pipeline_mode=const),
            full,
        ] + gain_spec,
        out_specs=[full] + ([full] if emit_norm else []),
        out_shape=[jax.ShapeDtypeStruct((n, d), F32)]
        + ([jax.ShapeDtypeStruct((n, d), BF16)] if emit_norm else []),
        scratch_shapes=scratch,
        compiler_params=_cparams(("arbitrary",)),
        name="outproj",
    )(mix_a, *second, w_f32, w_f32, x2d, *gain)
    return out if emit_norm else out[0]


CONV_TS = 256
CONV_RC = 32
CONV_LANE_PARTS = 4


class ConvRefs:
    def __init__(self, lo, hi, plo, phi, nlo, nhi, gate, w, b, lg, lb, out, xs, wb, y):
        self.lo, self.hi, self.plo, self.phi, self.nlo, self.nhi = lo, hi, plo, phi, nlo, nhi
        self.gate, self.w, self.b, self.lg, self.lb, self.out = gate, w, b, lg, lb, out
        self.xs, self.wb, self.y = xs, wb, y


def _conv_glu(cr, first, last):
    ts, c = cr.y.shape
    padded = ts + 2 * HALO

    def glu(a, b):
        return a.astype(F32) * _sigmoid(b.astype(F32))

    cr.xs[0, HALO:HALO + ts, :] = glu(cr.lo[...], cr.hi[...])
    cr.xs[0, 0:HALO, :] = jnp.where(first, 0.0, glu(cr.plo[...], cr.phi[...]))
    cr.xs[0, HALO + ts:padded, :] = jnp.where(last, 0.0, glu(cr.nlo[...], cr.nhi[...]))
    cr.xs[0, padded:padded + SUBLANES, :] = jnp.zeros((SUBLANES, c), F32)
    for k in range(CONV_W):
        cr.wb[k] = jnp.broadcast_to(cr.w[k:k + 1, :], (SUBLANES, c))


def _conv_shift(cr, q):
    c = cr.y.shape[1]
    lw = c // CONV_LANE_PARTS
    row = pl.multiple_of(q * CONV_RC, CONV_RC)
    for col in range(0, c, lw):
        win = cr.xs[0, pl.ds(row, CONV_RC + SUBLANES), col:col + lw]
        for r in range(1, SUBLANES):
            cr.xs[r, pl.ds(row, CONV_RC), col:col + lw] = pltpu.roll(
                win, CONV_RC + SUBLANES - r, axis=0)[:CONV_RC]


CONV_GROUPS = CONV_RC // SUBLANES
CONV_GROUP_PARTS = 1
CONV_PIECES = CONV_LANE_PARTS * CONV_GROUP_PARTS


def _zero_token(x):
    bits = pltpu.bitcast(x[:SUBLANES, :DH], jnp.uint32)
    return pltpu.bitcast(lax.shift_right_logical(bits, jnp.uint32(32)), F32)


def _conv_tap_piece(cr, q, piece, token=None):
    c = cr.y.shape[1]
    lw = c // CONV_LANE_PARTS
    base0 = HALO - CONV_PAD
    part, gpart = divmod(piece, CONV_GROUP_PARTS)
    per = CONV_GROUPS // CONV_GROUP_PARTS
    groups = range(gpart * per, (gpart + 1) * per)
    col = part * lw
    row = pl.multiple_of(q * CONV_RC, CONV_RC)
    start = jnp.zeros((SUBLANES, lw), F32) if token is None else jnp.concatenate(
        [token] * (lw // DH), axis=-1)
    acc = {g: start for g in groups}
    for r in range(SUBLANES):
        taps = [(a, SUBLANES * a + r - base0) for a in range((base0 + CONV_W - 1) // SUBLANES + 1)
                if 0 <= SUBLANES * a + r - base0 < CONV_W]
        x = {s: cr.xs[r, pl.ds(row + SUBLANES * s, SUBLANES), col:col + lw]
             for s in sorted({a + g for a, _ in taps for g in groups})}
        for a, k in taps:
            wk = cr.wb[k, :, col:col + lw]
            for g in groups:
                acc[g] = acc[g] + x[a + g] * wk
    bias = cr.b[:, col:col + lw]
    for g in groups:
        cr.y[pl.ds(row + SUBLANES * g, SUBLANES), col:col + lw] = acc[g] + bias


def _conv_taps(cr, q):
    for piece in range(CONV_PIECES):
        _conv_tap_piece(cr, q, piece)


def _conv_norm(cr, q):
    row = pl.multiple_of(q * CONV_RC, CONV_RC)
    y = cr.y[pl.ds(row, CONV_RC), :]
    mu = jnp.mean(y, axis=-1, keepdims=True)
    yc = y - mu
    var = jnp.mean(yc * yc, axis=-1, keepdims=True)
    z = yc * lax.rsqrt(var + EPS) * cr.lg[...] + cr.lb[...]
    out = _silu(z) * _silu(cr.gate[pl.ds(row, CONV_RC), :].astype(F32))
    cr.out[pl.ds(row, CONV_RC), :] = out.astype(BF16)


def _conv_phase(cr, chunk_fn, n_chunks, unroll=1):
    def body(q, carry):
        chunk_fn(cr, q)
        return carry
    lax.fori_loop(0, n_chunks, body, 0, unroll=unroll)


def _conv_specs(index_of_step, n_rows):
    c = W_HALF
    hb = CONV_TS // HALO
    nhb = n_rows // HALO

    def rows(col):
        return pl.BlockSpec((CONV_TS, c), lambda *g: (index_of_step(*g), col))

    def prev(col):
        return pl.BlockSpec((HALO, c), lambda *g: (jnp.maximum(index_of_step(*g) * hb - 1, 0), col))

    def nxt(col):
        return pl.BlockSpec(
            (HALO, c), lambda *g: (jnp.minimum((index_of_step(*g) + 1) * hb, nhb - 1), col))

    vec = pl.BlockSpec((1, c), lambda *g: (0, 0))
    in_specs = [rows(EVEN_A_LO), rows(EVEN_A_HI), prev(EVEN_A_LO), prev(EVEN_A_HI),
                nxt(EVEN_A_LO), nxt(EVEN_A_HI), rows(EVEN_A_GATE),
                pl.BlockSpec((CONV_W, c), lambda *g: (0, 0)), vec, vec, vec]
    out_spec = pl.BlockSpec((CONV_TS, c), lambda *g: (index_of_step(*g), 0))
    scratch = [pltpu.VMEM((SUBLANES, CONV_TS + 2 * HALO + SUBLANES, c), F32),
               pltpu.VMEM((CONV_W, SUBLANES, c), F32),
               pltpu.VMEM((CONV_TS, c), F32)]
    return in_specs, out_spec, scratch


def _bucket_thresholds():
    nb = N_BUCKETS // 2
    max_exact = nb // 2
    n = np.arange(1, 4 * MAX_DIST, dtype=np.float64)
    large = max_exact + (np.log(n / max_exact) / math.log(MAX_DIST / max_exact)
                         * (nb - max_exact)).astype(np.int64)
    large = np.minimum(large, nb - 1)
    thr = [int(n[np.argmax(large >= b)]) for b in range(max_exact + 1, nb)]
    return max_exact, thr


def _bias_kernel(rb_ref, qg_ref, kg_ref, o_ref, range_ref, *, tk, lo_diag, q_scale):
    h = pl.program_id(0)
    d = pl.program_id(1) + lo_diag
    gq = jnp.abs(qg_ref[0])
    gk = jnp.abs(kg_ref[0])
    for c in range(1, DH):
        gq = jnp.maximum(gq, jnp.abs(qg_ref[c]))
        gk = jnp.maximum(gk, jnp.abs(kg_ref[c]))
    qk_bound = gq * gk * (DH * q_scale * BF16_SLACK)
    bmax = rb_ref[0, h]
    bmin = rb_ref[0, h]
    for b in range(1, N_BUCKETS):
        bmax = jnp.maximum(bmax, rb_ref[b, h])
        bmin = jnp.minimum(bmin, rb_ref[b, h])
    shift = qk_bound + bmax * LOG2E
    range_ref[h] = 2.0 * qk_bound + (bmax - bmin) * LOG2E

    @pl.when(jnp.abs(d) < FAR)
    def _():
        row = lax.broadcasted_iota(jnp.int32, (tk, tk), 0)
        col = lax.broadcasted_iota(jnp.int32, (tk, tk), 1)
        rel = col - row + d * tk
        n = jnp.abs(rel)
        max_exact, thr = _bucket_thresholds()
        bucket = jnp.minimum(n, max_exact)
        for t in thr:
            bucket = bucket + jnp.where(n >= t, 1, 0)
        bucket = bucket + jnp.where(rel > 0, N_BUCKETS // 2, 0)
        level = [rb_ref[b, h] for b in range(N_BUCKETS)]
        bit = 1
        while len(level) > 1:
            odd = (bucket & bit) != 0
            level = [jnp.where(odd, level[i + 1], level[i]) for i in range(0, len(level), 2)]
            bit *= 2
        o_ref[0, 0] = level[0] * LOG2E - shift

    @pl.when(d <= -FAR)
    def _():
        o_ref[0, 0] = jnp.full((tk, tk), rb_ref[N_BUCKETS // 2 - 1, h] * LOG2E - shift, F32)

    @pl.when(d >= FAR)
    def _():
        o_ref[0, 0] = jnp.full((tk, tk), rb_ref[N_BUCKETS - 1, h] * LOG2E - shift, F32)


EXP2_SAFE_RANGE = 100.0
BF16_SLACK = 1.01
N_DIAG = 5
FAR = N_DIAG // 2


def _bias_tiles(rel_bias, q_gain, k_gain, *, tk, q_scale):
    assert tk + 1 >= _bucket_thresholds()[1][-1]
    nd = N_DIAG
    smem = pl.BlockSpec(memory_space=pltpu.SMEM)
    kern = functools.partial(_bias_kernel, tk=tk, lo_diag=-FAR, q_scale=q_scale)
    return pl.pallas_call(
        kern,
        grid=(N_HEADS, nd),
        in_specs=[smem, smem, smem],
        out_specs=[pl.BlockSpec((1, 1, tk, tk), lambda h, d: (h, d, 0, 0)), smem],
        out_shape=[jax.ShapeDtypeStruct((N_HEADS, nd, tk, tk), F32),
                   jax.ShapeDtypeStruct((N_HEADS,), F32)],
        compiler_params=_cparams(("arbitrary", "arbitrary")),
        name="bias_tiles",
    )(rel_bias, q_gain, k_gain)


def _sub_head_norm(x, gain, scale):
    parts = []
    for t in range(x.shape[1] // DH):
        blk = x[:, t * DH:(t + 1) * DH]
        ms = jnp.mean(blk * blk, axis=-1, keepdims=True)
        parts.append(blk * lax.rsqrt(ms + EPS) * gain * scale)
    return jnp.concatenate(parts, axis=-1)


def _knorm_kernel(k_ref, g_ref, o_ref):
    o_ref[...] = _sub_head_norm(k_ref[...].astype(F32), g_ref[...], 1.0).astype(BF16)


def _knorm(p, k_gain, *, tr):
    n = p.shape[0]
    return pl.pallas_call(
        _knorm_kernel,
        grid=(n // tr,),
        in_specs=[pl.BlockSpec((tr, W_HALF), lambda i: (i, EVEN_K)),
                  pl.BlockSpec((1, DH), lambda i: (0, 0))],
        out_specs=pl.BlockSpec((tr, W_HALF), lambda i: (i, 0)),
        out_shape=jax.ShapeDtypeStruct((n, W_HALF), BF16),
        compiler_params=_cparams(("parallel",)),
        name="knorm",
    )(p, k_gain)


def _attn_kernel(range_ref, q_ref, k_ref, v_ref, bias_ref, gate_ref, qg_ref, lq1_ref, lk1_ref,
                 lq2_ref, lk2_ref, sg_ref, *rest, tq, tk, nkv, lam_init, q_scale, unroll,
                 conv_tiles_per_seq):
    conv_in = rest[:11]
    o_ref, conv_out = rest[11:13]
    qx_ref, acc_ref, ls_ref, l_ref = rest[13:17]
    cr = ConvRefs(*conv_in, conv_out, *rest[17:20])
    h = pl.program_id(1)
    i = pl.program_id(2)
    contract_last = (((1,), (1,)), ((), ()))

    step = (pl.program_id(0) * pl.num_programs(1) + h) * pl.num_programs(2) + i
    seq_pos = step % conv_tiles_per_seq
    _conv_glu(cr, seq_pos == 0, seq_pos == conv_tiles_per_seq - 1)
    conv_chunks = CONV_TS // CONV_RC

    qx_ref[...] = _sub_head_norm(q_ref[...].astype(F32), qg_ref[...], q_scale).astype(BF16)
    acc_ref[...] = jnp.zeros_like(acc_ref)

    nsub = tq // tk

    def shifted_scores(j, t):
        rows = pl.ds(pl.multiple_of(j * tk, tk), tk)
        s = lax.dot_general(qx_ref[:, t * DH:(t + 1) * DH], k_ref[rows, t * DH:(t + 1) * DH],
                            contract_last, preferred_element_type=F32)
        parts = []
        for a in range(nsub):
            diag = jnp.clip(j - (i * nsub + a), -FAR, FAR) + FAR
            parts.append(s[a * tk:(a + 1) * tk] + bias_ref[0, diag])
        return jnp.concatenate(parts, axis=0)

    def bounded():
        ls_ref[...] = jnp.zeros_like(ls_ref)
        trips = nkv // unroll
        chunks_per_trip = conv_chunks // trips
        _conv_phase(cr, _conv_shift, conv_chunks + 1)
        pieces = [(cc, piece) for cc in range(chunks_per_trip) for piece in range(CONV_PIECES)]
        per_softmax = len(pieces) // (2 * unroll)
        assert per_softmax * 2 * unroll == len(pieces)

        def body(jj, carry):
            for u in range(unroll):
                j = unroll * jj + u
                vv = v_ref[pl.ds(pl.multiple_of(j * tk, tk), tk), :]
                for t in range(2):
                    p = jnp.exp2(shifted_scores(j, t))
                    lane_sums = p[:, :DH]
                    for c in range(1, tk // DH):
                        lane_sums = lane_sums + p[:, c * DH:(c + 1) * DH]
                    at = (2 * u + t) * per_softmax
                    for n, (cc, piece) in enumerate(pieces[at:at + per_softmax]):
                        r0 = n * (tq // per_softmax)
                        _conv_tap_piece(cr, jj * chunks_per_trip + cc, piece,
                                        _zero_token(lane_sums[r0:r0 + SUBLANES]))
                    ls_ref[t] = ls_ref[t] + lane_sums
                    acc_ref[t] = acc_ref[t] + jnp.dot(p.astype(BF16), vv,
                                                      preferred_element_type=F32)
            return carry

        lax.fori_loop(0, trips, body, 0)
        _conv_phase(cr, _conv_norm, conv_chunks, unroll=2)
        for t in range(2):
            l_ref[t] = jnp.sum(ls_ref[t], axis=-1, keepdims=True)

    def running_max():
        _conv_phase(cr, _conv_shift, conv_chunks + 1)
        _conv_phase(cr, _conv_taps, conv_chunks)
        _conv_phase(cr, _conv_norm, conv_chunks, unroll=2)

        def body(j, ml):
            vv = v_ref[pl.ds(pl.multiple_of(j * tk, tk), tk), :]
            out = []
            for t in range(2):
                m, l = ml[2 * t], ml[2 * t + 1]
                s = shifted_scores(j, t)
                mn = jnp.maximum(m, jnp.max(s, axis=-1, keepdims=True))
                p = jnp.exp2(s - mn)
                a = jnp.exp2(m - mn)
                l = a * l + jnp.sum(p, axis=-1, keepdims=True)
                acc_ref[t] = a * acc_ref[t] + jnp.dot(p.astype(BF16), vv,
                                                      preferred_element_type=F32)
                out += [mn, l]
            return tuple(out)

        neg = jnp.full((tq, 1), -jnp.inf, F32)
        zero = jnp.zeros((tq, 1), F32)
        _, l1, _, l2 = lax.fori_loop(0, nkv, body, (neg, zero, neg, zero))
        l_ref[0] = l1
        l_ref[1] = l2

    lax.cond(range_ref[h] <= EXP2_SAFE_RANGE, bounded, running_max)

    lam = (jnp.exp(jnp.sum(lq1_ref[...] * lk1_ref[...], axis=-1, keepdims=True))
           - jnp.exp(jnp.sum(lq2_ref[...] * lk2_ref[...], axis=-1, keepdims=True)) + lam_init)
    o = acc_ref[0] * (1.0 / l_ref[0]) - lam * (acc_ref[1] * (1.0 / l_ref[1]))
    ms = jnp.mean(o * o, axis=-1, keepdims=True)
    y = o * lax.rsqrt(ms + EPS) * sg_ref[...] * (1.0 - lam_init)
    o_ref[...] = (y * _silu(gate_ref[...].astype(F32))).astype(BF16)


def _attention_and_conv(exp_range, p, kx, bias_tiles, q_gain, lq1, lk1, lq2, lk2, subln_g, conv_w,
                        conv_b, conv_ln_g, conv_ln_b, *, batch, seq, tq, tk, lam_init, q_scale):
    n = p.shape[0]
    nq = seq // tq
    nd = bias_tiles.shape[1]
    nkv = seq // tk
    unroll = 4
    trips = nkv // unroll
    conv_chunks = CONV_TS // CONV_RC
    assert tq % tk == 0 and bias_tiles.shape[2:] == (tk, tk) and nkv % unroll == 0
    assert batch * N_HEADS * nq * CONV_TS == n and conv_chunks % trips == 0
    assert seq % CONV_TS == 0 and (CONV_TS + 2 * HALO) % CONV_RC == 0
    per_head = W_HALF // DV

    def q_rows(seg):
        return pl.BlockSpec((tq, DV), lambda b, h, i: (b * nq + i, seg * per_head + h))

    vec = pl.BlockSpec((1, DH), lambda b, h, i: (0, 0))
    conv_in, conv_out, conv_scratch = _conv_specs(lambda b, h, i: (b * N_HEADS + h) * nq + i, n)
    kern = functools.partial(_attn_kernel, tq=tq, tk=tk, nkv=nkv, lam_init=lam_init,
                             q_scale=q_scale, unroll=unroll, conv_tiles_per_seq=seq // CONV_TS)
    return pl.pallas_call(
        kern,
        grid=(batch, N_HEADS, nq),
        in_specs=[
            pl.BlockSpec(memory_space=pltpu.SMEM),
            q_rows(EVEN_Q),
            pl.BlockSpec((seq, DV), lambda b, h, i: (b, h)),
            pl.BlockSpec((seq, DV), lambda b, h, i: (b, EVEN_V * per_head + h)),
            pl.BlockSpec((1, nd, tk, tk), lambda b, h, i: (h, 0, 0, 0),
                         pipeline_mode=pl.Buffered(1)),
            q_rows(EVEN_B_GATE),
            vec, vec, vec, vec, vec,
            pl.BlockSpec((1, DV), lambda b, h, i: (0, 0)),
        ] + conv_in,
        out_specs=[pl.BlockSpec((tq, DV), lambda b, h, i: (b * nq + i, h)), conv_out],
        out_shape=[jax.ShapeDtypeStruct((n, N_HEADS * DV), BF16),
                   jax.ShapeDtypeStruct((n, W_HALF), BF16)],
        scratch_shapes=[
            pltpu.VMEM((tq, DV), BF16),
            pltpu.VMEM((2, tq, DV), F32),
            pltpu.VMEM((2, tq, DH), F32),
            pltpu.VMEM((2, tq, 1), F32),
        ] + conv_scratch,
        compiler_params=_cparams(("parallel", "parallel", "parallel")),
        name="diff_attention",
    )(exp_range, p, kx, p, bias_tiles, p, q_gain, lq1, lk1, lq2, lk2, subln_g,
      p, p, p, p, p, p, p, conv_w, conv_b, conv_ln_g, conv_ln_b)


ROWS_BF16 = 16


def _fnet_constants(seq, ch):
    r = seq // FFT_INNER
    two_pi = 2.0 * np.pi
    ang_r = two_pi * np.outer(np.arange(r), np.arange(r)) / r
    base = np.stack([np.cos(ang_r), -np.sin(ang_r)], axis=1).reshape(2 * r, r)
    w1 = np.kron(base, np.eye(ROWS_BF16))
    ang_tw = two_pi * np.outer(np.arange(FFT_INNER), np.arange(r)) / seq
    tw = np.stack([np.cos(ang_tw), np.sin(ang_tw)])
    tw = tw.reshape(2, FFT_INNER, r // ROWS_BF16, ROWS_BF16).transpose(2, 0, 1, 3)
    ang_i = two_pi * np.outer(np.arange(FFT_INNER), np.arange(FFT_INNER)) / FFT_INNER
    c, s = np.cos(ang_i), np.sin(ang_i)
    w2 = np.block([[c, s], [-s, c]])
    scale = 1.0 / math.sqrt(seq * GROUP_C)
    eye_g = np.eye(ch // GROUP_C)
    bdc = np.kron(eye_g, c * scale)
    bds = np.kron(eye_g, s * scale)
    perm = np.zeros((ROWS_BF16 * ROWS_BF16,) * 2)
    for b in range(ROWS_BF16):
        for f in range(ROWS_BF16):
            perm[b * ROWS_BF16 + f, f * ROWS_BF16 + b] = 1.0
    as_bf16 = lambda a: jnp.asarray(a, dtype=BF16)
    return (as_bf16(w1), jnp.asarray(tw, dtype=F32), as_bf16(w2), as_bf16(bdc), as_bf16(bds),
            as_bf16(perm))


def _fft1_kernel(x_ref, w_ref, o_ref):
    _, r, rows, c = x_ref.shape
    x = x_ref[0].reshape(r * rows, c)
    y = jnp.dot(w_ref[...], x, preferred_element_type=F32).astype(BF16)
    o_ref[0] = y.reshape(r, 2, rows, c)


def _fft2_kernel(a_ref, tw_ref, w2_ref, bdc_ref, bds_ref, perm_ref, gate_ref, o_ref, u_ref, y_ref):
    nf = a_ref.shape[1]
    ch = a_ref.shape[-1]
    for kk in range(nf):
        br = a_ref[0, kk, 0].astype(F32)
        bi = a_ref[0, kk, 1].astype(F32)
        cw = tw_ref[0, 0][:, kk:kk + 1]
        sw = tw_ref[0, 1][:, kk:kk + 1]
        x = jnp.concatenate([br * cw + bi * sw, bi * cw - br * sw], axis=0).astype(BF16)
        u = jnp.dot(w2_ref[...], x, preferred_element_type=F32).astype(BF16)
        u_ref[0, kk * FFT_INNER:(kk + 1) * FFT_INNER, :] = u[:FFT_INNER]
        u_ref[1, kk * FFT_INNER:(kk + 1) * FFT_INNER, :] = u[FFT_INNER:]
    y_ref[...] = (jnp.dot(u_ref[0], bdc_ref[...], preferred_element_type=F32)
                  + jnp.dot(u_ref[1], bds_ref[...], preferred_element_type=F32)).astype(BF16)
    for a in range(FFT_INNER // ROWS_BF16):
        lo = a * ROWS_BF16
        piece = jnp.concatenate(
            [y_ref[k * FFT_INNER + lo:k * FFT_INNER + lo + ROWS_BF16, :] for k in range(nf)], axis=0)
        z = jnp.dot(perm_ref[...], piece, preferred_element_type=F32)
        g = gate_ref[0, lo:lo + ROWS_BF16].reshape(ROWS_BF16 * nf, ch).astype(F32)
        o_ref[0, lo:lo + ROWS_BF16] = (z * _silu(g)).astype(BF16).reshape(ROWS_BF16, nf, ch)


def _fnet(p, *, batch, seq, ch=512):
    c = W_HALF
    r = seq // FFT_INNER
    nf = ROWS_BF16
    assert r % nf == 0 and c % ch == 0
    w1, tw, w2, bdc, bds, perm = _fnet_constants(seq, ch)
    const = pl.Buffered(1)
    stage1 = pl.pallas_call(
        _fft1_kernel,
        grid=(batch, FFT_INNER // nf),
        in_specs=[
            pl.BlockSpec((1, r, nf, c), lambda b, t: (b, 0, t, ODD_C_IN)),
            pl.BlockSpec((2 * r * nf, r * nf), lambda b, t: (0, 0), pipeline_mode=const),
        ],
        out_specs=pl.BlockSpec((1, r, 2, nf, c), lambda b, t: (b, 0, 0, t, 0)),
        out_shape=jax.ShapeDtypeStruct((batch, r, 2, FFT_INNER, c), BF16),
        compiler_params=_cparams(("parallel", "parallel")),
        name="fft_stage1",
    )(p.reshape(batch, r, FFT_INNER, p.shape[1]), w1)
    gate_cols = ODD_C_GATE * (c // ch)
    cmat = lambda shape: pl.BlockSpec(shape, lambda b, f, j: (0,) * len(shape), pipeline_mode=const)
    out = pl.pallas_call(
        _fft2_kernel,
        grid=(batch, r // nf, c // ch),
        in_specs=[
            pl.BlockSpec((1, nf, 2, FFT_INNER, ch), lambda b, f, j: (b, f, 0, 0, j)),
            pl.BlockSpec((1, 2, FFT_INNER, nf), lambda b, f, j: (f, 0, 0, 0)),
            cmat((2 * FFT_INNER, 2 * FFT_INNER)), cmat((ch, ch)), cmat((ch, ch)),
            cmat((nf * nf, nf * nf)),
            pl.BlockSpec((1, FFT_INNER, nf, ch), lambda b, f, j: (b, 0, f, gate_cols + j)),
        ],
        out_specs=pl.BlockSpec((1, FFT_INNER, nf, ch), lambda b, f, j: (b, 0, f, j)),
        out_shape=jax.ShapeDtypeStruct((batch, FFT_INNER, r, c), BF16),
        scratch_shapes=[pltpu.VMEM((2, nf * FFT_INNER, ch), BF16),
                        pltpu.VMEM((nf * FFT_INNER, ch), BF16)],
        compiler_params=_cparams(("parallel", "parallel", "parallel")),
        name="fft_stage2",
    )(stage1, tw, w2, bdc, bds, perm, p.reshape(batch, FFT_INNER, r, p.shape[1]))
    return out.reshape(batch * seq, c)


def _sgu_kernel(u_ref, v_ref, gate_ref, lg_ref, lb_ref, ws_ref, bt_ref, o_ref, *, tr):
    v = v_ref[...].astype(F32)
    mu = jnp.mean(v, axis=-1, keepdims=True)
    vc = v - mu
    var = jnp.mean(vc * vc, axis=-1, keepdims=True)
    vn = (vc * lax.rsqrt(var + EPS) * lg_ref[...] + lb_ref[...]).astype(BF16)
    for g in range(SGU_GROUPS):
        cols = slice(g * SGU_DG, (g + 1) * SGU_DG)
        bcol = jnp.broadcast_to(bt_ref[:, g:g + 1], (SGU_CHUNK, SGU_DG))
        for n in range(tr // SGU_CHUNK):
            rows = slice(n * SGU_CHUNK, (n + 1) * SGU_CHUNK)
            sv = jnp.dot(ws_ref[g], vn[rows, cols], preferred_element_type=F32) + bcol
            out = u_ref[rows, cols].astype(F32) * sv * _silu(gate_ref[rows, cols].astype(F32))
            o_ref[rows, cols] = out.astype(BF16)


def _sgu_inputs(p, ln_g, ln_b, ws_bf16, b_t, *, tr):
    c = W_HALF

    def seg(s):
        return pl.BlockSpec((tr, c), lambda i: (i, s))

    vec = pl.BlockSpec((1, c), lambda i: (0, 0))
    specs = [seg(ODD_U), seg(ODD_V), seg(ODD_D_GATE), vec, vec,
             pl.BlockSpec((SGU_GROUPS, SGU_CHUNK, SGU_CHUNK), lambda i: (0, 0, 0)),
             pl.BlockSpec((SGU_CHUNK, SGU_GROUPS), lambda i: (0, 0))]
    return [p, p, p, ln_g, ln_b, ws_bf16, b_t], specs


QK_SCALE_LOG2 = LOG2E / math.sqrt(DH)


def _attention_bias(p, *, tk):
    return _bias_tiles(p["rel_bias"], p["q_norm_g"], p["k_norm_g"], tk=tk, q_scale=QK_SCALE_LOG2)


def _trunk(x, p, bias_tiles, exp_range):
    batch, seq, d = x.shape
    x2d = x.reshape(batch * seq, d)
    row = lambda a: a.reshape(1, -1)

    lam_init = 0.8 - 0.6 * math.exp(-0.3 * 0)
    pe = _inproj(x2d, row(p["norm_g"][0]), p["w_in_even"], tm=INPROJ_TM, tn=INPROJ_TN_EVEN)
    kx = _knorm(pe, row(p["k_norm_g"]), tr=KNORM_TR)
    mix_b, mix_a = _attention_and_conv(
        exp_range, pe, kx, bias_tiles, row(p["q_norm_g"]), row(p["lam_q1"]), row(p["lam_k1"]),
        row(p["lam_q2"]), row(p["lam_k2"]), row(p["subln_g"]), p["conv_w"], row(p["conv_b"]),
        row(p["conv_ln_g"]), row(p["conv_ln_b"]), batch=batch, seq=seq, tq=ATTN_TQ, tk=ATTN_TK,
        lam_init=lam_init, q_scale=QK_SCALE_LOG2)
    x1, h1 = _outproj(mix_a, mix_b, p["w_out_even"], x2d, tm=OUTPROJ_TM,
                      next_norm_gain=row(p["norm_g"][1]))

    po = _inproj_normalised(h1, p["w_in_odd"], tm=INPROJ_TM, tn=INPROJ_TN_ODD)
    mix_c = _fnet(po, batch=batch, seq=seq)
    sgu = _sgu_inputs(po, row(p["sgu_ln_g"]), row(p["sgu_ln_b"]), p["sgu_w"], p["sgu_b"].T,
                      tr=OUTPROJ_TM)
    y = _outproj(mix_c, None, p["w_out_odd"], x1, tm=OUTPROJ_TM, sgu=sgu)
    return y.reshape(batch, seq, d)


def kernel(x_prompt, x_sample, norm_g, w_in_even, conv_w, conv_b, conv_ln_g, conv_ln_b,
           q_norm_g, k_norm_g, lam_q1, lam_k1, lam_q2, lam_k2, subln_g, rel_bias, w_out_even,
           w_in_odd, sgu_ln_g, sgu_ln_b, sgu_w, sgu_b, w_out_odd):
    p = dict(
        norm_g=norm_g, w_in_even=w_in_even[0].astype(BF16), conv_w=conv_w[0], conv_b=conv_b[0],
        conv_ln_g=conv_ln_g[0], conv_ln_b=conv_ln_b[0], q_norm_g=q_norm_g[0],
        k_norm_g=k_norm_g[0], lam_q1=lam_q1[0], lam_k1=lam_k1[0], lam_q2=lam_q2[0],
        lam_k2=lam_k2[0], subln_g=subln_g[0], w_out_even=w_out_even[0],
        w_in_odd=w_in_odd[0], sgu_ln_g=sgu_ln_g[0], sgu_ln_b=sgu_ln_b[0],
        sgu_w=sgu_w[0].astype(BF16), sgu_b=sgu_b[0], w_out_odd=w_out_odd[0],
        rel_bias=rel_bias)
    bias_tiles, exp_range = _attention_bias(p, tk=ATTN_TK)
    y_prompt = _trunk(x_prompt, p, bias_tiles, exp_range)
    y_sample = _trunk(x_sample, p, bias_tiles, exp_range)
    return (y_prompt, y_sample)
```

```python
import functools
import math

import numpy as np
import jax
import jax.numpy as jnp
from jax import lax
from jax.experimental import pallas as pl
from jax.experimental.pallas import tpu as pltpu

F32 = jnp.float32
BF16 = jnp.bfloat16

EPS = 1e-6
LOG2E = math.log2(math.e)

D_MODEL = 2048
W_HALF = D_MODEL // 2
DH = 128
N_HEADS = 4
DV = 2 * DH
CONV_W = 31
CONV_PAD = CONV_W // 2
N_BUCKETS = 32
MAX_DIST = 128
FFT_INNER = 128
GROUP_C = 128
SGU_GROUPS = 4
SGU_CHUNK = 128
SGU_DG = W_HALF // SGU_GROUPS

EVEN_A_LO, EVEN_A_HI, EVEN_A_GATE, EVEN_Q, EVEN_K, EVEN_V, EVEN_B_GATE = range(7)
ODD_C_IN, ODD_C_GATE, ODD_U, ODD_V, ODD_D_GATE = range(5)

VMEM_LIMIT_V7X = 56 * 1024 * 1024
HALO = 16
SUBLANES = 8

INPROJ_TM = 1024
INPROJ_TN_EVEN = 1792
INPROJ_TN_ODD = 1280
OUTPROJ_TM = 512
KNORM_TR = 1024
ATTN_TQ = 1024
ATTN_TK = 512


def _cparams(sem):
    return pltpu.CompilerParams(dimension_semantics=sem, vmem_limit_bytes=VMEM_LIMIT_V7X)


def _sigmoid(x):
    return 0.5 * jnp.tanh(0.5 * x) + 0.5


def _silu(x):
    return x * _sigmoid(x)


def _inproj_kernel(x_ref, g_ref, w_ref, o_ref, h_ref):
    @pl.when(pl.program_id(1) == 0)
    def _():
        x = x_ref[...]
        ms = jnp.mean(x * x, axis=-1, keepdims=True)
        h_ref[...] = (x * lax.rsqrt(ms + EPS) * g_ref[...]).astype(BF16)

    o_ref[...] = jnp.dot(h_ref[...], w_ref[...], preferred_element_type=F32).astype(BF16)


def _inproj(x2d, g, w_bf16, *, tm, tn):
    n, d = x2d.shape
    p = w_bf16.shape[1]
    return pl.pallas_call(
        _inproj_kernel,
        grid=(n // tm, p // tn),
        in_specs=[
            pl.BlockSpec((tm, d), lambda i, j: (i, 0)),
            pl.BlockSpec((1, d), lambda i, j: (0, 0)),
            pl.BlockSpec((d, tn), lambda i, j: (0, j)),
        ],
        out_specs=pl.BlockSpec((tm, tn), lambda i, j: (i, j)),
        out_shape=jax.ShapeDtypeStruct((n, p), BF16),
        scratch_shapes=[pltpu.VMEM((tm, d), BF16)],
        compiler_params=_cparams(("parallel", "arbitrary")),
        name="inproj",
    )(x2d, g, w_bf16)


def _matmul_kernel(h_ref, w_ref, o_ref, wb_ref):
    @pl.when(pl.program_id(1) == 0)
    def _():
        wb_ref[...] = w_ref[...].astype(BF16)

    o_ref[...] = jnp.dot(h_ref[...], wb_ref[...], preferred_element_type=F32).astype(BF16)


def _inproj_normalised(h, w_f32, *, tm, tn):
    n, d = h.shape
    p = w_f32.shape[1]
    return pl.pallas_call(
        _matmul_kernel,
        grid=(p // tn, n // tm),
        in_specs=[
            pl.BlockSpec((tm, d), lambda j, i: (i, 0)),
            pl.BlockSpec((d, tn), lambda j, i: (0, j)),
        ],
        out_specs=pl.BlockSpec((tm, tn), lambda j, i: (i, j)),
        out_shape=jax.ShapeDtypeStruct((n, p), BF16),
        scratch_shapes=[pltpu.VMEM((d, tn), BF16)],
        compiler_params=_cparams(("arbitrary", "arbitrary")),
        name="inproj",
    )(h, w_f32)


def _outproj_kernel(ma_ref, *rest, n_sgu_inputs, emit_norm):
    rest = list(rest)
    second = [rest.pop(0) for _ in range(n_sgu_inputs or 1)]
    wa_ref, wb_ref, x_ref = rest[:3]
    rest = rest[3:]
    gain_ref = rest.pop(0) if emit_norm else None
    o_ref = rest.pop(0)
    h_ref = rest.pop(0) if emit_norm else None
    w_ref = rest.pop(0)
    mb_ref = rest.pop(0) if n_sgu_inputs else second[0]

    @pl.when(pl.program_id(0) == 0)
    def _():
        w_ref[0] = wa_ref[...].astype(BF16)
        w_ref[1] = wb_ref[...].astype(BF16)

    acc = jnp.dot(ma_ref[...], w_ref[0], preferred_element_type=F32)
    if n_sgu_inputs:
        _sgu_kernel(*second, mb_ref, tr=mb_ref.shape[0])
    acc = acc + jnp.dot(mb_ref[...], w_ref[1], preferred_element_type=F32)
    y = x_ref[...] + acc
    o_ref[...] = y
    if emit_norm:
        ms = jnp.mean(y * y, axis=-1, keepdims=True)
        h_ref[...] = (y * lax.rsqrt(ms + EPS) * gain_ref[...]).astype(BF16)


def _outproj(mix_a, mix_b, w_f32, x2d, *, tm, sgu=None, next_norm_gain=None):
    n, d = x2d.shape
    half = mix_a.shape[1]
    const = pl.Buffered(1)
    rows = pl.BlockSpec((tm, half), lambda i: (i, 0))
    full = pl.BlockSpec((tm, d), lambda i: (i, 0))
    emit_norm = next_norm_gain is not None
    scratch = [pltpu.VMEM((2, half, d), BF16)]
    if sgu is None:
        second, second_specs = [mix_b], [rows]
    else:
        second, second_specs = sgu
        scratch.append(pltpu.VMEM((tm, half), BF16))
    gain, gain_spec = ([next_norm_gain], [pl.BlockSpec((1, d), lambda i: (0, 0))]) if emit_norm \
        else ([], [])
    out = pl.pallas_call(
        functools.partial(_outproj_kernel, n_sgu_inputs=0 if sgu is None else len(second),
                          emit_norm=emit_norm),
        grid=(n // tm,),
        in_specs=[rows] + second_specs + [
            pl.BlockSpec((half, d), lambda i: (0, 0), pipeline_mode=const),
            pl.BlockSpec((half, d), lambda i: (1, 0), pipeline_mode=const),
            full,
        ] + gain_spec,
        out_specs=[full] + ([full] if emit_norm else []),
        out_shape=[jax.ShapeDtypeStruct((n, d), F32)]
        + ([jax.ShapeDtypeStruct((n, d), BF16)] if emit_norm else []),
        scratch_shapes=scratch,
        compiler_params=_cparams(("arbitrary",)),
        name="outproj",
    )(mix_a, *second, w_f32, w_f32, x2d, *gain)
    return out if emit_norm else out[0]


CONV_TS = 256
CONV_RC = 32
CONV_LANE_PARTS = 4


class ConvRefs:
    def __init__(self, lo, hi, plo, phi, nlo, nhi, gate, w, b, lg, lb, out, xs, wb, y):
        self.lo, self.hi, self.plo, self.phi, self.nlo, self.nhi = lo, hi, plo, phi, nlo, nhi
        self.gate, self.w, self.b, self.lg, self.lb, self.out = gate, w, b, lg, lb, out
        self.xs, self.wb, self.y = xs, wb, y


def _conv_glu(cr, first, last):
    ts, c = cr.y.shape
    padded = ts + 2 * HALO

    def glu(a, b):
        return a.astype(F32) * _sigmoid(b.astype(F32))

    cr.xs[0, HALO:HALO + ts, :] = glu(cr.lo[...], cr.hi[...])
    cr.xs[0, 0:HALO, :] = jnp.where(first, 0.0, glu(cr.plo[...], cr.phi[...]))
    cr.xs[0, HALO + ts:padded, :] = jnp.where(last, 0.0, glu(cr.nlo[...], cr.nhi[...]))
    cr.xs[0, padded:padded + SUBLANES, :] = jnp.zeros((SUBLANES, c), F32)
    for k in range(CONV_W):
        cr.wb[k] = jnp.broadcast_to(cr.w[k:k + 1, :], (SUBLANES, c))


def _conv_shift(cr, q):
    c = cr.y.shape[1]
    lw = c // CONV_LANE_PARTS
    row = pl.multiple_of(q * CONV_RC, CONV_RC)
    for col in range(0, c, lw):
        win = cr.xs[0, pl.ds(row, CONV_RC + SUBLANES), col:col + lw]
        for r in range(1, SUBLANES):
            cr.xs[r, pl.ds(row, CONV_RC), col:col + lw] = pltpu.roll(
                win, CONV_RC + SUBLANES - r, axis=0)[:CONV_RC]


CONV_GROUPS = CONV_RC // SUBLANES
CONV_GROUP_PARTS = 1
CONV_PIECES = CONV_LANE_PARTS * CONV_GROUP_PARTS


def _zero_token(x):
    bits = pltpu.bitcast(x[:SUBLANES, :DH], jnp.uint32)
    return pltpu.bitcast(lax.shift_right_logical(bits, jnp.uint32(32)), F32)


def _conv_tap_piece(cr, q, piece, token=None):
    c = cr.y.shape[1]
    lw = c // CONV_LANE_PARTS
    base0 = HALO - CONV_PAD
    part, gpart = divmod(piece, CONV_GROUP_PARTS)
    per = CONV_GROUPS // CONV_GROUP_PARTS
    groups = range(gpart * per, (gpart + 1) * per)
    col = part * lw
    row = pl.multiple_of(q * CONV_RC, CONV_RC)
    start = jnp.zeros((SUBLANES, lw), F32) if token is None else jnp.concatenate(
        [token] * (lw // DH), axis=-1)
    acc = {g: start for g in groups}
    for r in range(SUBLANES):
        taps = [(a, SUBLANES * a + r - base0) for a in range((base0 + CONV_W - 1) // SUBLANES + 1)
                if 0 <= SUBLANES * a + r - base0 < CONV_W]
        x = {s: cr.xs[r, pl.ds(row + SUBLANES * s, SUBLANES), col:col + lw]
             for s in sorted({a + g for a, _ in taps for g in groups})}
        for a, k in taps:
            wk = cr.wb[k, :, col:col + lw]
            for g in groups:
                acc[g] = acc[g] + x[a + g] * wk
    bias = cr.b[:, col:col + lw]
    for g in groups:
        cr.y[pl.ds(row + SUBLANES * g, SUBLANES), col:col + lw] = acc[g] + bias


def _conv_taps(cr, q):
    for piece in range(CONV_PIECES):
        _conv_tap_piece(cr, q, piece)


def _conv_norm(cr, q):
    row = pl.multiple_of(q * CONV_RC, CONV_RC)
    y = cr.y[pl.ds(row, CONV_RC), :]
    mu = jnp.mean(y, axis=-1, keepdims=True)
    yc = y - mu
    var = jnp.mean(yc * yc, axis=-1, keepdims=True)
    z = yc * lax.rsqrt(var + EPS) * cr.lg[...] + cr.lb[...]
    out = _silu(z) * _silu(cr.gate[pl.ds(row, CONV_RC), :].astype(F32))
    cr.out[pl.ds(row, CONV_RC), :] = out.astype(BF16)


def _conv_phase(cr, chunk_fn, n_chunks, unroll=1):
    def body(q, carry):
        chunk_fn(cr, q)
        return carry
    lax.fori_loop(0, n_chunks, body, 0, unroll=unroll)


def _conv_specs(index_of_step, n_rows):
    c = W_HALF
    hb = CONV_TS // HALO
    nhb = n_rows // HALO

    def rows(col):
        return pl.BlockSpec((CONV_TS, c), lambda *g: (index_of_step(*g), col))

    def prev(col):
        return pl.BlockSpec((HALO, c), lambda *g: (jnp.maximum(index_of_step(*g) * hb - 1, 0), col))

    def nxt(col):
        return pl.BlockSpec(
            (HALO, c), lambda *g: (jnp.minimum((index_of_step(*g) + 1) * hb, nhb - 1), col))

    vec = pl.BlockSpec((1, c), lambda *g: (0, 0))
    in_specs = [rows(EVEN_A_LO), rows(EVEN_A_HI), prev(EVEN_A_LO), prev(EVEN_A_HI),
                nxt(EVEN_A_LO), nxt(EVEN_A_HI), rows(EVEN_A_GATE),
                pl.BlockSpec((CONV_W, c), lambda *g: (0, 0)), vec, vec, vec]
    out_spec = pl.BlockSpec((CONV_TS, c), lambda *g: (index_of_step(*g), 0))
    scratch = [pltpu.VMEM((SUBLANES, CONV_TS + 2 * HALO + SUBLANES, c), F32),
               pltpu.VMEM((CONV_W, SUBLANES, c), F32),
               pltpu.VMEM((CONV_TS, c), F32)]
    return in_specs, out_spec, scratch


def _bucket_thresholds():
    nb = N_BUCKETS // 2
    max_exact = nb // 2
    n = np.arange(1, 4 * MAX_DIST, dtype=np.float64)
    large = max_exact + (np.log(n / max_exact) / math.log(MAX_DIST / max_exact)
                         * (nb - max_exact)).astype(np.int64)
    large = np.minimum(large, nb - 1)
    thr = [int(n[np.argmax(large >= b)]) for b in range(max_exact + 1, nb)]
    return max_exact, thr


def _bias_kernel(rb_ref, qg_ref, kg_ref, o_ref, range_ref, *, tk, lo_diag, q_scale):
    h = pl.program_id(0)
    d = pl.program_id(1) + lo_diag
    gq = jnp.abs(qg_ref[0])
    gk = jnp.abs(kg_ref[0])
    for c in range(1, DH):
        gq = jnp.maximum(gq, jnp.abs(qg_ref[c]))
        gk = jnp.maximum(gk, jnp.abs(kg_ref[c]))
    qk_bound = gq * gk * (DH * q_scale * BF16_SLACK)
    bmax = rb_ref[0, h]
    bmin = rb_ref[0, h]
    for b in range(1, N_BUCKETS):
        bmax = jnp.maximum(bmax, rb_ref[b, h])
        bmin = jnp.minimum(bmin, rb_ref[b, h])
    shift = qk_bound + bmax * LOG2E
    range_ref[h] = 2.0 * qk_bound + (bmax - bmin) * LOG2E

    @pl.when(jnp.abs(d) < FAR)
    def _():
        row = lax.broadcasted_iota(jnp.int32, (tk, tk), 0)
        col = lax.broadcasted_iota(jnp.int32, (tk, tk), 1)
        rel = col - row + d * tk
        n = jnp.abs(rel)
        max_exact, thr = _bucket_thresholds()
        bucket = jnp.minimum(n, max_exact)
        for t in thr:
            bucket = bucket + jnp.where(n >= t, 1, 0)
        bucket = bucket + jnp.where(rel > 0, N_BUCKETS // 2, 0)
        level = [rb_ref[b, h] for b in range(N_BUCKETS)]
        bit = 1
        while len(level) > 1:
            odd = (bucket & bit) != 0
            level = [jnp.where(odd, level[i + 1], level[i]) for i in range(0, len(level), 2)]
            bit *= 2
        o_ref[0, 0] = level[0] * LOG2E - shift

    @pl.when(d <= -FAR)
    def _():
        o_ref[0, 0] = jnp.full((tk, tk), rb_ref[N_BUCKETS // 2 - 1, h] * LOG2E - shift, F32)

    @pl.when(d >= FAR)
    def _():
        o_ref[0, 0] = jnp.full((tk, tk), rb_ref[N_BUCKETS - 1, h] * LOG2E - shift, F32)


EXP2_SAFE_RANGE = 100.0
BF16_SLACK = 1.01
N_DIAG = 5
FAR = N_DIAG // 2


def _bias_tiles(rel_bias, q_gain, k_gain, *, tk, q_scale):
    assert tk + 1 >= _bucket_thresholds()[1][-1]
    nd = N_DIAG
    smem = pl.BlockSpec(memory_space=pltpu.SMEM)
    kern = functools.partial(_bias_kernel, tk=tk, lo_diag=-FAR, q_scale=q_scale)
    return pl.pallas_call(
        kern,
        grid=(N_HEADS, nd),
        in_specs=[smem, smem, smem],
        out_specs=[pl.BlockSpec((1, 1, tk, tk), lambda h, d: (h, d, 0, 0)), smem],
        out_shape=[jax.ShapeDtypeStruct((N_HEADS, nd, tk, tk), F32),
                   jax.ShapeDtypeStruct((N_HEADS,), F32)],
        compiler_params=_cparams(("arbitrary", "arbitrary")),
        name="bias_tiles",
    )(rel_bias, q_gain, k_gain)


def _sub_head_norm(x, gain, scale):
    parts = []
    for t in range(x.shape[1] // DH):
        blk = x[:, t * DH:(t + 1) * DH]
        ms = jnp.mean(blk * blk, axis=-1, keepdims=True)
        parts.append(blk * lax.rsqrt(ms + EPS) * gain * scale)
    return jnp.concatenate(parts, axis=-1)


def _knorm_kernel(k_ref, g_ref, o_ref):
    o_ref[...] = _sub_head_norm(k_ref[...].astype(F32), g_ref[...], 1.0).astype(BF16)


def _knorm(p, k_gain, *, tr):
    n = p.shape[0]
    return pl.pallas_call(
        _knorm_kernel,
        grid=(n // tr,),
        in_specs=[pl.BlockSpec((tr, W_HALF), lambda i: (i, EVEN_K)),
                  pl.BlockSpec((1, DH), lambda i: (0, 0))],
        out_specs=pl.BlockSpec((tr, W_HALF), lambda i: (i, 0)),
        out_shape=jax.ShapeDtypeStruct((n, W_HALF), BF16),
        compiler_params=_cparams(("parallel",)),
        name="knorm",
    )(p, k_gain)


def _attn_kernel(range_ref, q_ref, k_ref, v_ref, bias_ref, gate_ref, qg_ref, lq1_ref, lk1_ref,
                 lq2_ref, lk2_ref, sg_ref, *rest, tq, tk, nkv, lam_init, q_scale, unroll,
                 conv_tiles_per_seq):
    conv_in = rest[:11]
    o_ref, conv_out = rest[11:13]
    qx_ref, acc_ref, ls_ref, l_ref = rest[13:17]
    cr = ConvRefs(*conv_in, conv_out, *rest[17:20])
    h = pl.program_id(1)
    i = pl.program_id(2)
    contract_last = (((1,), (1,)), ((), ()))

    step = (pl.program_id(0) * pl.num_programs(1) + h) * pl.num_programs(2) + i
    seq_pos = step % conv_tiles_per_seq
    _conv_glu(cr, seq_pos == 0, seq_pos == conv_tiles_per_seq - 1)
    conv_chunks = CONV_TS // CONV_RC

    qx_ref[...] = _sub_head_norm(q_ref[...].astype(F32), qg_ref[...], q_scale).astype(BF16)
    acc_ref[...] = jnp.zeros_like(acc_ref)

    nsub = tq // tk

    def shifted_scores(j, t):
        rows = pl.ds(pl.multiple_of(j * tk, tk), tk)
        s = lax.dot_general(qx_ref[:, t * DH:(t + 1) * DH], k_ref[rows, t * DH:(t + 1) * DH],
                            contract_last, preferred_element_type=F32)
        parts = []
        for a in range(nsub):
            diag = jnp.clip(j - (i * nsub + a), -FAR, FAR) + FAR
            parts.append(s[a * tk:(a + 1) * tk] + bias_ref[0, diag])
        return jnp.concatenate(parts, axis=0)

    def bounded():
        ls_ref[...] = jnp.zeros_like(ls_ref)
        trips = nkv // unroll
        chunks_per_trip = conv_chunks // trips
        _conv_phase(cr, _conv_shift, conv_chunks + 1)
        pieces = [(cc, piece) for cc in range(chunks_per_trip) for piece in range(CONV_PIECES)]
        per_softmax = len(pieces) // (2 * unroll)
        assert per_softmax * 2 * unroll == len(pieces)

        def body(jj, carry):
            for u in range(unroll):
                j = unroll * jj + u
                vv = v_ref[pl.ds(pl.multiple_of(j * tk, tk), tk), :]
                for t in range(2):
                    p = jnp.exp2(shifted_scores(j, t))
                    lane_sums = p[:, :DH]
                    for c in range(1, tk // DH):
                        lane_sums = lane_sums + p[:, c * DH:(c + 1) * DH]
                    at = (2 * u + t) * per_softmax
                    for n, (cc, piece) in enumerate(pieces[at:at + per_softmax]):
                        r0 = n * (tq // per_softmax)
                        _conv_tap_piece(cr, jj * chunks_per_trip + cc, piece,
                                        _zero_token(lane_sums[r0:r0 + SUBLANES]))
                    ls_ref[t] = ls_ref[t] + lane_sums
                    acc_ref[t] = acc_ref[t] + jnp.dot(p.astype(BF16), vv,
                                                      preferred_element_type=F32)
            return carry

        lax.fori_loop(0, trips, body, 0)
        _conv_phase(cr, _conv_norm, conv_chunks, unroll=2)
        for t in range(2):
            l_ref[t] = jnp.sum(ls_ref[t], axis=-1, keepdims=True)

    def running_max():
        _conv_phase(cr, _conv_shift, conv_chunks + 1)
        _conv_phase(cr, _conv_taps, conv_chunks)
        _conv_phase(cr, _conv_norm, conv_chunks, unroll=2)

        def body(j, ml):
            vv = v_ref[pl.ds(pl.multiple_of(j * tk, tk), tk), :]
            out = []
            for t in range(2):
                m, l = ml[2 * t], ml[2 * t + 1]
                s = shifted_scores(j, t)
                mn = jnp.maximum(m, jnp.max(s, axis=-1, keepdims=True))
                p = jnp.exp2(s - mn)
                a = jnp.exp2(m - mn)
                l = a * l + jnp.sum(p, axis=-1, keepdims=True)
                acc_ref[t] = a * acc_ref[t] + jnp.dot(p.astype(BF16), vv,
                                                      preferred_element_type=F32)
                out += [mn, l]
            return tuple(out)

        neg = jnp.full((tq, 1), -jnp.inf, F32)
        zero = jnp.zeros((tq, 1), F32)
        _, l1, _, l2 = lax.fori_loop(0, nkv, body, (neg, zero, neg, zero))
        l_ref[0] = l1
        l_ref[1] = l2

    lax.cond(range_ref[h] <= EXP2_SAFE_RANGE, bounded, running_max)

    lam = (jnp.exp(jnp.sum(lq1_ref[...] * lk1_ref[...], axis=-1, keepdims=True))
           - jnp.exp(jnp.sum(lq2_ref[...] * lk2_ref[...], axis=-1, keepdims=True)) + lam_init)
    o = acc_ref[0] * (1.0 / l_ref[0]) - lam * (acc_ref[1] * (1.0 / l_ref[1]))
    ms = jnp.mean(o * o, axis=-1, keepdims=True)
    y = o * lax.rsqrt(ms + EPS) * sg_ref[...] * (1.0 - lam_init)
    o_ref[...] = (y * _silu(gate_ref[...].astype(F32))).astype(BF16)


def _attention_and_conv(exp_range, p, kx, bias_tiles, q_gain, lq1, lk1, lq2, lk2, subln_g, conv_w,
                        conv_b, conv_ln_g, conv_ln_b, *, batch, seq, tq, tk, lam_init, q_scale):
    n = p.shape[0]
    nq = seq // tq
    nd = bias_tiles.shape[1]
    nkv = seq // tk
    unroll = 8 if nkv >= 16 else 4
    trips = nkv // unroll
    conv_chunks = CONV_TS // CONV_RC
    assert tq % tk == 0 and bias_tiles.shape[2:] == (tk, tk) and nkv % unroll == 0
    assert batch * N_HEADS * nq * CONV_TS == n and conv_chunks % trips == 0
    assert seq % CONV_TS == 0 and (CONV_TS + 2 * HALO) % CONV_RC == 0
    per_head = W_HALF // DV

    def q_rows(seg):
        return pl.BlockSpec((tq, DV), lambda b, h, i: (b * nq + i, seg * per_head + h))

    vec = pl.BlockSpec((1, DH), lambda b, h, i: (0, 0))
    conv_in, conv_out, conv_scratch = _conv_specs(lambda b, h, i: (b * N_HEADS + h) * nq + i, n)
    kern = functools.partial(_attn_kernel, tq=tq, tk=tk, nkv=nkv, lam_init=lam_init,
                             q_scale=q_scale, unroll=unroll, conv_tiles_per_seq=seq // CONV_TS)
    return pl.pallas_call(
        kern,
        grid=(batch, N_HEADS, nq),
        in_specs=[
            pl.BlockSpec(memory_space=pltpu.SMEM),
            q_rows(EVEN_Q),
            pl.BlockSpec((seq, DV), lambda b, h, i: (b, h)),
            pl.BlockSpec((seq, DV), lambda b, h, i: (b, EVEN_V * per_head + h)),
            pl.BlockSpec((1, nd, tk, tk), lambda b, h, i: (h, 0, 0, 0),
                         pipeline_mode=pl.Buffered(1)),
            q_rows(EVEN_B_GATE),
            vec, vec, vec, vec, vec,
            pl.BlockSpec((1, DV), lambda b, h, i: (0, 0)),
        ] + conv_in,
        out_specs=[pl.BlockSpec((tq, DV), lambda b, h, i: (b * nq + i, h)), conv_out],
        out_shape=[jax.ShapeDtypeStruct((n, N_HEADS * DV), BF16),
                   jax.ShapeDtypeStruct((n, W_HALF), BF16)],
        scratch_shapes=[
            pltpu.VMEM((tq, DV), BF16),
            pltpu.VMEM((2, tq, DV), F32),
            pltpu.VMEM((2, tq, DH), F32),
            pltpu.VMEM((2, tq, 1), F32),
        ] + conv_scratch,
        compiler_params=_cparams(("parallel", "parallel", "parallel")),
        name="diff_attention",
    )(exp_range, p, kx, p, bias_tiles, p, q_gain, lq1, lk1, lq2, lk2, subln_g,
      p, p, p, p, p, p, p, conv_w, conv_b, conv_ln_g, conv_ln_b)


ROWS_BF16 = 16


def _fnet_constants(seq, ch):
    r = seq // FFT_INNER
    two_pi = 2.0 * np.pi
    ang_r = two_pi * np.outer(np.arange(r), np.arange(r)) / r
    base = np.stack([np.cos(ang_r), -np.sin(ang_r)], axis=1).reshape(2 * r, r)
    w1 = np.kron(base, np.eye(ROWS_BF16))
    ang_tw = two_pi * np.outer(np.arange(FFT_INNER), np.arange(r)) / seq
    tw = np.stack([np.cos(ang_tw), np.sin(ang_tw)])
    tw = tw.reshape(2, FFT_INNER, r // ROWS_BF16, ROWS_BF16).transpose(2, 0, 1, 3)
    ang_i = two_pi * np.outer(np.arange(FFT_INNER), np.arange(FFT_INNER)) / FFT_INNER
    c, s = np.cos(ang_i), np.sin(ang_i)
    w2 = np.block([[c, s], [-s, c]])
    scale = 1.0 / math.sqrt(seq * GROUP_C)
    eye_g = np.eye(ch // GROUP_C)
    bdc = np.kron(eye_g, c * scale)
    bds = np.kron(eye_g, s * scale)
    perm = np.zeros((ROWS_BF16 * ROWS_BF16,) * 2)
    for b in range(ROWS_BF16):
        for f in range(ROWS_BF16):
            perm[b * ROWS_BF16 + f, f * ROWS_BF16 + b] = 1.0
    as_bf16 = lambda a: jnp.asarray(a, dtype=BF16)
    return (as_bf16(w1), jnp.asarray(tw, dtype=F32), as_bf16(w2), as_bf16(bdc), as_bf16(bds),
            as_bf16(perm))


def _fft1_kernel(x_ref, w_ref, o_ref):
    _, r, rows, c = x_ref.shape
    x = x_ref[0].reshape(r * rows, c)
    y = jnp.dot(w_ref[...], x, preferred_element_type=F32).astype(BF16)
    o_ref[0] = y.reshape(r, 2, rows, c)


def _fft2_kernel(a_ref, tw_ref, w2_ref, bdc_ref, bds_ref, perm_ref, gate_ref, o_ref, u_ref, y_ref):
    nf = a_ref.shape[1]
    ch = a_ref.shape[-1]
    for kk in range(nf):
        br = a_ref[0, kk, 0].astype(F32)
        bi = a_ref[0, kk, 1].astype(F32)
        cw = tw_ref[0, 0][:, kk:kk + 1]
        sw = tw_ref[0, 1][:, kk:kk + 1]
        x = jnp.concatenate([br * cw + bi * sw, bi * cw - br * sw], axis=0).astype(BF16)
        u = jnp.dot(w2_ref[...], x, preferred_element_type=F32).astype(BF16)
        u_ref[0, kk * FFT_INNER:(kk + 1) * FFT_INNER, :] = u[:FFT_INNER]
        u_ref[1, kk * FFT_INNER:(kk + 1) * FFT_INNER, :] = u[FFT_INNER:]
    y_ref[...] = (jnp.dot(u_ref[0], bdc_ref[...], preferred_element_type=F32)
                  + jnp.dot(u_ref[1], bds_ref[...], preferred_element_type=F32)).astype(BF16)
    for a in range(FFT_INNER // ROWS_BF16):
        lo = a * ROWS_BF16
        piece = jnp.concatenate(
            [y_ref[k * FFT_INNER + lo:k * FFT_INNER + lo + ROWS_BF16, :] for k in range(nf)], axis=0)
        z = jnp.dot(perm_ref[...], piece, preferred_element_type=F32)
        g = gate_ref[0, lo:lo + ROWS_BF16].reshape(ROWS_BF16 * nf, ch).astype(F32)
        o_ref[0, lo:lo + ROWS_BF16] = (z * _silu(g)).astype(BF16).reshape(ROWS_BF16, nf, ch)


def _fnet(p, *, batch, seq, ch=512):
    c = W_HALF
    r = seq // FFT_INNER
    nf = ROWS_BF16
    assert r % nf == 0 and c % ch == 0
    w1, tw, w2, bdc, bds, perm = _fnet_constants(seq, ch)
    const = pl.Buffered(1)
    stage1 = pl.pallas_call(
        _fft1_kernel,
        grid=(batch, FFT_INNER // nf),
        in_specs=[
            pl.BlockSpec((1, r, nf, c), lambda b, t: (b, 0, t, ODD_C_IN)),
            pl.BlockSpec((2 * r * nf, r * nf), lambda b, t: (0, 0), pipeline_mode=const),
        ],
        out_specs=pl.BlockSpec((1, r, 2, nf, c), lambda b, t: (b, 0, 0, t, 0)),
        out_shape=jax.ShapeDtypeStruct((batch, r, 2, FFT_INNER, c), BF16),
        compiler_params=_cparams(("parallel", "parallel")),
        name="fft_stage1",
    )(p.reshape(batch, r, FFT_INNER, p.shape[1]), w1)
    gate_cols = ODD_C_GATE * (c // ch)
    cmat = lambda shape: pl.BlockSpec(shape, lambda b, f, j: (0,) * len(shape), pipeline_mode=const)
    out = pl.pallas_call(
        _fft2_kernel,
        grid=(batch, r // nf, c // ch),
        in_specs=[
            pl.BlockSpec((1, nf, 2, FFT_INNER, ch), lambda b, f, j: (b, f, 0, 0, j)),
            pl.BlockSpec((1, 2, FFT_INNER, nf), lambda b, f, j: (f, 0, 0, 0)),
            cmat((2 * FFT_INNER, 2 * FFT_INNER)), cmat((ch, ch)), cmat((ch, ch)),
            cmat((nf * nf, nf * nf)),
            pl.BlockSpec((1, FFT_INNER, nf, ch), lambda b, f, j: (b, 0, f, gate_cols + j)),
        ],
        out_specs=pl.BlockSpec((1, FFT_INNER, nf, ch), lambda b, f, j: (b, 0, f, j)),
        out_shape=jax.ShapeDtypeStruct((batch, FFT_INNER, r, c), BF16),
        scratch_shapes=[pltpu.VMEM((2, nf * FFT_INNER, ch), BF16),
                        pltpu.VMEM((nf * FFT_INNER, ch), BF16)],
        compiler_params=_cparams(("parallel", "parallel", "parallel")),
        name="fft_stage2",
    )(stage1, tw, w2, bdc, bds, perm, p.reshape(batch, FFT_INNER, r, p.shape[1]))
    return out.reshape(batch * seq, c)


def _sgu_kernel(u_ref, v_ref, gate_ref, lg_ref, lb_ref, ws_ref, bt_ref, o_ref, *, tr):
    v = v_ref[...].astype(F32)
    mu = jnp.mean(v, axis=-1, keepdims=True)
    vc = v - mu
    var = jnp.mean(vc * vc, axis=-1, keepdims=True)
    vn = (vc * lax.rsqrt(var + EPS) * lg_ref[...] + lb_ref[...]).astype(BF16)
    for g in range(SGU_GROUPS):
        cols = slice(g * SGU_DG, (g + 1) * SGU_DG)
        bcol = jnp.broadcast_to(bt_ref[:, g:g + 1], (SGU_CHUNK, SGU_DG))
        for n in range(tr // SGU_CHUNK):
            rows = slice(n * SGU_CHUNK, (n + 1) * SGU_CHUNK)
            sv = jnp.dot(ws_ref[g], vn[rows, cols], preferred_element_type=F32) + bcol
            out = u_ref[rows, cols].astype(F32) * sv * _silu(gate_ref[rows, cols].astype(F32))
            o_ref[rows, cols] = out.astype(BF16)


def _sgu_inputs(p, ln_g, ln_b, ws_bf16, b_t, *, tr):
    c = W_HALF

    def seg(s):
        return pl.BlockSpec((tr, c), lambda i: (i, s))

    vec = pl.BlockSpec((1, c), lambda i: (0, 0))
    specs = [seg(ODD_U), seg(ODD_V), seg(ODD_D_GATE), vec, vec,
             pl.BlockSpec((SGU_GROUPS, SGU_CHUNK, SGU_CHUNK), lambda i: (0, 0, 0)),
             pl.BlockSpec((SGU_CHUNK, SGU_GROUPS), lambda i: (0, 0))]
    return [p, p, p, ln_g, ln_b, ws_bf16, b_t], specs


QK_SCALE_LOG2 = LOG2E / math.sqrt(DH)


def _attention_bias(p, *, tk):
    return _bias_tiles(p["rel_bias"], p["q_norm_g"], p["k_norm_g"], tk=tk, q_scale=QK_SCALE_LOG2)


def _trunk(x, p, bias_tiles, exp_range):
    batch, seq, d = x.shape
    x2d = x.reshape(batch * seq, d)
    row = lambda a: a.reshape(1, -1)

    lam_init = 0.8 - 0.6 * math.exp(-0.3 * 0)
    pe = _inproj(x2d, row(p["norm_g"][0]), p["w_in_even"], tm=INPROJ_TM, tn=INPROJ_TN_EVEN)
    kx = _knorm(pe, row(p["k_norm_g"]), tr=KNORM_TR)
    mix_b, mix_a = _attention_and_conv(
        exp_range, pe, kx, bias_tiles, row(p["q_norm_g"]), row(p["lam_q1"]), row(p["lam_k1"]),
        row(p["lam_q2"]), row(p["lam_k2"]), row(p["subln_g"]), p["conv_w"], row(p["conv_b"]),
        row(p["conv_ln_g"]), row(p["conv_ln_b"]), batch=batch, seq=seq, tq=ATTN_TQ, tk=ATTN_TK,
        lam_init=lam_init, q_scale=QK_SCALE_LOG2)
    x1, h1 = _outproj(mix_a, mix_b, p["w_out_even"], x2d, tm=OUTPROJ_TM,
                      next_norm_gain=row(p["norm_g"][1]))

    po = _inproj_normalised(h1, p["w_in_odd"], tm=INPROJ_TM, tn=INPROJ_TN_ODD)
    mix_c = _fnet(po, batch=batch, seq=seq)
    sgu = _sgu_inputs(po, row(p["sgu_ln_g"]), row(p["sgu_ln_b"]), p["sgu_w"], p["sgu_b"].T,
                      tr=OUTPROJ_TM)
    y = _outproj(mix_c, None, p["w_out_odd"], x1, tm=OUTPROJ_TM, sgu=sgu)
    return y.reshape(batch, seq, d)


def kernel(x_prompt, x_sample, norm_g, w_in_even, conv_w, conv_b, conv_ln_g, conv_ln_b,
           q_norm_g, k_norm_g, lam_q1, lam_k1, lam_q2, lam_k2, subln_g, rel_bias, w_out_even,
           w_in_odd, sgu_ln_g, sgu_ln_b, sgu_w, sgu_b, w_out_odd):
    p = dict(
        norm_g=norm_g, w_in_even=w_in_even[0].astype(BF16), conv_w=conv_w[0], conv_b=conv_b[0],
        conv_ln_g=conv_ln_g[0], conv_ln_b=conv_ln_b[0], q_norm_g=q_norm_g[0],
        k_norm_g=k_norm_g[0], lam_q1=lam_q1[0], lam_k1=lam_k1[0], lam_q2=lam_q2[0],
        lam_k2=lam_k2[0], subln_g=subln_g[0], w_out_even=w_out_even[0],
        w_in_odd=w_in_odd[0], sgu_ln_g=sgu_ln_g[0], sgu_ln_b=sgu_ln_b[0],
        sgu_w=sgu_w[0].astype(BF16), sgu_b=sgu_b[0], w_out_odd=w_out_odd[0],
        rel_bias=rel_bias)
    bias_tiles, exp_range = _attention_bias(p, tk=ATTN_TK)
    y_prompt = _trunk(x_prompt, p, bias_tiles, exp_range)
    y_sample = _trunk(x_sample, p, bias_tiles, exp_range)
    return (y_prompt, y_sample)
```

```python
import functools
import math

import numpy as np
import jax
import jax.numpy as jnp
from jax import lax
from jax.experimental import pallas as pl
from jax.experimental.pallas import tpu as pltpu

F32 = jnp.float32
BF16 = jnp.bfloat16

EPS = 1e-6
LOG2E = math.log2(math.e)

D_MODEL = 2048
W_HALF = D_MODEL // 2
DH = 128
N_HEADS = 4
DV = 2 * DH
CONV_W = 31
CONV_PAD = CONV_W // 2
N_BUCKETS = 32
MAX_DIST = 128
FFT_INNER = 128
GROUP_C = 128
SGU_GROUPS = 4
SGU_CHUNK = 128
SGU_DG = W_HALF // SGU_GROUPS

EVEN_A_LO, EVEN_A_HI, EVEN_A_GATE, EVEN_Q, EVEN_K, EVEN_V, EVEN_B_GATE = range(7)
ODD_C_IN, ODD_C_GATE, ODD_U, ODD_V, ODD_D_GATE = range(5)

VMEM_LIMIT_V7X = 56 * 1024 * 1024
HALO = 16
SUBLANES = 8

INPROJ_TM = 1024
INPROJ_TN_EVEN = 1792
INPROJ_TN_ODD = 1280
OUTPROJ_TM = 512
KNORM_TR = 1024
ATTN_TQ = 1024
ATTN_TK = 512


def _cparams(sem):
    return pltpu.CompilerParams(dimension_semantics=sem, vmem_limit_bytes=VMEM_LIMIT_V7X)


def _sigmoid(x):
    return 0.5 * jnp.tanh(0.5 * x) + 0.5


def _silu(x):
    return x * _sigmoid(x)


def _inproj_kernel(x_ref, g_ref, w_ref, o_ref, h_ref):
    @pl.when(pl.program_id(1) == 0)
    def _():
        x = x_ref[...]
        ms = jnp.mean(x * x, axis=-1, keepdims=True)
        h_ref[...] = (x * lax.rsqrt(ms + EPS) * g_ref[...]).astype(BF16)

    o_ref[...] = jnp.dot(h_ref[...], w_ref[...], preferred_element_type=F32).astype(BF16)


def _inproj(x2d, g, w_bf16, *, tm, tn):
    n, d = x2d.shape
    p = w_bf16.shape[1]
    return pl.pallas_call(
        _inproj_kernel,
        grid=(n // tm, p // tn),
        in_specs=[
            pl.BlockSpec((tm, d), lambda i, j: (i, 0)),
            pl.BlockSpec((1, d), lambda i, j: (0, 0)),
            pl.BlockSpec((d, tn), lambda i, j: (0, j)),
        ],
        out_specs=pl.BlockSpec((tm, tn), lambda i, j: (i, j)),
        out_shape=jax.ShapeDtypeStruct((n, p), BF16),
        scratch_shapes=[pltpu.VMEM((tm, d), BF16)],
        compiler_params=_cparams(("parallel", "arbitrary")),
        name="inproj",
    )(x2d, g, w_bf16)


def _matmul_kernel(h_ref, w_ref, o_ref, wb_ref):
    @pl.when(pl.program_id(1) == 0)
    def _():
        wb_ref[...] = w_ref[...].astype(BF16)

    o_ref[...] = jnp.dot(h_ref[...], wb_ref[...], preferred_element_type=F32).astype(BF16)


def _inproj_normalised(h, w_f32, *, tm, tn):
    n, d = h.shape
    p = w_f32.shape[1]
    return pl.pallas_call(
        _matmul_kernel,
        grid=(p // tn, n // tm),
        in_specs=[
            pl.BlockSpec((tm, d), lambda j, i: (i, 0)),
            pl.BlockSpec((d, tn), lambda j, i: (0, j)),
        ],
        out_specs=pl.BlockSpec((tm, tn), lambda j, i: (i, j)),
        out_shape=jax.ShapeDtypeStruct((n, p), BF16),
        scratch_shapes=[pltpu.VMEM((d, tn), BF16)],
        compiler_params=_cparams(("arbitrary", "arbitrary")),
        name="inproj",
    )(h, w_f32)


def _outproj_kernel(ma_ref, *rest, n_sgu_inputs, emit_norm):
    rest = list(rest)
    second = [rest.pop(0) for _ in range(n_sgu_inputs or 1)]
    wa_ref, wb_ref, x_ref = rest[:3]
    rest = rest[3:]
    gain_ref = rest.pop(0) if emit_norm else None
    o_ref = rest.pop(0)
    h_ref = rest.pop(0) if emit_norm else None
    w_ref = rest.pop(0)
    mb_ref = rest.pop(0) if n_sgu_inputs else second[0]

    @pl.when(pl.program_id(0) == 0)
    def _():
        w_ref[0] = wa_ref[...].astype(BF16)
        w_ref[1] = wb_ref[...].astype(BF16)

    acc = jnp.dot(ma_ref[...], w_ref[0], preferred_element_type=F32)
    if n_sgu_inputs:
        _sgu_kernel(*second, mb_ref, tr=mb_ref.shape[0])
    acc = acc + jnp.dot(mb_ref[...], w_ref[1], preferred_element_type=F32)
    y = x_ref[...] + acc
    o_ref[...] = y
    if emit_norm:
        ms = jnp.mean(y * y, axis=-1, keepdims=True)
        h_ref[...] = (y * lax.rsqrt(ms + EPS) * gain_ref[...]).astype(BF16)


def _outproj(mix_a, mix_b, w_f32, x2d, *, tm, sgu=None, next_norm_gain=None):
    n, d = x2d.shape
    half = mix_a.shape[1]
    const = pl.Buffered(1)
    rows = pl.BlockSpec((tm, half), lambda i: (i, 0))
    full = pl.BlockSpec((tm, d), lambda i: (i, 0))
    emit_norm = next_norm_gain is not None
    scratch = [pltpu.VMEM((2, half, d), BF16)]
    if sgu is None:
        second, second_specs = [mix_b], [rows]
    else:
        second, second_specs = sgu
        scratch.append(pltpu.VMEM((tm, half), BF16))
    gain, gain_spec = ([next_norm_gain], [pl.BlockSpec((1, d), lambda i: (0, 0))]) if emit_norm \
        else ([], [])
    out = pl.pallas_call(
        functools.partial(_outproj_kernel, n_sgu_inputs=0 if sgu is None else len(second),
                          emit_norm=emit_norm),
        grid=(n // tm,),
        in_specs=[rows] + second_specs + [
            pl.BlockSpec((half, d), lambda i: (0, 0), pipeline_mode=const),
            pl.BlockSpec((half, d), lambda i: (1, 0), pipeline_mode=const),
            full,
        ] + gain_spec,
        out_specs=[full] + ([full] if emit_norm else []),
        out_shape=[jax.ShapeDtypeStruct((n, d), F32)]
        + ([jax.ShapeDtypeStruct((n, d), BF16)] if emit_norm else []),
        scratch_shapes=scratch,
        compiler_params=_cparams(("arbitrary",)),
        name="outproj",
    )(mix_a, *second, w_f32, w_f32, x2d, *gain)
    return out if emit_norm else out[0]


CONV_TS = 256
CONV_RC = 32
CONV_LANE_PARTS = 4


class ConvRefs:
    def __init__(self, lo, hi, plo, phi, nlo, nhi, gate, w, b, lg, lb, out, xs, wb, y):
        self.lo, self.hi, self.plo, self.phi, self.nlo, self.nhi = lo, hi, plo, phi, nlo, nhi
        self.gate, self.w, self.b, self.lg, self.lb, self.out = gate, w, b, lg, lb, out
        self.xs, self.wb, self.y = xs, wb, y


def _conv_glu(cr, first, last):
    ts, c = cr.y.shape
    padded = ts + 2 * HALO

    def glu(a, b):
        return a.astype(F32) * _sigmoid(b.astype(F32))

    cr.xs[0, HALO:HALO + ts, :] = glu(cr.lo[...], cr.hi[...])
    cr.xs[0, 0:HALO, :] = jnp.where(first, 0.0, glu(cr.plo[...], cr.phi[...]))
    cr.xs[0, HALO + ts:padded, :] = jnp.where(last, 0.0, glu(cr.nlo[...], cr.nhi[...]))
    cr.xs[0, padded:padded + SUBLANES, :] = jnp.zeros((SUBLANES, c), F32)
    for k in range(CONV_W):
        cr.wb[k] = jnp.broadcast_to(cr.w[k:k + 1, :], (SUBLANES, c))


def _conv_shift(cr, q):
    c = cr.y.shape[1]
    lw = c // CONV_LANE_PARTS
    row = pl.multiple_of(q * CONV_RC, CONV_RC)
    for col in range(0, c, lw):
        win = cr.xs[0, pl.ds(row, CONV_RC + SUBLANES), col:col + lw]
        for r in range(1, SUBLANES):
            cr.xs[r, pl.ds(row, CONV_RC), col:col + lw] = pltpu.roll(
                win, CONV_RC + SUBLANES - r, axis=0)[:CONV_RC]


CONV_GROUPS = CONV_RC // SUBLANES
CONV_GROUP_PARTS = 1
CONV_PIECES = CONV_LANE_PARTS * CONV_GROUP_PARTS


def _zero_token(x):
    bits = pltpu.bitcast(x[:SUBLANES, :DH], jnp.uint32)
    return pltpu.bitcast(lax.shift_right_logical(bits, jnp.uint32(32)), F32)


def _conv_tap_piece(cr, q, piece, token=None):
    c = cr.y.shape[1]
    lw = c // CONV_LANE_PARTS
    base0 = HALO - CONV_PAD
    part, gpart = divmod(piece, CONV_GROUP_PARTS)
    per = CONV_GROUPS // CONV_GROUP_PARTS
    groups = range(gpart * per, (gpart + 1) * per)
    col = part * lw
    row = pl.multiple_of(q * CONV_RC, CONV_RC)
    start = jnp.zeros((SUBLANES, lw), F32) if token is None else jnp.concatenate(
        [token] * (lw // DH), axis=-1)
    acc = {g: start for g in groups}
    for r in range(SUBLANES):
        taps = [(a, SUBLANES * a + r - base0) for a in range((base0 + CONV_W - 1) // SUBLANES + 1)
                if 0 <= SUBLANES * a + r - base0 < CONV_W]
        x = {s: cr.xs[r, pl.ds(row + SUBLANES * s, SUBLANES), col:col + lw]
             for s in sorted({a + g for a, _ in taps for g in groups})}
        for a, k in taps:
            wk = cr.wb[k, :, col:col + lw]
            for g in groups:
                acc[g] = acc[g] + x[a + g] * wk
    bias = cr.b[:, col:col + lw]
    for g in groups:
        cr.y[pl.ds(row + SUBLANES * g, SUBLANES), col:col + lw] = acc[g] + bias


def _conv_taps(cr, q):
    for piece in range(CONV_PIECES):
        _conv_tap_piece(cr, q, piece)


def _conv_norm(cr, q):
    row = pl.multiple_of(q * CONV_RC, CONV_RC)
    y = cr.y[pl.ds(row, CONV_RC), :]
    mu = jnp.mean(y, axis=-1, keepdims=True)
    yc = y - mu
    var = jnp.mean(yc * yc, axis=-1, keepdims=True)
    z = yc * lax.rsqrt(var + EPS) * cr.lg[...] + cr.lb[...]
    out = _silu(z) * _silu(cr.gate[pl.ds(row, CONV_RC), :].astype(F32))
    cr.out[pl.ds(row, CONV_RC), :] = out.astype(BF16)


def _conv_phase(cr, chunk_fn, n_chunks, unroll=1):
    def body(q, carry):
        chunk_fn(cr, q)
        return carry
    lax.fori_loop(0, n_chunks, body, 0, unroll=unroll)


def _conv_specs(index_of_step, n_rows):
    c = W_HALF
    hb = CONV_TS // HALO
    nhb = n_rows // HALO

    def rows(col):
        return pl.BlockSpec((CONV_TS, c), lambda *g: (index_of_step(*g), col))

    def prev(col):
        return pl.BlockSpec((HALO, c), lambda *g: (jnp.maximum(index_of_step(*g) * hb - 1, 0), col))

    def nxt(col):
        return pl.BlockSpec(
            (HALO, c), lambda *g: (jnp.minimum((index_of_step(*g) + 1) * hb, nhb - 1), col))

    vec = pl.BlockSpec((1, c), lambda *g: (0, 0))
    in_specs = [rows(EVEN_A_LO), rows(EVEN_A_HI), prev(EVEN_A_LO), prev(EVEN_A_HI),
                nxt(EVEN_A_LO), nxt(EVEN_A_HI), rows(EVEN_A_GATE),
                pl.BlockSpec((CONV_W, c), lambda *g: (0, 0)), vec, vec, vec]
    out_spec = pl.BlockSpec((CONV_TS, c), lambda *g: (index_of_step(*g), 0))
    scratch = [pltpu.VMEM((SUBLANES, CONV_TS + 2 * HALO + SUBLANES, c), F32),
               pltpu.VMEM((CONV_W, SUBLANES, c), F32),
               pltpu.VMEM((CONV_TS, c), F32)]
    return in_specs, out_spec, scratch


def _bucket_thresholds():
    nb = N_BUCKETS // 2
    max_exact = nb // 2
    n = np.arange(1, 4 * MAX_DIST, dtype=np.float64)
    large = max_exact + (np.log(n / max_exact) / math.log(MAX_DIST / max_exact)
                         * (nb - max_exact)).astype(np.int64)
    large = np.minimum(large, nb - 1)
    thr = [int(n[np.argmax(large >= b)]) for b in range(max_exact + 1, nb)]
    return max_exact, thr


def _bias_kernel(rb_ref, qg_ref, kg_ref, o_ref, range_ref, *, tk, lo_diag, q_scale):
    h = pl.program_id(0)
    d = pl.program_id(1) + lo_diag
    gq = jnp.abs(qg_ref[0])
    gk = jnp.abs(kg_ref[0])
    for c in range(1, DH):
        gq = jnp.maximum(gq, jnp.abs(qg_ref[c]))
        gk = jnp.maximum(gk, jnp.abs(kg_ref[c]))
    qk_bound = gq * gk * (DH * q_scale * BF16_SLACK)
    bmax = rb_ref[0, h]
    bmin = rb_ref[0, h]
    for b in range(1, N_BUCKETS):
        bmax = jnp.maximum(bmax, rb_ref[b, h])
        bmin = jnp.minimum(bmin, rb_ref[b, h])
    shift = qk_bound + bmax * LOG2E
    range_ref[h] = 2.0 * qk_bound + (bmax - bmin) * LOG2E

    @pl.when(jnp.abs(d) < FAR)
    def _():
        row = lax.broadcasted_iota(jnp.int32, (tk, tk), 0)
        col = lax.broadcasted_iota(jnp.int32, (tk, tk), 1)
        rel = col - row + d * tk
        n = jnp.abs(rel)
        max_exact, thr = _bucket_thresholds()
        bucket = jnp.minimum(n, max_exact)
        for t in thr:
            bucket = bucket + jnp.where(n >= t, 1, 0)
        bucket = bucket + jnp.where(rel > 0, N_BUCKETS // 2, 0)
        level = [rb_ref[b, h] for b in range(N_BUCKETS)]
        bit = 1
        while len(level) > 1:
            odd = (bucket & bit) != 0
            level = [jnp.where(odd, level[i + 1], level[i]) for i in range(0, len(level), 2)]
            bit *= 2
        o_ref[0, 0] = level[0] * LOG2E - shift

    @pl.when(d <= -FAR)
    def _():
        o_ref[0, 0] = jnp.full((tk, tk), rb_ref[N_BUCKETS // 2 - 1, h] * LOG2E - shift, F32)

    @pl.when(d >= FAR)
    def _():
        o_ref[0, 0] = jnp.full((tk, tk), rb_ref[N_BUCKETS - 1, h] * LOG2E - shift, F32)


EXP2_SAFE_RANGE = 100.0
BF16_SLACK = 1.01
N_DIAG = 5
FAR = N_DIAG // 2


def _bias_tiles(rel_bias, q_gain, k_gain, *, tk, q_scale):
    assert tk + 1 >= _bucket_thresholds()[1][-1]
    nd = N_DIAG
    smem = pl.BlockSpec(memory_space=pltpu.SMEM)
    kern = functools.partial(_bias_kernel, tk=tk, lo_diag=-FAR, q_scale=q_scale)
    return pl.pallas_call(
        kern,
        grid=(N_HEADS, nd),
        in_specs=[smem, smem, smem],
        out_specs=[pl.BlockSpec((1, 1, tk, tk), lambda h, d: (h, d, 0, 0)), smem],
        out_shape=[jax.ShapeDtypeStruct((N_HEADS, nd, tk, tk), F32),
                   jax.ShapeDtypeStruct((N_HEADS,), F32)],
        compiler_params=_cparams(("arbitrary", "arbitrary")),
        name="bias_tiles",
    )(rel_bias, q_gain, k_gain)


def _sub_head_norm(x, gain, scale):
    parts = []
    for t in range(x.shape[1] // DH):
        blk = x[:, t * DH:(t + 1) * DH]
        ms = jnp.mean(blk * blk, axis=-1, keepdims=True)
        parts.append(blk * lax.rsqrt(ms + EPS) * gain * scale)
    return jnp.concatenate(parts, axis=-1)


def _knorm_kernel(k_ref, g_ref, o_ref):
    o_ref[...] = _sub_head_norm(k_ref[...].astype(F32), g_ref[...], 1.0).astype(BF16)


def _knorm(p, k_gain, *, tr):
    n = p.shape[0]
    return pl.pallas_call(
        _knorm_kernel,
        grid=(n // tr,),
        in_specs=[pl.BlockSpec((tr, W_HALF), lambda i: (i, EVEN_K)),
                  pl.BlockSpec((1, DH), lambda i: (0, 0))],
        out_specs=pl.BlockSpec((tr, W_HALF), lambda i: (i, 0)),
        out_shape=jax.ShapeDtypeStruct((n, W_HALF), BF16),
        compiler_params=_cparams(("parallel",)),
        name="knorm",
    )(p, k_gain)


def _attn_kernel(range_ref, q_ref, k_ref, v_ref, bias_ref, gate_ref, qg_ref, lq1_ref, lk1_ref,
                 lq2_ref, lk2_ref, sg_ref, *rest, tq, tk, nkv, lam_init, q_scale, unroll,
                 conv_tiles_per_seq):
    conv_in = rest[:11]
    o_ref, conv_out = rest[11:13]
    qx_ref, acc_ref, ls_ref, l_ref = rest[13:17]
    cr = ConvRefs(*conv_in, conv_out, *rest[17:20])
    h = pl.program_id(1)
    i = pl.program_id(2)
    contract_last = (((1,), (1,)), ((), ()))

    step = (pl.program_id(0) * pl.num_programs(1) + h) * pl.num_programs(2) + i
    seq_pos = step % conv_tiles_per_seq
    _conv_glu(cr, seq_pos == 0, seq_pos == conv_tiles_per_seq - 1)
    conv_chunks = CONV_TS // CONV_RC

    qx_ref[...] = _sub_head_norm(q_ref[...].astype(F32), qg_ref[...], q_scale).astype(BF16)
    acc_ref[...] = jnp.zeros_like(acc_ref)

    nsub = tq // tk

    def shifted_scores(j, t):
        rows = pl.ds(pl.multiple_of(j * tk, tk), tk)
        s = lax.dot_general(qx_ref[:, t * DH:(t + 1) * DH], k_ref[rows, t * DH:(t + 1) * DH],
                            contract_last, preferred_element_type=F32)
        parts = []
        for a in range(nsub):
            diag = jnp.clip(j - (i * nsub + a), -FAR, FAR) + FAR
            parts.append(s[a * tk:(a + 1) * tk] + bias_ref[0, diag])
        return jnp.concatenate(parts, axis=0)

    def bounded():
        ls_ref[...] = jnp.zeros_like(ls_ref)
        trips = nkv // unroll
        chunks_per_trip = conv_chunks // trips
        _conv_phase(cr, _conv_shift, conv_chunks + 1)
        pieces = [(cc, piece) for cc in range(chunks_per_trip) for piece in range(CONV_PIECES)]
        per_softmax = len(pieces) // (2 * unroll)
        assert per_softmax * 2 * unroll == len(pieces)

        def body(jj, carry):
            for u in range(unroll):
                j = unroll * jj + u
                vv = v_ref[pl.ds(pl.multiple_of(j * tk, tk), tk), :]
                for t in range(2):
                    p = jnp.exp2(shifted_scores(j, t))
                    lane_sums = p[:, :DH]
                    for c in range(1, tk // DH):
                        lane_sums = lane_sums + p[:, c * DH:(c + 1) * DH]
                    at = (2 * u + t) * per_softmax
                    for n, (cc, piece) in enumerate(pieces[at:at + per_softmax]):
                        r0 = n * (tq // per_softmax)
                        _conv_tap_piece(cr, jj * chunks_per_trip + cc, piece,
                                        _zero_token(lane_sums[r0:r0 + SUBLANES]))
                    ls_ref[t] = ls_ref[t] + lane_sums
                    acc_ref[t] = acc_ref[t] + jnp.dot(p.astype(BF16), vv,
                                                      preferred_element_type=F32)
            return carry

        lax.fori_loop(0, trips, body, 0)
        _conv_phase(cr, _conv_norm, conv_chunks, unroll=2)
        for t in range(2):
            l_ref[t] = jnp.sum(ls_ref[t], axis=-1, keepdims=True)

    def running_max():
        _conv_phase(cr, _conv_shift, conv_chunks + 1)
        _conv_phase(cr, _conv_taps, conv_chunks)
        _conv_phase(cr, _conv_norm, conv_chunks, unroll=2)

        def body(j, ml):
            vv = v_ref[pl.ds(pl.multiple_of(j * tk, tk), tk), :]
            out = []
            for t in range(2):
                m, l = ml[2 * t], ml[2 * t + 1]
                s = shifted_scores(j, t)
                mn = jnp.maximum(m, jnp.max(s, axis=-1, keepdims=True))
                p = jnp.exp2(s - mn)
                a = jnp.exp2(m - mn)
                l = a * l + jnp.sum(p, axis=-1, keepdims=True)
                acc_ref[t] = a * acc_ref[t] + jnp.dot(p.astype(BF16), vv,
                                                      preferred_element_type=F32)
                out += [mn, l]
            return tuple(out)

        neg = jnp.full((tq, 1), -jnp.inf, F32)
        zero = jnp.zeros((tq, 1), F32)
        _, l1, _, l2 = lax.fori_loop(0, nkv, body, (neg, zero, neg, zero))
        l_ref[0] = l1
        l_ref[1] = l2

    lax.cond(range_ref[h] <= EXP2_SAFE_RANGE, bounded, running_max)

    lam = (jnp.exp(jnp.sum(lq1_ref[...] * lk1_ref[...], axis=-1, keepdims=True))
           - jnp.exp(jnp.sum(lq2_ref[...] * lk2_ref[...], axis=-1, keepdims=True)) + lam_init)
    o = acc_ref[0] * (1.0 / l_ref[0]) - lam * (acc_ref[1] * (1.0 / l_ref[1]))
    ms = jnp.mean(o * o, axis=-1, keepdims=True)
    y = o * lax.rsqrt(ms + EPS) * sg_ref[...] * (1.0 - lam_init)
    o_ref[...] = (y * _silu(gate_ref[...].astype(F32))).astype(BF16)


def _attention_and_conv(exp_range, p, kx, bias_tiles, q_gain, lq1, lk1, lq2, lk2, subln_g, conv_w,
                        conv_b, conv_ln_g, conv_ln_b, *, batch, seq, tq, tk, lam_init, q_scale):
    n = p.shape[0]
    nq = seq // tq
    nd = bias_tiles.shape[1]
    nkv = seq // tk
    unroll = 8
    trips = nkv // unroll
    conv_chunks = CONV_TS // CONV_RC
    assert tq % tk == 0 and bias_tiles.shape[2:] == (tk, tk) and nkv % unroll == 0
    assert batch * N_HEADS * nq * CONV_TS == n and conv_chunks % trips == 0
    assert seq % CONV_TS == 0 and (CONV_TS + 2 * HALO) % CONV_RC == 0
    per_head = W_HALF // DV

    def q_rows(seg):
        return pl.BlockSpec((tq, DV), lambda b, h, i: (b * nq + i, seg * per_head + h))

    vec = pl.BlockSpec((1, DH), lambda b, h, i: (0, 0))
    conv_in, conv_out, conv_scratch = _conv_specs(lambda b, h, i: (b * N_HEADS + h) * nq + i, n)
    kern = functools.partial(_attn_kernel, tq=tq, tk=tk, nkv=nkv, lam_init=lam_init,
                             q_scale=q_scale, unroll=unroll, conv_tiles_per_seq=seq // CONV_TS)
    return pl.pallas_call(
        kern,
        grid=(batch, N_HEADS, nq),
        in_specs=[
            pl.BlockSpec(memory_space=pltpu.SMEM),
            q_rows(EVEN_Q),
            pl.BlockSpec((seq, DV), lambda b, h, i: (b, h)),
            pl.BlockSpec((seq, DV), lambda b, h, i: (b, EVEN_V * per_head + h)),
            pl.BlockSpec((1, nd, tk, tk), lambda b, h, i: (h, 0, 0, 0),
                         pipeline_mode=pl.Buffered(1)),
            q_rows(EVEN_B_GATE),
            vec, vec, vec, vec, vec,
            pl.BlockSpec((1, DV), lambda b, h, i: (0, 0)),
        ] + conv_in,
        out_specs=[pl.BlockSpec((tq, DV), lambda b, h, i: (b * nq + i, h)), conv_out],
        out_shape=[jax.ShapeDtypeStruct((n, N_HEADS * DV), BF16),
                   jax.ShapeDtypeStruct((n, W_HALF), BF16)],
        scratch_shapes=[
            pltpu.VMEM((tq, DV), BF16),
            pltpu.VMEM((2, tq, DV), F32),
            pltpu.VMEM((2, tq, DH), F32),
            pltpu.VMEM((2, tq, 1), F32),
        ] + conv_scratch,
        compiler_params=_cparams(("parallel", "parallel", "parallel")),
        name="diff_attention",
    )(exp_range, p, kx, p, bias_tiles, p, q_gain, lq1, lk1, lq2, lk2, subln_g,
      p, p, p, p, p, p, p, conv_w, conv_b, conv_ln_g, conv_ln_b)


ROWS_BF16 = 16


def _fnet_constants(seq, ch):
    r = seq // FFT_INNER
    two_pi = 2.0 * np.pi
    ang_r = two_pi * np.outer(np.arange(r), np.arange(r)) / r
    base = np.stack([np.cos(ang_r), -np.sin(ang_r)], axis=1).reshape(2 * r, r)
    w1 = np.kron(base, np.eye(ROWS_BF16))
    ang_tw = two_pi * np.outer(np.arange(FFT_INNER), np.arange(r)) / seq
    tw = np.stack([np.cos(ang_tw), np.sin(ang_tw)])
    tw = tw.reshape(2, FFT_INNER, r // ROWS_BF16, ROWS_BF16).transpose(2, 0, 1, 3)
    ang_i = two_pi * np.outer(np.arange(FFT_INNER), np.arange(FFT_INNER)) / FFT_INNER
    c, s = np.cos(ang_i), np.sin(ang_i)
    w2 = np.block([[c, s], [-s, c]])
    scale = 1.0 / math.sqrt(seq * GROUP_C)
    eye_g = np.eye(ch // GROUP_C)
    bdc = np.kron(eye_g, c * scale)
    bds = np.kron(eye_g, s * scale)
    perm = np.zeros((ROWS_BF16 * ROWS_BF16,) * 2)
    for b in range(ROWS_BF16):
        for f in range(ROWS_BF16):
            perm[b * ROWS_BF16 + f, f * ROWS_BF16 + b] = 1.0
    as_bf16 = lambda a: jnp.asarray(a, dtype=BF16)
    return (as_bf16(w1), jnp.asarray(tw, dtype=F32), as_bf16(w2), as_bf16(bdc), as_bf16(bds),
            as_bf16(perm))


def _fft1_kernel(x_ref, w_ref, o_ref):
    _, r, rows, c = x_ref.shape
    x = x_ref[0].reshape(r * rows, c)
    y = jnp.dot(w_ref[...], x, preferred_element_type=F32).astype(BF16)
    o_ref[0] = y.reshape(r, 2, rows, c)


def _fft2_kernel(a_ref, tw_ref, w2_ref, bdc_ref, bds_ref, perm_ref, gate_ref, o_ref, u_ref, y_ref):
    nf = a_ref.shape[1]
    ch = a_ref.shape[-1]
    for kk in range(nf):
        br = a_ref[0, kk, 0].astype(F32)
        bi = a_ref[0, kk, 1].astype(F32)
        cw = tw_ref[0, 0][:, kk:kk + 1]
        sw = tw_ref[0, 1][:, kk:kk + 1]
        x = jnp.concatenate([br * cw + bi * sw, bi * cw - br * sw], axis=0).astype(BF16)
        u = jnp.dot(w2_ref[...], x, preferred_element_type=F32).astype(BF16)
        u_ref[0, kk * FFT_INNER:(kk + 1) * FFT_INNER, :] = u[:FFT_INNER]
        u_ref[1, kk * FFT_INNER:(kk + 1) * FFT_INNER, :] = u[FFT_INNER:]
    y_ref[...] = (jnp.dot(u_ref[0], bdc_ref[...], preferred_element_type=F32)
                  + jnp.dot(u_ref[1], bds_ref[...], preferred_element_type=F32)).astype(BF16)
    for a in range(FFT_INNER // ROWS_BF16):
        lo = a * ROWS_BF16
        piece = jnp.concatenate(
            [y_ref[k * FFT_INNER + lo:k * FFT_INNER + lo + ROWS_BF16, :] for k in range(nf)], axis=0)
        z = jnp.dot(perm_ref[...], piece, preferred_element_type=F32)
        g = gate_ref[0, lo:lo + ROWS_BF16].reshape(ROWS_BF16 * nf, ch).astype(F32)
        o_ref[0, lo:lo + ROWS_BF16] = (z * _silu(g)).astype(BF16).reshape(ROWS_BF16, nf, ch)


def _fnet(p, *, batch, seq, ch=512):
    c = W_HALF
    r = seq // FFT_INNER
    nf = ROWS_BF16
    assert r % nf == 0 and c % ch == 0
    w1, tw, w2, bdc, bds, perm = _fnet_constants(seq, ch)
    const = pl.Buffered(1)
    stage1 = pl.pallas_call(
        _fft1_kernel,
        grid=(batch, FFT_INNER // nf),
        in_specs=[
            pl.BlockSpec((1, r, nf, c), lambda b, t: (b, 0, t, ODD_C_IN)),
            pl.BlockSpec((2 * r * nf, r * nf), lambda b, t: (0, 0), pipeline_mode=const),
        ],
        out_specs=pl.BlockSpec((1, r, 2, nf, c), lambda b, t: (b, 0, 0, t, 0)),
        out_shape=jax.ShapeDtypeStruct((batch, r, 2, FFT_INNER, c), BF16),
        compiler_params=_cparams(("parallel", "parallel")),
        name="fft_stage1",
    )(p.reshape(batch, r, FFT_INNER, p.shape[1]), w1)
    gate_cols = ODD_C_GATE * (c // ch)
    cmat = lambda shape: pl.BlockSpec(shape, lambda b, f, j: (0,) * len(shape), pipeline_mode=const)
    out = pl.pallas_call(
        _fft2_kernel,
        grid=(batch, r // nf, c // ch),
        in_specs=[
            pl.BlockSpec((1, nf, 2, FFT_INNER, ch), lambda b, f, j: (b, f, 0, 0, j)),
            pl.BlockSpec((1, 2, FFT_INNER, nf), lambda b, f, j: (f, 0, 0, 0)),
            cmat((2 * FFT_INNER, 2 * FFT_INNER)), cmat((ch, ch)), cmat((ch, ch)),
            cmat((nf * nf, nf * nf)),
            pl.BlockSpec((1, FFT_INNER, nf, ch), lambda b, f, j: (b, 0, f, gate_cols + j)),
        ],
        out_specs=pl.BlockSpec((1, FFT_INNER, nf, ch), lambda b, f, j: (b, 0, f, j)),
        out_shape=jax.ShapeDtypeStruct((batch, FFT_INNER, r, c), BF16),
        scratch_shapes=[pltpu.VMEM((2, nf * FFT_INNER, ch), BF16),
                        pltpu.VMEM((nf * FFT_INNER, ch), BF16)],
        compiler_params=_cparams(("parallel", "parallel", "parallel")),
        name="fft_stage2",
    )(stage1, tw, w2, bdc, bds, perm, p.reshape(batch, FFT_INNER, r, p.shape[1]))
    return out.reshape(batch * seq, c)


def _sgu_kernel(u_ref, v_ref, gate_ref, lg_ref, lb_ref, ws_ref, bt_ref, o_ref, *, tr):
    v = v_ref[...].astype(F32)
    mu = jnp.mean(v, axis=-1, keepdims=True)
    vc = v - mu
    var = jnp.mean(vc * vc, axis=-1, keepdims=True)
    vn = (vc * lax.rsqrt(var + EPS) * lg_ref[...] + lb_ref[...]).astype(BF16)
    for g in range(SGU_GROUPS):
        cols = slice(g * SGU_DG, (g + 1) * SGU_DG)
        bcol = jnp.broadcast_to(bt_ref[:, g:g + 1], (SGU_CHUNK, SGU_DG))
        for n in range(tr // SGU_CHUNK):
            rows = slice(n * SGU_CHUNK, (n + 1) * SGU_CHUNK)
            sv = jnp.dot(ws_ref[g], vn[rows, cols], preferred_element_type=F32) + bcol
            out = u_ref[rows, cols].astype(F32) * sv * _silu(gate_ref[rows, cols].astype(F32))
            o_ref[rows, cols] = out.astype(BF16)


def _sgu_inputs(p, ln_g, ln_b, ws_bf16, b_t, *, tr):
    c = W_HALF

    def seg(s):
        return pl.BlockSpec((tr, c), lambda i: (i, s))

    vec = pl.BlockSpec((1, c), lambda i: (0, 0))
    specs = [seg(ODD_U), seg(ODD_V), seg(ODD_D_GATE), vec, vec,
             pl.BlockSpec((SGU_GROUPS, SGU_CHUNK, SGU_CHUNK), lambda i: (0, 0, 0)),
             pl.BlockSpec((SGU_CHUNK, SGU_GROUPS), lambda i: (0, 0))]
    return [p, p, p, ln_g, ln_b, ws_bf16, b_t], specs


QK_SCALE_LOG2 = LOG2E / math.sqrt(DH)


def _attention_bias(p, *, tk):
    return _bias_tiles(p["rel_bias"], p["q_norm_g"], p["k_norm_g"], tk=tk, q_scale=QK_SCALE_LOG2)


def _trunk(x, p, bias_tiles, exp_range):
    batch, seq, d = x.shape
    x2d = x.reshape(batch * seq, d)
    row = lambda a: a.reshape(1, -1)

    lam_init = 0.8 - 0.6 * math.exp(-0.3 * 0)
    pe = _inproj(x2d, row(p["norm_g"][0]), p["w_in_even"], tm=INPROJ_TM, tn=INPROJ_TN_EVEN)
    kx = _knorm(pe, row(p["k_norm_g"]), tr=KNORM_TR)
    mix_b, mix_a = _attention_and_conv(
        exp_range, pe, kx, bias_tiles, row(p["q_norm_g"]), row(p["lam_q1"]), row(p["lam_k1"]),
        row(p["lam_q2"]), row(p["lam_k2"]), row(p["subln_g"]), p["conv_w"], row(p["conv_b"]),
        row(p["conv_ln_g"]), row(p["conv_ln_b"]), batch=batch, seq=seq, tq=ATTN_TQ, tk=ATTN_TK,
        lam_init=lam_init, q_scale=QK_SCALE_LOG2)
    x1, h1 = _outproj(mix_a, mix_b, p["w_out_even"], x2d, tm=OUTPROJ_TM,
                      next_norm_gain=row(p["norm_g"][1]))

    po = _inproj_normalised(h1, p["w_in_odd"], tm=INPROJ_TM, tn=INPROJ_TN_ODD)
    mix_c = _fnet(po, batch=batch, seq=seq)
    sgu = _sgu_inputs(po, row(p["sgu_ln_g"]), row(p["sgu_ln_b"]), p["sgu_w"], p["sgu_b"].T,
                      tr=OUTPROJ_TM)
    y = _outproj(mix_c, None, p["w_out_odd"], x1, tm=OUTPROJ_TM, sgu=sgu)
    return y.reshape(batch, seq, d)


def kernel(x_prompt, x_sample, norm_g, w_in_even, conv_w, conv_b, conv_ln_g, conv_ln_b,
           q_norm_g, k_norm_g, lam_q1, lam_k1, lam_q2, lam_k2, subln_g, rel_bias, w_out_even,
           w_in_odd, sgu_ln_g, sgu_ln_b, sgu_w, sgu_b, w_out_odd):
    p = dict(
        norm_g=norm_g, w_in_even=w_in_even[0].astype(BF16), conv_w=conv_w[0], conv_b=conv_b[0],
        conv_ln_g=conv_ln_g[0], conv_ln_b=conv_ln_b[0], q_norm_g=q_norm_g[0],
        k_norm_g=k_norm_g[0], lam_q1=lam_q1[0], lam_k1=lam_k1[0], lam_q2=lam_q2[0],
        lam_k2=lam_k2[0], subln_g=subln_g[0], w_out_even=w_out_even[0],
        w_in_odd=w_in_odd[0], sgu_ln_g=sgu_ln_g[0], sgu_ln_b=sgu_ln_b[0],
        sgu_w=sgu_w[0].astype(BF16), sgu_b=sgu_b[0], w_out_odd=w_out_odd[0],
        rel_bias=rel_bias)
    bias_tiles, exp_range = _attention_bias(p, tk=ATTN_TK)
    y_prompt = _trunk(x_prompt, p, bias_tiles, exp_range)
    y_sample = _trunk(x_sample, p, bias_tiles, exp_range)
    return (y_prompt, y_sample)
```
